```python
import math
import jax
import jax.numpy as jnp
from jax import lax
import numpy as np

D_MODEL = 2048
BATCH = 8
SEQ = 2048
DEPTH = 2
DEC_BATCH = 32
DEC_SEQ = 1
PAST_LEN = 8192
PAGE_SIZE = 128

HEAD_DIM = 128
BRANCH_W = D_MODEL // 2
N_BRANCH = 3
NSA_HEADS = BRANCH_W // HEAD_DIM
NSA_KV = max(1, NSA_HEADS // 4)
NSA_REP = NSA_HEADS // NSA_KV
CMP_STRIDE = 16
CMP_LEN = 2 * CMP_STRIDE
CMP_HIDDEN = HEAD_DIM
SEL_BLOCK = 64
N_SEL = 16
WINDOW = 512
DSA_HEADS = BRANCH_W // HEAD_DIM
DSA_KV = max(1, DSA_HEADS // 4)
DSA_REP = DSA_HEADS // DSA_KV
IDX_HEADS = 8
IDX_DIM = 64
DSA_TOPK = 256
CONV_DIM = BRANCH_W
CONV_WIDTH = 3
FFN_HIDDEN = ((8 * D_MODEL + 3 * 256 - 1) // (3 * 256)) * 256
ROPE_THETA = 500000.0
ROPE_FRACTION = 4
RMS_EPS = 1e-6
ATTN_SCALE = HEAD_DIM ** -0.5
IDX_SCALE = IDX_DIM ** -0.5
Q_BLOCK = 128
NEG_INF = -1e30
FORCE_SCORE = 1e30
IN_SPLITS = (NSA_HEADS * HEAD_DIM, 2 * NSA_KV * HEAD_DIM, 2 * NSA_KV * HEAD_DIM, 2 * NSA_KV * HEAD_DIM, 3 * NSA_HEADS,
             DSA_HEADS * HEAD_DIM, 2 * DSA_KV * HEAD_DIM, IDX_HEADS * IDX_DIM, IDX_DIM, IDX_HEADS,
             CONV_DIM, CONV_DIM, CONV_DIM, N_BRANCH * D_MODEL)
IN_COLS = sum(IN_SPLITS)

kernel_name = "hybrid_nsa_dsa_shortconv_decoder_step"


def rms_norm(x, g):
    x32 = x.astype(jnp.float32)
    y = x32 * lax.rsqrt(jnp.mean(x32 * x32, axis=-1, keepdims=True) + RMS_EPS)
    return (y * g.astype(jnp.float32)).astype(x.dtype)


def partial_rope(x, pos):
    d_rot = x.shape[-1] // ROPE_FRACTION
    half = d_rot // 2
    inv_freq = jnp.exp(jnp.arange(half, dtype=jnp.float32) * (-2.0 * math.log(ROPE_THETA) / d_rot))
    ang = pos.astype(jnp.float32)[:, None] * inv_freq[None, :]
    cos = jnp.cos(ang)[None, :, None, :]
    sin = jnp.sin(ang)[None, :, None, :]
    x32 = x.astype(jnp.float32)
    x1 = x32[..., :half]
    x2 = x32[..., half:d_rot]
    out = jnp.concatenate([x1 * cos - x2 * sin, x2 * cos + x1 * sin, x32[..., d_rot:]], axis=-1)
    return out.astype(x.dtype)


def rope_keys(kv, pos):
    k = partial_rope(kv[:, :, 0], pos)
    return jnp.concatenate([k[:, :, None], kv[:, :, 1:]], axis=2)


def masked_softmax(s, mask):
    s = jnp.where(mask, s, NEG_INF)
    m = jnp.max(s, axis=-1, keepdims=True)
    e = jnp.where(mask, jnp.exp(s - m), 0.0)
    return e / jnp.maximum(jnp.sum(e, axis=-1, keepdims=True), 1e-30)


def query_block(q_len):
    return Q_BLOCK if q_len % Q_BLOCK == 0 else q_len


def sweep_query_blocks(fn, p0, q_side):
    n, q_len = q_side[0].shape[:2]
    blk = query_block(q_len)
    nb = q_len // blk
    items = jnp.arange(n * nb, dtype=jnp.int32)
    b_idx = items // nb
    q_start = p0 + (items % nb) * blk
    xs = tuple(a.reshape((n * nb, blk) + a.shape[2:]) for a in q_side)
    out = lax.map(lambda it: fn(it[0], it[1], *it[2:]), (b_idx, q_start) + xs)
    return out.reshape((n, q_len) + out.shape[2:])


def nsa_compressed(q, kv_full, pos, w1, w2, pe):
    n, q_len = q.shape[:2]
    seq_len = kv_full.shape[1]
    n_chunk = seq_len // CMP_STRIDE
    n_cmp = n_chunk - 1
    half = CMP_STRIDE * HEAD_DIM
    ch = kv_full[:, :n_chunk * CMP_STRIDE].reshape(n, n_chunk, CMP_STRIDE, 2, NSA_KV, HEAD_DIM)
    ch = ch.transpose(0, 1, 3, 4, 2, 5).reshape(n, n_chunk, 2, NSA_KV, half)
    pe_bias = jnp.einsum('kf,kfh->kh', pe.reshape(2, CMP_LEN * HEAD_DIM), w1)
    pre = (jnp.einsum('bckgf,kfh->bckgh', ch[:, :-1], w1[:, :half])
           + jnp.einsum('bckgf,kfh->bckgh', ch[:, 1:], w1[:, half:])
           + pe_bias[None, None, :, None, :])
    cmp = jnp.einsum('bckgh,khd->bckgd', jax.nn.silu(pre), w2)
    k_c, v_c = cmp[:, :, 0], cmp[:, :, 1]
    qg = q.reshape(n, q_len, NSA_KV, NSA_REP, HEAD_DIM)
    s = jnp.einsum('bqgrd,bcgd->bgrqc', qg, k_c).astype(jnp.float32) * ATTN_SCALE
    blk_end = jnp.arange(n_cmp) * CMP_STRIDE + (CMP_LEN - 1)
    p = masked_softmax(s, blk_end[None, :] <= pos[:, None])
    o = jnp.einsum('bgrqc,bcgd->bqgrd', p.astype(v_c.dtype), v_c).reshape(n, q_len, NSA_HEADS, HEAD_DIM)
    n_slc = -(-seq_len // SEL_BLOCK)
    c_start = jnp.arange(n_cmp) * CMP_STRIDE
    s_start = jnp.arange(n_slc) * SEL_BLOCK
    overlap = ((c_start[:, None] < s_start[None, :] + SEL_BLOCK)
               & (c_start[:, None] + CMP_LEN > s_start[None, :])).astype(jnp.float32)
    p_slc = jnp.einsum('bgqc,cj->bqgj', p.sum(axis=2), overlap)
    return o, p_slc


def nsa_selected(q, kv_full, p_slc, p0):
    seq_len = kv_full.shape[1]
    n_slc = p_slc.shape[-1]
    kv_pad = jnp.pad(kv_full, ((0, 0), (0, n_slc * SEL_BLOCK - seq_len), (0, 0), (0, 0), (0, 0)))
    k_sel = min(N_SEL, n_slc)
    blocks = jnp.arange(n_slc)
    offs = jnp.arange(SEL_BLOCK)
    grp = jnp.arange(NSA_KV)[None, :, None]

    def attend(b, q_start, qb, pb):
        blk = qb.shape[0]
        pos = q_start + jnp.arange(blk)
        kv_b = kv_pad[b]
        cur = (pos // SEL_BLOCK)[:, None]
        forced = (blocks[None] == 0) | (blocks[None] == cur) | (blocks[None] == cur - 1)
        visible = blocks[None] * SEL_BLOCK <= pos[:, None]
        score = jnp.where(forced[:, None], FORCE_SCORE, jnp.where(visible[:, None], pb, NEG_INF))
        _, sel = lax.top_k(score, k_sel)
        tok = (sel[..., None] * SEL_BLOCK + offs).reshape(blk, NSA_KV, k_sel * SEL_BLOCK)
        k_g = kv_b[:, 0][tok, grp]
        v_g = kv_b[:, 1][tok, grp]
        qg = qb.reshape(blk, NSA_KV, NSA_REP, HEAD_DIM)
        s = jnp.einsum('qgrd,qgmd->qgrm', qg, k_g).astype(jnp.float32) * ATTN_SCALE
        p = masked_softmax(s, (tok <= pos[:, None, None])[:, :, None, :])
        o = jnp.einsum('qgrm,qgmd->qgrd', p.astype(v_g.dtype), v_g)
        return o.reshape(blk, NSA_HEADS, HEAD_DIM)

    return sweep_query_blocks(attend, p0, (q, p_slc))


def nsa_window(q, win_all, p0):
    n, q_len = q.shape[:2]
    pad = WINDOW - (win_all.shape[1] - q_len)
    wp = jnp.pad(win_all, ((0, 0), (pad, 0), (0, 0), (0, 0), (0, 0)))
    blk = query_block(q_len)
    nb = q_len // blk
    span = blk + WINDOW
    idx = jnp.arange(nb)[:, None] * blk + jnp.arange(span)[None, :]
    kw = wp[:, idx]
    qg = q.reshape(n, nb, blk, NSA_KV, NSA_REP, HEAD_DIM)
    s = jnp.einsum('bnqgrd,bnkgd->bngrqk', qg, kw[:, :, :, 0]).astype(jnp.float32) * ATTN_SCALE
    q_pos = p0 + jnp.arange(nb)[:, None] * blk + jnp.arange(blk)[None, :]
    k_pos = p0 - WINDOW + idx
    rel = q_pos[:, :, None] - k_pos[:, None, :]
    mask = (k_pos[:, None, :] >= 0) & (rel >= 0) & (rel <= WINDOW)
    p = masked_softmax(s, mask[None, :, None, None])
    o = jnp.einsum('bngrqk,bnkgd->bnqgrd', p.astype(kw.dtype), kw[:, :, :, 1])
    return o.reshape(n, q_len, NSA_HEADS, HEAD_DIM)


def dsa_attention(q, q_idx, w_idx, kv_full, k_idx_full, p0):
    seq_len = kv_full.shape[1]
    k_top = min(DSA_TOPK, seq_len // 4)
    key_pos = jnp.arange(seq_len)

    def attend(b, q_start, qb, qib, wib):
        blk = qb.shape[0]
        pos = q_start + jnp.arange(blk)
        kv_b = kv_full[b]
        logits = jnp.einsum('qhe,se->qsh', qib, k_idx_full[b]).astype(jnp.float32) * IDX_SCALE
        score = jnp.einsum('qsh,qh->qs', jax.nn.relu(logits), wib.astype(jnp.float32))
        causal = key_pos[None, :] <= pos[:, None]
        _, sel = lax.top_k(jnp.where(causal, score, NEG_INF), k_top)
        k_g = kv_b[:, 0][sel]
        v_g = kv_b[:, 1][sel]
        qg = qb.reshape(blk, DSA_KV, DSA_REP, HEAD_DIM)
        s = jnp.einsum('qgrd,qmgd->qgrm', qg, k_g).astype(jnp.float32) * ATTN_SCALE
        p = masked_softmax(s, (sel <= pos[:, None])[:, None, None, :])
        o = jnp.einsum('qgrm,qmgd->qgrd', p.astype(v_g.dtype), v_g)
        return o.reshape(blk, DSA_HEADS, HEAD_DIM)

    return sweep_query_blocks(attend, p0, (q, q_idx, w_idx))


def short_conv(u, gate_b, gate_c, buf, w):
    v = gate_c * u
    vp = jnp.concatenate([buf.astype(v.dtype), v], axis=1)
    q_len = v.shape[1]
    y = w[0] * vp[:, 0:q_len]
    for j in range(1, CONV_WIDTH):
        y = y + w[j] * vp[:, j:j + q_len]
    return gate_b * y, vp[:, q_len:]


def mixer(h, past, p0, w_in, cmp_w1, cmp_w2, cmp_pe, conv_w, w_branch, w_out):
    n, q_len, _ = h.shape
    pos = p0 + jnp.arange(q_len, dtype=jnp.int32)
    proj = jnp.einsum('bqd,de->bqe', h, w_in)
    (q_a, cmp_kv, slc_kv, win_kv, gate_a, q_b, dsa_kv, q_i, k_i, w_i,
     conv_u, conv_b, conv_c, gate_m) = jnp.split(proj, np.cumsum(IN_SPLITS)[:-1].tolist(), axis=-1)
    past_cmp, past_slc, past_win, past_dsa, past_idx, past_conv = past
    nsa_kv_shape = (n, q_len, 2, NSA_KV, HEAD_DIM)
    q_a = partial_rope(q_a.reshape(n, q_len, NSA_HEADS, HEAD_DIM), pos)
    cmp_kv = rope_keys(cmp_kv.reshape(nsa_kv_shape), pos)
    slc_kv = rope_keys(slc_kv.reshape(nsa_kv_shape), pos)
    win_kv = rope_keys(win_kv.reshape(nsa_kv_shape), pos)
    win_all = jnp.concatenate([past_win.astype(win_kv.dtype), win_kv], axis=1)
    o_cmp, p_slc = nsa_compressed(q_a, jnp.concatenate([past_cmp.astype(cmp_kv.dtype), cmp_kv], axis=1),
                                  pos, cmp_w1, cmp_w2, cmp_pe)
    o_slc = nsa_selected(q_a, jnp.concatenate([past_slc.astype(slc_kv.dtype), slc_kv], axis=1), p_slc, p0)
    o_win = nsa_window(q_a, win_all, p0)
    g_a = jax.nn.sigmoid(gate_a.astype(jnp.float32)).reshape(n, q_len, NSA_HEADS, 3)
    o_a = (g_a[..., 0:1] * o_cmp + g_a[..., 1:2] * o_slc + g_a[..., 2:3] * o_win).astype(h.dtype)
    q_b = partial_rope(q_b.reshape(n, q_len, DSA_HEADS, HEAD_DIM), pos)
    dsa_kv = rope_keys(dsa_kv.reshape(n, q_len, 2, DSA_KV, HEAD_DIM), pos)
    q_i = partial_rope(q_i.reshape(n, q_len, IDX_HEADS, IDX_DIM), pos)
    k_i = partial_rope(k_i.reshape(n, q_len, 1, IDX_DIM), pos)[:, :, 0]
    o_b = dsa_attention(q_b, q_i, w_i * (IDX_HEADS ** -0.5),
                        jnp.concatenate([past_dsa.astype(dsa_kv.dtype), dsa_kv], axis=1),
                        jnp.concatenate([past_idx.astype(k_i.dtype), k_i], axis=1), p0)
    o_c, conv_state = short_conv(conv_u, conv_b, conv_c, past_conv, conv_w)
    branches = (o_a.reshape(n, q_len, BRANCH_W), o_b.reshape(n, q_len, BRANCH_W), o_c)
    g_m = jax.nn.sigmoid(gate_m.astype(jnp.float32)).reshape(n, q_len, N_BRANCH, D_MODEL)
    merged = jnp.zeros((n, q_len, D_MODEL), jnp.float32)
    for i in range(N_BRANCH):
        merged = merged + g_m[:, :, i] * jnp.einsum('bqw,wd->bqd', branches[i], w_branch[i]).astype(jnp.float32)
    out = jnp.einsum('bqd,de->bqe', merged.astype(h.dtype), w_out)
    win_keep = min(WINDOW, win_all.shape[1])
    new_state = (cmp_kv, slc_kv, win_all[:, win_all.shape[1] - win_keep:], dsa_kv, k_i, conv_state)
    return out, new_state


def swiglu(h, w_gate_up, w_down):
    gu = jnp.einsum('bqd,df->bqf', h, w_gate_up)
    g, u = jnp.split(gu, 2, axis=-1)
    return jnp.einsum('bqf,fd->bqd', jax.nn.silu(g) * u, w_down)


def trunk_layer(x, past, p0, g_mix_pre, g_mix_post, g_ffn_pre, g_ffn_post, w_in, cmp_w1, cmp_w2, cmp_pe,
                conv_w, w_branch, w_out, w_gate_up, w_down):
    mix, new_state = mixer(rms_norm(x, g_mix_pre), past, p0, w_in, cmp_w1, cmp_w2, cmp_pe, conv_w, w_branch, w_out)
    x = x + rms_norm(mix, g_mix_post)
    x = x + rms_norm(swiglu(rms_norm(x, g_ffn_pre), w_gate_up, w_down), g_ffn_post)
    return x, new_state


def gather_pages(pool, page_table):
    g = pool[page_table]
    return g.reshape((page_table.shape[0], page_table.shape[1] * pool.shape[1]) + pool.shape[2:])


def setup_inputs(seed: int = 0) -> dict:
    key = jax.random.key(seed)
    ks = jax.random.split(key, 24)
    n_pages = PAST_LEN // PAGE_SIZE
    n_used = DEC_BATCH * n_pages
    n_pool = n_used + (n_used + 3) // 4

    def nrm(k, shape, scale):
        return jax.random.normal(k, shape, jnp.float32) * scale

    page_table = jax.random.permutation(ks[8], n_pool)[:n_used].reshape(DEC_BATCH, n_pages).astype(jnp.int32)
    return {
        "x_prompt": nrm(ks[0], (BATCH, SEQ, D_MODEL), 1.0),
        "x_sample": nrm(ks[1], (DEC_BATCH, DEC_SEQ, D_MODEL), 1.0),
        "cache_nsa_cmp_kv": nrm(ks[2], (DEPTH, n_pool, PAGE_SIZE, 2, NSA_KV, HEAD_DIM), 1.0),
        "cache_nsa_slc_kv": nrm(ks[3], (DEPTH, n_pool, PAGE_SIZE, 2, NSA_KV, HEAD_DIM), 1.0),
        "state_nsa_win_kv": nrm(ks[4], (DEPTH, DEC_BATCH, min(WINDOW, PAST_LEN), 2, NSA_KV, HEAD_DIM), 1.0),
        "cache_dsa_kv": nrm(ks[5], (DEPTH, n_pool, PAGE_SIZE, 2, DSA_KV, HEAD_DIM), 1.0),
        "cache_dsa_idx_k": nrm(ks[6], (DEPTH, n_pool, PAGE_SIZE, IDX_DIM), 1.0),
        "state_conv": nrm(ks[7], (DEPTH, DEC_BATCH, CONV_WIDTH - 1, CONV_DIM), 1.0),
        "page_table": page_table,
        "norm_mix_pre": 1.0 + nrm(ks[9], (DEPTH, D_MODEL), 0.05),
        "norm_mix_post": 1.0 + nrm(ks[10], (DEPTH, D_MODEL), 0.05),
        "norm_ffn_pre": 1.0 + nrm(ks[11], (DEPTH, D_MODEL), 0.05),
        "norm_ffn_post": 1.0 + nrm(ks[12], (DEPTH, D_MODEL), 0.05),
        "w_in": nrm(ks[13], (DEPTH, D_MODEL, IN_COLS), D_MODEL ** -0.5),
        "cmp_w1": nrm(ks[14], (DEPTH, 2, CMP_LEN * HEAD_DIM, CMP_HIDDEN), (CMP_LEN * HEAD_DIM) ** -0.5),
        "cmp_w2": nrm(ks[15], (DEPTH, 2, CMP_HIDDEN, HEAD_DIM), CMP_HIDDEN ** -0.5),
        "cmp_pe": nrm(ks[16], (DEPTH, 2, CMP_LEN, HEAD_DIM), 0.1),
        "conv_w": nrm(ks[17], (DEPTH, CONV_WIDTH, CONV_DIM), CONV_WIDTH ** -0.5),
        "w_branch": nrm(ks[18], (DEPTH, N_BRANCH, BRANCH_W, D_MODEL), BRANCH_W ** -0.5),
        "w_out": nrm(ks[19], (DEPTH, D_MODEL, D_MODEL), D_MODEL ** -0.5),
        "ffn_w_gate_up": nrm(ks[20], (DEPTH, D_MODEL, 2 * FFN_HIDDEN), D_MODEL ** -0.5),
        "ffn_w_down": nrm(ks[21], (DEPTH, FFN_HIDDEN, D_MODEL), FFN_HIDDEN ** -0.5),
    }


def reference(x_prompt, x_sample, cache_nsa_cmp_kv, cache_nsa_slc_kv, state_nsa_win_kv, cache_dsa_kv,
              cache_dsa_idx_k, state_conv, page_table, norm_mix_pre, norm_mix_post, norm_ffn_pre, norm_ffn_post,
              w_in, cmp_w1, cmp_w2, cmp_pe, conv_w, w_branch, w_out, ffn_w_gate_up, ffn_w_down):
    y_prompt, y_sample = x_prompt, x_sample
    nb, dt = x_prompt.shape[0], x_prompt.dtype
    past_len = page_table.shape[1] * cache_nsa_cmp_kv.shape[2]
    new_p, new_s = [], []
    for l in range(DEPTH):
        lw = (norm_mix_pre[l], norm_mix_post[l], norm_ffn_pre[l], norm_ffn_post[l], w_in[l], cmp_w1[l], cmp_w2[l],
              cmp_pe[l], conv_w[l], w_branch[l], w_out[l], ffn_w_gate_up[l], ffn_w_down[l])
        past_p = (jnp.zeros((nb, 0, 2, NSA_KV, HEAD_DIM), dt), jnp.zeros((nb, 0, 2, NSA_KV, HEAD_DIM), dt),
                  jnp.zeros((nb, 0, 2, NSA_KV, HEAD_DIM), dt), jnp.zeros((nb, 0, 2, DSA_KV, HEAD_DIM), dt),
                  jnp.zeros((nb, 0, IDX_DIM), dt), jnp.zeros((nb, CONV_WIDTH - 1, CONV_DIM), dt))
        past_s = (gather_pages(cache_nsa_cmp_kv[l], page_table), gather_pages(cache_nsa_slc_kv[l], page_table),
                  state_nsa_win_kv[l], gather_pages(cache_dsa_kv[l], page_table),
                  gather_pages(cache_dsa_idx_k[l], page_table), state_conv[l])
        y_prompt, st_p = trunk_layer(y_prompt, past_p, 0, *lw)
        y_sample, st_s = trunk_layer(y_sample, past_s, past_len, *lw)
        new_p.append(st_p)
        new_s.append(st_s)
    p_cmp, p_slc, p_win, p_dsa, p_idx, p_conv = [jnp.stack([st[i] for st in new_p]) for i in range(6)]
    s_cmp, s_slc, s_win, s_dsa, s_idx, s_conv = [jnp.stack([st[i] for st in new_s]) for i in range(6)]
    return (y_prompt, y_sample, p_cmp, p_slc, p_win, p_dsa, p_idx, p_conv, s_cmp, s_slc, s_win, s_dsa, s_idx, s_conv)
```

```python
import functools
import math

import numpy as np
import jax
import jax.numpy as jnp
from jax import lax
from jax.experimental import pallas as pl
from jax.experimental.pallas import tpu as pltpu

D_MODEL = 2048
HEAD_DIM = 128
BRANCH_W = D_MODEL // 2
N_BRANCH = 3
NSA_HEADS = BRANCH_W // HEAD_DIM
NSA_KV = 2
NSA_REP = NSA_HEADS // NSA_KV
CMP_STRIDE = 16
CMP_LEN = 2 * CMP_STRIDE
SEL_BLOCK = 64
N_SEL = 16
WINDOW = 512
DSA_HEADS = BRANCH_W // HEAD_DIM
DSA_KV = 2
IDX_HEADS = 8
IDX_DIM = 64
DSA_TOPK = 256
CONV_DIM = BRANCH_W
CONV_WIDTH = 3
FFN_HIDDEN = ((8 * D_MODEL + 3 * 256 - 1) // (3 * 256)) * 256
ROPE_THETA = 500000.0
ROPE_FRACTION = 4
RMS_EPS = 1e-6
ATTN_SCALE = HEAD_DIM ** -0.5
IDX_SCALE = IDX_DIM ** -0.5
NEG_INF = -1e30
FORCE_SCORE = 1e30
BELOW_ALL = -3.0e38
PAGE_SIZE = 128

LANE = 128
KV_W = 2 * NSA_KV * HEAD_DIM
CMP_FEAT = CMP_STRIDE * KV_W
VMEM_LIMIT = 60 * 1024 * 1024
INT_MIN = -(2 ** 31)

F32 = jnp.float32
BF16 = jnp.bfloat16


def _cp(sem, vmem=VMEM_LIMIT):
    return pltpu.CompilerParams(dimension_semantics=sem, vmem_limit_bytes=vmem)


def _pick(n, pref, mult=8):
    if n <= pref:
        return n
    for b in range(pref, 0, -1):
        if n % b == 0 and b % mult == 0:
            return b
    return n


def _sigmoid(x):
    return 1.0 / (1.0 + jnp.exp(-x))


def _rms(x, g):
    return x * lax.rsqrt(jnp.mean(x * x, axis=-1, keepdims=True) + RMS_EPS) * g


def _rmsnorm_kernel(x_ref, g_ref, o_ref):
    o_ref[...] = _rms(x_ref[...], g_ref[...]).astype(o_ref.dtype)


def rmsnorm(x, g, out_dtype=BF16):
    t, d = x.shape
    bm = _pick(t, 512)
    return pl.pallas_call(
        _rmsnorm_kernel,
        grid=(t // bm,),
        in_specs=[pl.BlockSpec((bm, d), lambda i: (i, 0)), pl.BlockSpec((1, d), lambda i: (0, 0))],
        out_specs=pl.BlockSpec((bm, d), lambda i: (i, 0)),
        out_shape=jax.ShapeDtypeStruct((t, d), out_dtype),
        compiler_params=_cp(("parallel",)),
        name="rmsnorm",
    )(x, g.reshape(1, d))


def _mm_rope_kernel(x_ref, w_ref, c_ref, s1_ref, s2_ref, o_ref, *, half, rope_blocks):
    y = jnp.dot(x_ref[...], w_ref[...], preferred_element_type=F32)
    c, s1, s2 = c_ref[...], s1_ref[...], s2_ref[...]
    for h, roped in enumerate(rope_blocks):
        yh = y[:, h * LANE:(h + 1) * LANE]
        if roped:
            yh = yh * c + pltpu.roll(yh, LANE - half, 1) * s1 + pltpu.roll(yh, half, 1) * s2
        o_ref[:, h * LANE:(h + 1) * LANE] = yh


def proj_rope(xn, w, tabs, *, bn, half, rope_blocks, stacked):
    t, k = xn.shape
    n = w.shape[1]
    tab_rows = tabs[0].shape[0]
    bm = _pick(math.gcd(t, tab_rows), 1024)
    tab_blocks = tab_rows // bm
    kern = functools.partial(_mm_rope_kernel, half=half, rope_blocks=rope_blocks)
    tab_spec = pl.BlockSpec((bm, LANE), lambda i, j: (i % tab_blocks, 0))
    if stacked:
        out_shape = jax.ShapeDtypeStruct((n // bn, t, bn), F32)
        out_spec = pl.BlockSpec((None, bm, bn), lambda i, j: (j, i, 0))
    else:
        out_shape = jax.ShapeDtypeStruct((t, n), F32)
        out_spec = pl.BlockSpec((bm, bn), lambda i, j: (i, j))
    return pl.pallas_call(
        kern,
        grid=(t // bm, n // bn),
        in_specs=[pl.BlockSpec((bm, k), lambda i, j: (i, 0)), pl.BlockSpec((k, bn), lambda i, j: (0, j)),
                  tab_spec, tab_spec, tab_spec],
        out_specs=out_spec,
        out_shape=out_shape,
        compiler_params=_cp(("parallel", "arbitrary")),
        name="proj_rope",
    )(xn, w, *tabs)


def _mm_kernel(x_ref, w_ref, o_ref):
    o_ref[...] = jnp.dot(x_ref[...], w_ref[...], preferred_element_type=F32).astype(o_ref.dtype)


def matmul(x, w, *, bn, out_dtype=F32):
    t, k = x.shape
    n = w.shape[1]
    bm = _pick(t, 1024)
    return pl.pallas_call(
        _mm_kernel,
        grid=(t // bm, n // bn),
        in_specs=[pl.BlockSpec((bm, k), lambda i, j: (i, 0)), pl.BlockSpec((k, bn), lambda i, j: (0, j))],
        out_specs=pl.BlockSpec((bm, bn), lambda i, j: (i, j)),
        out_shape=jax.ShapeDtypeStruct((t, n), out_dtype),
        compiler_params=_cp(("parallel", "arbitrary")),
        name="proj_plain",
    )(x, w)


def _merge_kernel(a1_ref, a2_ref, a3_ref, ob_ref, oc_ref, w_ref, g0_ref, g1_ref, g2_ref, o_ref):
    xa = (a1_ref[...] + a2_ref[...] + a3_ref[...]).astype(BF16)
    acc = _sigmoid(g0_ref[...]) * jnp.dot(xa, w_ref[0], preferred_element_type=F32)
    acc += _sigmoid(g1_ref[...]) * jnp.dot(ob_ref[...].astype(BF16), w_ref[1], preferred_element_type=F32)
    acc += _sigmoid(g2_ref[...]) * jnp.dot(oc_ref[...], w_ref[2], preferred_element_type=F32)
    o_ref[...] = acc.astype(o_ref.dtype)


def merge_branches(a1, a2, a3, ob, oc, wb, cg):
    t = a1.shape[0]
    bm = _pick(t, 512)
    bn = 1024
    nj = D_MODEL // bn
    g_base = 3 * CONV_DIM // bn
    xs = pl.BlockSpec((bm, BRANCH_W), lambda i, j: (i, 0))

    def gspec(br):
        return pl.BlockSpec((bm, bn), lambda i, j: (i, g_base + br * nj + j))

    return pl.pallas_call(
        _merge_kernel,
        grid=(t // bm, nj),
        in_specs=[xs, xs, xs, xs, xs, pl.BlockSpec((N_BRANCH, BRANCH_W, bn), lambda i, j: (0, 0, j)),
                  gspec(0), gspec(1), gspec(2)],
        out_specs=pl.BlockSpec((bm, bn), lambda i, j: (i, j)),
        out_shape=jax.ShapeDtypeStruct((t, D_MODEL), BF16),
        compiler_params=_cp(("parallel", "arbitrary")),
        name="merge",
    )(a1, a2, a3, ob, oc, wb, cg, cg, cg)


def _outproj_kernel(m_ref, w_ref, x_ref, gpost_ref, gpre_ref, xo_ref, hn_ref):
    y = jnp.dot(m_ref[...], w_ref[...], preferred_element_type=F32)
    xn = x_ref[...] + _rms(y, gpost_ref[...])
    xo_ref[...] = xn
    hn_ref[...] = _rms(xn, gpre_ref[...]).astype(hn_ref.dtype)


def outproj_residual(merged, w_out, x, g_post, g_ffn_pre):
    t, d = x.shape
    bm = _pick(t, 512)
    row = pl.BlockSpec((bm, d), lambda i: (i, 0))
    vec = pl.BlockSpec((1, d), lambda i: (0, 0))
    return pl.pallas_call(
        _outproj_kernel,
        grid=(t // bm,),
        in_specs=[row, pl.BlockSpec((d, d), lambda i: (0, 0)), row, vec, vec],
        out_specs=[row, row],
        out_shape=[jax.ShapeDtypeStruct((t, d), F32), jax.ShapeDtypeStruct((t, d), BF16)],
        compiler_params=_cp(("parallel",)),
        name="outproj",
    )(merged, w_out, x, g_post.reshape(1, d), g_ffn_pre.reshape(1, d))


def _ffn_gu_kernel(h_ref, wg_ref, wu_ref, o_ref):
    h = h_ref[...]
    g = jnp.dot(h, wg_ref[...], preferred_element_type=F32)
    u = jnp.dot(h, wu_ref[...], preferred_element_type=F32)
    o_ref[...] = (g * _sigmoid(g) * u).astype(o_ref.dtype)


def ffn_gate_up(hn, w_gu):
    t, d = hn.shape
    bm = _pick(t, 1024)
    bn = 512
    nj = FFN_HIDDEN // bn
    return pl.pallas_call(
        _ffn_gu_kernel,
        grid=(t // bm, nj),
        in_specs=[pl.BlockSpec((bm, d), lambda i, j: (i, 0)), pl.BlockSpec((d, bn), lambda i, j: (0, j)),
                  pl.BlockSpec((d, bn), lambda i, j: (0, nj + j))],
        out_specs=pl.BlockSpec((bm, bn), lambda i, j: (i, j)),
        out_shape=jax.ShapeDtypeStruct((t, FFN_HIDDEN), BF16),
        compiler_params=_cp(("parallel", "arbitrary")),
        name="ffn_gate_up",
    )(hn, w_gu, w_gu)


def _ffn_down_kernel(a_ref, w_ref, x_ref, gpost_ref, gnext_ref, y_ref, xn_ref, acc_ref, *, nk):
    k = pl.program_id(1)

    @pl.when(k == 0)
    def _():
        acc_ref[...] = jnp.zeros_like(acc_ref)

    acc_ref[...] += jnp.dot(a_ref[...], w_ref[...], preferred_element_type=F32)

    @pl.when(k == nk - 1)
    def _():
        y = x_ref[...] + _rms(acc_ref[...], gpost_ref[...])
        y_ref[...] = y
        xn_ref[...] = _rms(y, gnext_ref[...]).astype(xn_ref.dtype)


def ffn_down_residual(act, w_down, x, g_post, g_next):
    t, d = x.shape
    bm = _pick(t, 512)
    bk = 1408
    nk = FFN_HIDDEN // bk
    row = pl.BlockSpec((bm, d), lambda i, k: (i, 0))
    vec = pl.BlockSpec((1, d), lambda i, k: (0, 0))
    return pl.pallas_call(
        functools.partial(_ffn_down_kernel, nk=nk),
        grid=(t // bm, nk),
        in_specs=[pl.BlockSpec((bm, bk), lambda i, k: (i, k)), pl.BlockSpec((bk, d), lambda i, k: (k, 0)),
                  row, vec, vec],
        out_specs=[row, row],
        out_shape=[jax.ShapeDtypeStruct((t, d), F32), jax.ShapeDtypeStruct((t, d), BF16)],
        scratch_shapes=[pltpu.VMEM((bm, d), F32)],
        compiler_params=_cp(("parallel", "arbitrary")),
        name="ffn_down",
    )(act, w_down, x, g_post.reshape(1, d), g_next.reshape(1, d))


def _conv_kernel(u_ref, b_ref, c_ref, w_ref, o_ref, st_ref):
    v = c_ref[...] * u_ref[...]
    s = v.shape[0]
    row = lax.broadcasted_iota(jnp.int32, v.shape, 0)
    v1 = jnp.where(row >= 1, pltpu.roll(v, 1, 0), 0.0)
    v2 = jnp.where(row >= 2, pltpu.roll(v, 2, 0), 0.0)
    w = w_ref[...]
    y = w[0:1] * v2 + w[1:2] * v1 + w[2:3] * v
    o_ref[...] = (b_ref[...] * y).astype(o_ref.dtype)
    st_ref[...] = v[s - (CONV_WIDTH - 1):, :]


def conv_prompt(cg, conv_w, n, s):
    bc = 256
    nj = CONV_DIM // bc
    return pl.pallas_call(
        _conv_kernel,
        grid=(n, nj),
        in_specs=[pl.BlockSpec((s, bc), lambda b, j: (b, j)), pl.BlockSpec((s, bc), lambda b, j: (b, nj + j)),
                  pl.BlockSpec((s, bc), lambda b, j: (b, 2 * nj + j)),
                  pl.BlockSpec((CONV_WIDTH, bc), lambda b, j: (0, j))],
        out_specs=[pl.BlockSpec((s, bc), lambda b, j: (b, j)),
                   pl.BlockSpec((None, CONV_WIDTH - 1, bc), lambda b, j: (b, 0, j))],
        out_shape=[jax.ShapeDtypeStruct((n * s, CONV_DIM), BF16),
                   jax.ShapeDtypeStruct((n, CONV_WIDTH - 1, CONV_DIM), F32)],
        compiler_params=_cp(("parallel", "arbitrary")),
        name="conv_prompt",
    )(cg, cg, cg, conv_w)


def _conv_dec_kernel(u_ref, b_ref, c_ref, buf_ref, w_ref, o_ref, st_ref):
    v = c_ref[...] * u_ref[...]
    b0 = buf_ref[:, 0, :]
    b1 = buf_ref[:, 1, :]
    w = w_ref[...]
    y = w[0:1] * b0 + w[1:2] * b1 + w[2:3] * v
    o_ref[...] = (b_ref[...] * y).astype(o_ref.dtype)
    st_ref[:, 0, :] = b1
    st_ref[:, 1, :] = v


def conv_decode(cg, buf, conv_w):
    n = cg.shape[0]
    blk = lambda j: pl.BlockSpec((n, CONV_DIM), lambda i: (0, j))
    full3 = pl.BlockSpec((n, CONV_WIDTH - 1, CONV_DIM), lambda i: (0, 0, 0))
    return pl.pallas_call(
        _conv_dec_kernel,
        grid=(1,),
        in_specs=[blk(0), blk(1), blk(2), full3, pl.BlockSpec((CONV_WIDTH, CONV_DIM), lambda i: (0, 0))],
        out_specs=[pl.BlockSpec((n, CONV_DIM), lambda i: (0, 0)), full3],
        out_shape=[jax.ShapeDtypeStruct((n, CONV_DIM), BF16),
                   jax.ShapeDtypeStruct((n, CONV_WIDTH - 1, CONV_DIM), F32)],
        compiler_params=_cp(("arbitrary",)),
        name="conv_decode",
    )(cg, cg, cg, buf, conv_w)


def _compress(read_x, w1_ref, w2_ref, pe_ref, nc):
    out = []
    for kv in range(2):
        acc = jnp.zeros((NSA_KV * nc, 2 * HEAD_DIM), F32)
        bias = jnp.zeros((8, 2 * HEAD_DIM), F32)
        for t in range(CMP_STRIDE):
            xt = jnp.concatenate([read_x(t, kv * NSA_KV + g) for g in range(NSA_KV)], axis=0).astype(BF16)
            wt = w1_ref[kv, t]
            acc += jnp.dot(xt, wt, preferred_element_type=F32)
            bias += jnp.dot(pe_ref[kv, t], wt, preferred_element_type=F32)
        pe_bias = bias[0:1, :HEAD_DIM] + bias[1:2, HEAD_DIM:]
        per_group = []
        for g in range(NSA_KV):
            a = acc[g * nc:(g + 1) * nc, :HEAD_DIM]
            b = acc[g * nc:(g + 1) * nc, HEAD_DIM:]
            pre = a + pltpu.roll(b, nc - 1, 0) + pe_bias
            hid = pre * _sigmoid(pre)
            per_group.append(jnp.dot(hid.astype(BF16), w2_ref[kv], preferred_element_type=F32))
        out.append(per_group)
    return out[0], out[1]


def _gate_rows(gates_blk):
    return gates_blk.T


def _stack_heads(q_ref, g, bq):
    return jnp.concatenate(
        [q_ref[:, (g * NSA_REP + r) * HEAD_DIM:(g * NSA_REP + r + 1) * HEAD_DIM] for r in range(NSA_REP)],
        axis=0).astype(BF16)


def _flash_t(qs, kv_ref, g, lo, hi, kc, mask_fn, bq):
    nq = NSA_REP * bq

    def body(c, carry):
        m, l, acc = carry
        k0 = pl.multiple_of(c * kc, kc)
        kb = kv_ref[pl.ds(k0, kc), g * HEAD_DIM:(g + 1) * HEAD_DIM].astype(BF16)
        vb = kv_ref[pl.ds(k0, kc), (NSA_KV + g) * HEAD_DIM:(NSA_KV + g + 1) * HEAD_DIM].astype(BF16)
        s = lax.dot_general(kb, qs, (((1,), (1,)), ((), ())), preferred_element_type=F32) * ATTN_SCALE
        mk = mask_fn(k0)
        mk = jnp.concatenate([mk] * NSA_REP, axis=1)
        s = jnp.where(mk, s, NEG_INF)
        m_new = jnp.maximum(m, jnp.max(s, axis=0, keepdims=True))
        alpha = jnp.exp(m - m_new)
        p = jnp.where(mk, jnp.exp(s - m_new), 0.0)
        l = alpha * l + jnp.sum(p, axis=0, keepdims=True)
        pv = lax.dot_general(vb, p.astype(BF16), (((0,), (0,)), ((), ())), preferred_element_type=F32)
        return m_new, l, alpha * acc + pv

    init = (jnp.full((1, nq), NEG_INF, F32), jnp.zeros((1, nq), F32), jnp.zeros((HEAD_DIM, nq), F32))
    _, l, acc = lax.fori_loop(lo, hi, body, init)
    return acc / jnp.maximum(l, 1e-30)


def _store_heads(o_ref, ot, g, bq, gate_t, branch):
    for r in range(NSA_REP):
        h = g * NSA_REP + r
        oh = ot[:, r * bq:(r + 1) * bq]
        if gate_t is not None:
            oh = oh * _sigmoid(gate_t[h * 3 + branch:h * 3 + branch + 1, :])
        o_ref[:, h * HEAD_DIM:(h + 1) * HEAD_DIM] = oh.T


def _cmp_prompt_kernel(x_ref, q_ref, gates_ref, w1_ref, w2_ref, pe_ref, ov_ref, o_ref, ps_ref, kc_ref, vc_ref,
                       *, nc, bq):
    qi = pl.program_id(1)

    @pl.when(qi == 0)
    def _():
        def read_x(t, kg):
            return x_ref[:, t * KV_W + kg * HEAD_DIM:t * KV_W + (kg + 1) * HEAD_DIM]
        k_c, v_c = _compress(read_x, w1_ref, w2_ref, pe_ref, nc)
        for g in range(NSA_KV):
            kc_ref[g] = k_c[g].astype(BF16)
            vc_ref[g] = v_c[g].astype(BF16)

    q0 = qi * bq
    nq = NSA_REP * bq
    gate_t = _gate_rows(gates_ref[...])
    pos = q0 + lax.broadcasted_iota(jnp.int32, (nc, bq), 1)
    blk_end = lax.broadcasted_iota(jnp.int32, (nc, bq), 0) * CMP_STRIDE + (CMP_LEN - 1)
    mk1 = blk_end <= pos
    mk = jnp.concatenate([mk1] * NSA_REP, axis=1)
    for g in range(NSA_KV):
        qs = _stack_heads(q_ref, g, bq)
        s = lax.dot_general(kc_ref[g], qs, (((1,), (1,)), ((), ())), preferred_element_type=F32) * ATTN_SCALE
        s = jnp.where(mk, s, NEG_INF)
        m = jnp.max(s, axis=0, keepdims=True)
        e = jnp.where(mk, jnp.exp(s - m), 0.0)
        p = e / jnp.maximum(jnp.sum(e, axis=0, keepdims=True), 1e-30)
        ot = lax.dot_general(vc_ref[g], p.astype(BF16), (((0,), (0,)), ((), ())), preferred_element_type=F32)
        _store_heads(o_ref, ot, g, bq, gate_t, 0)
        psum = p[:, 0:bq]
        for r in range(1, NSA_REP):
            psum = psum + p[:, r * bq:(r + 1) * bq]
        ps_ref[g] = jnp.dot(ov_ref[...], psum, precision=lax.Precision.HIGHEST, preferred_element_type=F32)


def cmp_prompt(cmp_x, q, idxm, w1cat, w2, pe8, ov_t, n, s):
    nc = s // CMP_STRIDE
    bq = 256
    nq = s // bq
    n_slc = ov_t.shape[0]
    t = n * s
    full = lambda a: pl.BlockSpec(a.shape, lambda b, i: (0,) * a.ndim)
    return pl.pallas_call(
        functools.partial(_cmp_prompt_kernel, nc=nc, bq=bq),
        grid=(n, nq),
        in_specs=[pl.BlockSpec((nc, CMP_FEAT), lambda b, i: (b, 0)),
                  pl.BlockSpec((bq, BRANCH_W), lambda b, i: (b * nq + i, 0)),
                  pl.BlockSpec((bq, LANE), lambda b, i: (b * nq + i, 5)),
                  full(w1cat), full(w2), full(pe8), full(ov_t)],
        out_specs=[pl.BlockSpec((bq, BRANCH_W), lambda b, i: (b * nq + i, 0)),
                   pl.BlockSpec((NSA_KV, n_slc, bq), lambda b, i: (0, 0, b * nq + i))],
        out_shape=[jax.ShapeDtypeStruct((t, BRANCH_W), F32), jax.ShapeDtypeStruct((NSA_KV, n_slc, t), F32)],
        scratch_shapes=[pltpu.VMEM((NSA_KV, nc, HEAD_DIM), BF16), pltpu.VMEM((NSA_KV, nc, HEAD_DIM), BF16)],
        compiler_params=_cp(("arbitrary", "arbitrary")),
        name="nsa_cmp_prompt",
    )(cmp_x, q, idxm, w1cat, w2, pe8, ov_t)


def _topk_rank_rows(sc, n_rows, k):
    j = lax.broadcasted_iota(jnp.int32, sc.shape, 0)
    rank = jnp.zeros(sc.shape, jnp.int32)
    for i in range(n_rows):
        si = sc[i:i + 1, :]
        beats = jnp.where(si > sc, 1, jnp.where((si == sc) & (j > i), 1, 0))
        rank = rank + beats
    return jnp.where(rank < k, 1.0, 0.0)


def _slc_prompt_kernel(q_ref, kv_ref, ps_ref, gates_ref, o_ref, mask_ref, *, n_slc, bq, kc):
    qi = pl.program_id(1)
    q0 = qi * bq
    gate_t = _gate_rows(gates_ref[...])
    blk = lax.broadcasted_iota(jnp.int32, (n_slc, bq), 0)
    pos = q0 + lax.broadcasted_iota(jnp.int32, (n_slc, bq), 1)
    cur = pos // SEL_BLOCK
    forced = (blk == 0) | (blk == cur) | (blk == cur - 1)
    visible = blk * SEL_BLOCK <= pos
    for g in range(NSA_KV):
        sc = jnp.where(forced, FORCE_SCORE, jnp.where(visible, ps_ref[g], NEG_INF))
        sel = _topk_rank_rows(sc, n_slc, min(N_SEL, n_slc))
        for j in range(n_slc):
            mask_ref[g, j * SEL_BLOCK:(j + 1) * SEL_BLOCK, :] = jnp.broadcast_to(sel[j:j + 1, :], (SEL_BLOCK, bq))

    qpos = q0 + lax.broadcasted_iota(jnp.int32, (kc, bq), 1)
    krow = lax.broadcasted_iota(jnp.int32, (kc, bq), 0)
    for g in range(NSA_KV):
        def mask_fn(k0, g=g):
            return (mask_ref[g, pl.ds(k0, kc), :] > 0.5) & (k0 + krow <= qpos)
        qs = _stack_heads(q_ref, g, bq)
        ot = _flash_t(qs, kv_ref, g, 0, (q0 + bq + kc - 1) // kc, kc, mask_fn, bq)
        _store_heads(o_ref, ot, g, bq, gate_t, 1)


def slc_prompt(q, kv4, p_slc, idxm, n, s):
    bq = kc = 256
    nq = s // bq
    n_slc = p_slc.shape[1]
    t = n * s
    return pl.pallas_call(
        functools.partial(_slc_prompt_kernel, n_slc=n_slc, bq=bq, kc=kc),
        grid=(n, nq),
        in_specs=[pl.BlockSpec((bq, BRANCH_W), lambda b, i: (b * nq + i, 0)),
                  pl.BlockSpec((None, s, KV_W), lambda b, i: (1, b, 0)),
                  pl.BlockSpec((NSA_KV, n_slc, bq), lambda b, i: (0, 0, b * nq + i)),
                  pl.BlockSpec((bq, LANE), lambda b, i: (b * nq + i, 5))],
        out_specs=pl.BlockSpec((bq, BRANCH_W), lambda b, i: (b * nq + i, 0)),
        out_shape=jax.ShapeDtypeStruct((t, BRANCH_W), F32),
        scratch_shapes=[pltpu.VMEM((NSA_KV, s, bq), F32)],
        compiler_params=_cp(("parallel", "arbitrary")),
        name="nsa_slc_prompt",
    )(q, kv4, p_slc, idxm)


def _win_prompt_kernel(q_ref, kv_ref, gates_ref, o_ref, *, bq, kc):
    qi = pl.program_id(1)
    q0 = qi * bq
    gate_t = _gate_rows(gates_ref[...])
    qpos = q0 + lax.broadcasted_iota(jnp.int32, (kc, bq), 1)
    krow = lax.broadcasted_iota(jnp.int32, (kc, bq), 0)

    def mask_fn(k0):
        rel = qpos - (k0 + krow)
        return (rel >= 0) & (rel <= WINDOW)

    lo = jnp.maximum(q0 - WINDOW, 0) // kc
    hi = (q0 + bq + kc - 1) // kc
    for g in range(NSA_KV):
        qs = _stack_heads(q_ref, g, bq)
        ot = _flash_t(qs, kv_ref, g, lo, hi, kc, mask_fn, bq)
        _store_heads(o_ref, ot, g, bq, gate_t, 2)


def win_prompt(q, kv4, idxm, n, s):
    bq = kc = 256
    nq = s // bq
    t = n * s
    return pl.pallas_call(
        functools.partial(_win_prompt_kernel, bq=bq, kc=kc),
        grid=(n, nq),
        in_specs=[pl.BlockSpec((bq, BRANCH_W), lambda b, i: (b * nq + i, 0)),
                  pl.BlockSpec((None, s, KV_W), lambda b, i: (2, b, 0)),
                  pl.BlockSpec((bq, LANE), lambda b, i: (b * nq + i, 5))],
        out_specs=pl.BlockSpec((bq, BRANCH_W), lambda b, i: (b * nq + i, 0)),
        out_shape=jax.ShapeDtypeStruct((t, BRANCH_W), F32),
        compiler_params=_cp(("parallel", "arbitrary")),
        name="nsa_win_prompt",
    )(q, kv4, idxm)


def _order_key(x):
    b = pltpu.bitcast(x + 0.0, jnp.int32)
    return b ^ ((b >> 31) & jnp.int32(0x7FFFFFFF))


def _radix_kth(count_ge, k, shape):
    zero = jnp.zeros(shape, jnp.int32)
    base = jnp.where(count_ge(zero) >= k, zero, jnp.full(shape, INT_MIN, jnp.int32))

    def body(i, base):
        cand = base | jnp.left_shift(jnp.int32(1), 30 - i)
        return jnp.where(count_ge(cand) >= k, cand, base)

    return lax.fori_loop(0, 31, body, base)


def _tie_cut(count_eq_below, need, shape, n_bits):
    def body(i, m):
        cand = m | jnp.left_shift(jnp.int32(1), n_bits - 1 - i)
        return jnp.where(count_eq_below(cand) <= need, cand, m)

    return lax.fori_loop(0, n_bits, body, jnp.zeros(shape, jnp.int32))


def _dsa_prompt_kernel(q_ref, qi_ref, kidx_ref, kv_ref, gates_ref, o_ref, key_ref, mask_ref, *, bq, kc, s_len):
    qi = pl.program_id(1)
    q0 = qi * bq
    n_chunks = (q0 + bq + kc - 1) // kc
    gate_t = _gate_rows(gates_ref[...])
    w_row = jnp.concatenate([gate_t[24 + h:25 + h, :] for h in range(IDX_HEADS)], axis=1) * (IDX_HEADS ** -0.5)
    qis = jnp.concatenate([qi_ref[:, h * IDX_DIM:(h + 1) * IDX_DIM] for h in range(IDX_HEADS)], axis=0).astype(BF16)
    qpos = q0 + lax.broadcasted_iota(jnp.int32, (kc, bq), 1)
    krow = lax.broadcasted_iota(jnp.int32, (kc, bq), 0)

    def score_body(c, _):
        k0 = pl.multiple_of(c * kc, kc)
        kb = kidx_ref[pl.ds(k0, kc), 0:IDX_DIM].astype(BF16)
        lg = lax.dot_general(kb, qis, (((1,), (1,)), ((), ())), preferred_element_type=F32) * IDX_SCALE
        wl = jnp.maximum(lg, 0.0) * w_row
        sc = wl[:, 0:bq]
        for h in range(1, IDX_HEADS):
            sc = sc + wl[:, h * bq:(h + 1) * bq]
        sc = jnp.where(k0 + krow <= qpos, sc, NEG_INF)
        key_ref[pl.ds(k0, kc), :] = _order_key(sc)
        return 0

    lax.fori_loop(0, n_chunks, score_body, 0)

    def count(pred):
        def body(c, acc):
            k0 = pl.multiple_of(c * kc, kc)
            hit = jnp.where(pred(key_ref[pl.ds(k0, kc), :], k0 + krow), 1, 0)
            return acc + jnp.sum(hit.reshape(kc // 8, 8, bq), axis=0)
        acc = lax.fori_loop(0, n_chunks, body, jnp.zeros((8, bq), jnp.int32))
        return jnp.sum(acc, axis=0, keepdims=True)

    k_top = min(DSA_TOPK, s_len // 4)
    row1 = (1, bq)

    def select(_):
        thr = _radix_kth(lambda cand: count(lambda key, idx: key >= cand), k_top, row1)
        need = k_top - count(lambda key, idx: key > thr)
        n_eq = count(lambda key, idx: key == thr)
        n_bits = max(1, int(s_len).bit_length())
        cut = lax.cond(
            jnp.any(n_eq != need),
            lambda _: _tie_cut(lambda m: count(lambda key, idx: (key == thr) & (idx < m)), need, row1, n_bits),
            lambda _: jnp.full(row1, s_len, jnp.int32), 0)
        return thr, cut

    thr, cut = lax.cond(q0 + bq > k_top, select,
                        lambda _: (jnp.full(row1, INT_MIN, jnp.int32), jnp.full(row1, s_len, jnp.int32)), 0)

    def mask_body(c, _):
        k0 = pl.multiple_of(c * kc, kc)
        key = key_ref[pl.ds(k0, kc), :]
        idx = k0 + krow
        sel = ((key > thr) | ((key == thr) & (idx < cut))) & (idx <= qpos)
        mask_ref[pl.ds(k0, kc), :] = jnp.where(sel, 1.0, 0.0)
        return 0

    lax.fori_loop(0, n_chunks, mask_body, 0)

    def mask_fn(k0):
        return mask_ref[pl.ds(k0, kc), :] > 0.5

    for g in range(DSA_KV):
        qs = _stack_heads(q_ref, g, bq)
        ot = _flash_t(qs, kv_ref, g, 0, n_chunks, kc, mask_fn, bq)
        _store_heads(o_ref, ot, g, bq, None, 0)


def dsa_prompt(q, kv4, idxm, n, s):
    bq = kc = 256
    nq = s // bq
    t = n * s
    return pl.pallas_call(
        functools.partial(_dsa_prompt_kernel, bq=bq, kc=kc, s_len=s),
        grid=(n, nq),
        in_specs=[pl.BlockSpec((bq, BRANCH_W), lambda b, i: (b * nq + i, 1)),
                  pl.BlockSpec((bq, IDX_HEADS * IDX_DIM), lambda b, i: (b * nq + i, 0)),
                  pl.BlockSpec((s, LANE), lambda b, i: (b, 4)),
                  pl.BlockSpec((None, s, KV_W), lambda b, i: (3, b, 0)),
                  pl.BlockSpec((bq, LANE), lambda b, i: (b * nq + i, 5))],
        out_specs=pl.BlockSpec((bq, BRANCH_W), lambda b, i: (b * nq + i, 0)),
        out_shape=jax.ShapeDtypeStruct((t, BRANCH_W), F32),
        scratch_shapes=[pltpu.VMEM((s, bq), jnp.int32), pltpu.VMEM((s, bq), F32)],
        compiler_params=_cp(("parallel", "arbitrary")),
        name="dsa_prompt",
    )(q, idxm, idxm, kv4, idxm)


def _page_copies(pt_ref, cache_ref, layer, buf_ref, sem_ref, seq, slot, n_pages, rows):
    def each(fn):
        def body(p, _):
            fn(pltpu.make_async_copy(cache_ref.at[layer, pt_ref[seq, p]],
                                     buf_ref.at[slot, pl.ds(p * rows, rows)], sem_ref.at[slot]))
            return 0
        lax.fori_loop(0, n_pages, body, 0)
    return each


def _gather_step(pt_ref, cache_ref, layer, buf_ref, sem_ref, n_pages, rows):
    b = pl.program_id(0)
    nb = pl.num_programs(0)
    slot = b % 2

    @pl.when(b == 0)
    def _():
        _page_copies(pt_ref, cache_ref, layer, buf_ref, sem_ref, 0, 0, n_pages, rows)(lambda cp: cp.start())

    @pl.when(b + 1 < nb)
    def _():
        _page_copies(pt_ref, cache_ref, layer, buf_ref, sem_ref, b + 1, 1 - slot, n_pages, rows)(lambda cp: cp.start())

    _page_copies(pt_ref, cache_ref, layer, buf_ref, sem_ref, b, slot, n_pages, rows)(lambda cp: cp.wait())
    return slot


def _head_column(row, offset, stride):
    lane = lax.broadcasted_iota(jnp.int32, (8, LANE), 1)
    h = lax.broadcasted_iota(jnp.int32, (8, LANE), 0)
    return jnp.sum(jnp.where(lane == offset + stride * h, jnp.broadcast_to(row, (8, LANE)), 0.0), axis=1, keepdims=True)


def _q8(q_ref, width):
    return jnp.concatenate([q_ref[0:1, h * width:(h + 1) * width] for h in range(8)], axis=0)


def _row_spec(width, blk, n_extra):
    if n_extra:
        return pl.BlockSpec((None, 1, width), lambda b, pt: (b, 0, blk))
    return pl.BlockSpec((None, 1, width), lambda b: (b, 0, blk))


def _cmp_dec_kernel(pt_ref, cache_ref, q_ref, gates_ref, w1_ref, w2_ref, pe_ref, ov_ref, o_ref, ps_ref,
                    buf_ref, sem_ref, *, layer, n_pages, p0):
    b = pl.program_id(0)
    rows = PAGE_SIZE // CMP_STRIDE
    nc = n_pages * rows
    slot = _gather_step(pt_ref, cache_ref, layer, buf_ref, sem_ref, n_pages, rows)

    def read_x(t, kg):
        return buf_ref[slot, :, t * KV_W + kg * HEAD_DIM:t * KV_W + (kg + 1) * HEAD_DIM]

    k_c, v_c = _compress(read_x, w1_ref, w2_ref, pe_ref, nc)
    q8 = _q8(q_ref, HEAD_DIM).astype(BF16)
    head = lax.broadcasted_iota(jnp.int32, (8, 1), 0)
    blk_end = lax.broadcasted_iota(jnp.int32, (8, nc), 1) * CMP_STRIDE + (CMP_LEN - 1)
    mk = blk_end <= p0
    s = jnp.zeros((8, nc), F32)
    for g in range(NSA_KV):
        sg = lax.dot_general(q8, k_c[g].astype(BF16), (((1,), (1,)), ((), ())), preferred_element_type=F32)
        s = jnp.where(head // NSA_REP == g, sg, s)
    s = jnp.where(mk, s * ATTN_SCALE, NEG_INF)
    m = jnp.max(s, axis=1, keepdims=True)
    e = jnp.where(mk, jnp.exp(s - m), 0.0)
    p = e / jnp.maximum(jnp.sum(e, axis=1, keepdims=True), 1e-30)
    o = jnp.zeros((8, HEAD_DIM), F32)
    for g in range(NSA_KV):
        og = jnp.dot(p.astype(BF16), v_c[g].astype(BF16), preferred_element_type=F32)
        o = jnp.where(head // NSA_REP == g, og, o)
        psum = jnp.sum(jnp.where(head // NSA_REP == g, p, 0.0), axis=0, keepdims=True)
        ps8 = jnp.dot(jnp.broadcast_to(psum, (8, nc)), ov_ref[...], precision=lax.Precision.HIGHEST,
                      preferred_element_type=F32)
        ps_ref[g:g + 1, :] = ps8[0:1, :]
    gate = _sigmoid(_head_column(gates_ref[...], 0, 3))
    o_ref[...] = o * gate


def cmp_decode(page_table, cache_x, layer, q, idxm, w1cat, w2, pe8, ov, p0):
    n, n_pages = page_table.shape
    rows = PAGE_SIZE // CMP_STRIDE
    nc = n_pages * rows
    n_slc_pad = ov.shape[1]
    full = lambda a: pl.BlockSpec(a.shape, lambda b, pt: (0,) * a.ndim)
    gs = pltpu.PrefetchScalarGridSpec(
        num_scalar_prefetch=1,
        grid=(n,),
        in_specs=[pl.BlockSpec(memory_space=pl.ANY), _row_spec(BRANCH_W, 0, 1), _row_spec(LANE, 5, 1),
                  full(w1cat), full(w2), full(pe8), full(ov)],
        out_specs=[pl.BlockSpec((None, NSA_HEADS, HEAD_DIM), lambda b, pt: (b, 0, 0)),
                   pl.BlockSpec((None, NSA_KV, n_slc_pad), lambda b, pt: (b, 0, 0))],
        scratch_shapes=[pltpu.VMEM((2, nc, CMP_FEAT), F32), pltpu.SemaphoreType.DMA((2,))],
    )
    return pl.pallas_call(
        functools.partial(_cmp_dec_kernel, layer=layer, n_pages=n_pages, p0=p0),
        grid_spec=gs,
        out_shape=[jax.ShapeDtypeStruct((n, NSA_HEADS, HEAD_DIM), F32),
                   jax.ShapeDtypeStruct((n, NSA_KV, n_slc_pad), F32)],
        compiler_params=_cp(("arbitrary",)),
        name="nsa_cmp_decode",
    )(page_table, cache_x, q, idxm, w1cat, w2, pe8, ov)


def _slc_mask_kernel(ps_ref, e_ref, o_ref, *, n_slc, p0):
    sc = ps_ref[...]
    j = lax.broadcasted_iota(jnp.int32, sc.shape, 1)
    cur = p0 // SEL_BLOCK
    forced = (j == 0) | (j == cur) | (j == cur - 1)
    visible = j * SEL_BLOCK <= p0
    sc = jnp.where(j >= n_slc, BELOW_ALL, jnp.where(forced, FORCE_SCORE, jnp.where(visible, sc, NEG_INF)))
    rank = jnp.zeros(sc.shape, jnp.int32)
    for i in range(n_slc):
        si = sc[:, i:i + 1]
        rank = rank + jnp.where(si > sc, 1, jnp.where((si == sc) & (j > i), 1, 0))
    sel = jnp.where((rank < min(N_SEL, n_slc)) & (j < n_slc), 1.0, 0.0).astype(BF16)
    o_ref[...] = jnp.dot(sel, e_ref[...], preferred_element_type=F32)


def slc_mask_decode(p_slc, expand, n_slc, p0):
    n, g, w = p_slc.shape
    l_pad = expand.shape[1]
    out = pl.pallas_call(
        functools.partial(_slc_mask_kernel, n_slc=n_slc, p0=p0),
        grid=(1,),
        in_specs=[pl.BlockSpec((n * g, w), lambda i: (0, 0)), pl.BlockSpec(expand.shape, lambda i: (0, 0))],
        out_specs=pl.BlockSpec((n * g, l_pad), lambda i: (0, 0)),
        out_shape=jax.ShapeDtypeStruct((n * g, l_pad), F32),
        compiler_params=_cp(("arbitrary",)),
        name="nsa_slc_mask_decode",
    )(p_slc.reshape(n * g, w), expand)
    return out.reshape(n, g, l_pad)


def _idx_score_kernel(pt_ref, cache_ref, qi_ref, knew_ref, gates_ref, o_ref, buf_ref, sem_ref,
                      *, layer, n_pages, p0):
    b = pl.program_id(0)
    past = n_pages * PAGE_SIZE
    l_pad = past + LANE
    slot = _gather_step(pt_ref, cache_ref, layer, buf_ref, sem_ref, n_pages, PAGE_SIZE)
    buf_ref[slot, past:l_pad, :] = jnp.zeros((LANE, IDX_DIM), F32)
    buf_ref[slot, past:past + 1, :] = knew_ref[0:1, 0:IDX_DIM]
    q8 = _q8(qi_ref, IDX_DIM).astype(BF16)
    lg = lax.dot_general(q8, buf_ref[slot].astype(BF16), (((1,), (1,)), ((), ())),
                         preferred_element_type=F32) * IDX_SCALE
    w_col = _head_column(gates_ref[...], 24, 1) * (IDX_HEADS ** -0.5)
    sc = jnp.sum(jnp.maximum(lg, 0.0) * w_col, axis=0, keepdims=True)
    key = lax.broadcasted_iota(jnp.int32, (1, l_pad), 1)
    o_ref[...] = jnp.where(key <= p0, sc, BELOW_ALL)


def idx_score_decode(page_table, cache_idx, layer, idxm, p0):
    n, n_pages = page_table.shape
    l_pad = n_pages * PAGE_SIZE + LANE
    gs = pltpu.PrefetchScalarGridSpec(
        num_scalar_prefetch=1,
        grid=(n,),
        in_specs=[pl.BlockSpec(memory_space=pl.ANY), _row_spec(IDX_HEADS * IDX_DIM, 0, 1),
                  _row_spec(LANE, 4, 1), _row_spec(LANE, 5, 1)],
        out_specs=pl.BlockSpec((None, 1, l_pad), lambda b, pt: (b, 0, 0)),
        scratch_shapes=[pltpu.VMEM((2, l_pad, IDX_DIM), F32), pltpu.SemaphoreType.DMA((2,))],
    )
    return pl.pallas_call(
        functools.partial(_idx_score_kernel, layer=layer, n_pages=n_pages, p0=p0),
        grid_spec=gs,
        out_shape=jax.ShapeDtypeStruct((n, 1, l_pad), F32),
        compiler_params=_cp(("arbitrary",)),
        name="dsa_idx_score_decode",
    )(page_table, cache_idx, idxm, idxm, idxm)


def _dsa_mask_kernel(sc_ref, o_ref, *, k_top, p0):
    key = _order_key(sc_ref[...])
    n, l_pad = key.shape
    idx = lax.broadcasted_iota(jnp.int32, key.shape, 1)
    col = (n, 1)

    def count(pred):
        return jnp.sum(jnp.where(pred, 1, 0), axis=1, keepdims=True)

    thr = _radix_kth(lambda cand: count(key >= cand), k_top, col)
    need = k_top - count(key > thr)
    cut = _tie_cut(lambda m: count((key == thr) & (idx < m)), need, col, max(1, int(l_pad).bit_length()))
    sel = ((key > thr) | ((key == thr) & (idx < cut))) & (idx <= p0)
    mask = jnp.where(sel, 1.0, 0.0)
    for g in range(DSA_KV):
        o_ref[:, g, :] = mask


def dsa_mask_decode(score, k_top, p0):
    n, _, l_pad = score.shape
    return pl.pallas_call(
        functools.partial(_dsa_mask_kernel, k_top=k_top, p0=p0),
        grid=(1,),
        in_specs=[pl.BlockSpec((n, l_pad), lambda i: (0, 0))],
        out_specs=pl.BlockSpec((n, DSA_KV, l_pad), lambda i: (0, 0, 0)),
        out_shape=jax.ShapeDtypeStruct((n, DSA_KV, l_pad), F32),
        compiler_params=_cp(("arbitrary",)),
        name="dsa_mask_decode",
    )(score.reshape(n, l_pad))


def _attend_rows(q8, kv_rows, mask_ref):
    head = lax.broadcasted_iota(jnp.int32, (8, 1), 0)
    o = jnp.zeros((8, HEAD_DIM), F32)
    for g in range(NSA_KV):
        kb = kv_rows[:, g * HEAD_DIM:(g + 1) * HEAD_DIM].astype(BF16)
        vb = kv_rows[:, (NSA_KV + g) * HEAD_DIM:(NSA_KV + g + 1) * HEAD_DIM].astype(BF16)
        s = lax.dot_general(q8, kb, (((1,), (1,)), ((), ())), preferred_element_type=F32) * ATTN_SCALE
        mk = mask_ref[g:g + 1, :] > 0.5
        s = jnp.where(mk, s, NEG_INF)
        m = jnp.max(s, axis=1, keepdims=True)
        e = jnp.where(mk, jnp.exp(s - m), 0.0)
        p = e / jnp.maximum(jnp.sum(e, axis=1, keepdims=True), 1e-30)
        og = jnp.dot(p.astype(BF16), vb, preferred_element_type=F32)
        o = jnp.where(head // NSA_REP == g, og, o)
    return o


def _attn_paged_kernel(pt_ref, cache_ref, q_ref, kvnew_ref, mask_ref, gates_ref, o_ref, buf_ref, sem_ref,
                       *, layer, n_pages, branch):
    b = pl.program_id(0)
    past = n_pages * PAGE_SIZE
    l_pad = past + LANE
    slot = _gather_step(pt_ref, cache_ref, layer, buf_ref, sem_ref, n_pages, PAGE_SIZE)
    buf_ref[slot, past:l_pad, :] = jnp.zeros((LANE, KV_W), F32)
    buf_ref[slot, past:past + 1, :] = kvnew_ref[...]
    o = _attend_rows(_q8(q_ref, HEAD_DIM).astype(BF16), buf_ref.at[slot], mask_ref)
    if branch is not None:
        o = o * _sigmoid(_head_column(gates_ref[...], branch, 3))
    o_ref[...] = o


def attn_paged_decode(page_table, cache, layer, q, q_blk, kvnew, mask, idxm, branch):
    n, n_pages = page_table.shape
    l_pad = n_pages * PAGE_SIZE + LANE
    gs = pltpu.PrefetchScalarGridSpec(
        num_scalar_prefetch=1,
        grid=(n,),
        in_specs=[pl.BlockSpec(memory_space=pl.ANY), _row_spec(BRANCH_W, q_blk, 1), _row_spec(KV_W, 0, 1),
                  pl.BlockSpec((None, NSA_KV, l_pad), lambda b, pt: (b, 0, 0)), _row_spec(LANE, 5, 1)],
        out_specs=pl.BlockSpec((None, NSA_HEADS, HEAD_DIM), lambda b, pt: (b, 0, 0)),
        scratch_shapes=[pltpu.VMEM((2, l_pad, KV_W), F32), pltpu.SemaphoreType.DMA((2,))],
    )
    return pl.pallas_call(
        functools.partial(_attn_paged_kernel, layer=layer, n_pages=n_pages, branch=branch),
        grid_spec=gs,
        out_shape=jax.ShapeDtypeStruct((n, NSA_HEADS, HEAD_DIM), F32),
        compiler_params=_cp(("arbitrary",)),
        name="attn_paged_decode",
    )(page_table, cache, q, kvnew, mask, idxm)


def _attn_win_kernel(st_ref, q_ref, kvnew_ref, gates_ref, o_ref, buf_ref, mask_ref, *, wb):
    b = pl.program_id(0)
    l_pad = wb + LANE
    buf_ref[0:wb, :] = st_ref[...]
    buf_ref[wb:l_pad, :] = jnp.zeros((LANE, KV_W), F32)
    buf_ref[wb:wb + 1, :] = kvnew_ref[...]
    key = lax.broadcasted_iota(jnp.int32, (NSA_KV, l_pad), 1)
    mask_ref[...] = jnp.where(key <= wb, 1.0, 0.0)
    o = _attend_rows(_q8(q_ref, HEAD_DIM).astype(BF16), buf_ref, mask_ref)
    o_ref[...] = o * _sigmoid(_head_column(gates_ref[...], 2, 3))


def attn_win_decode(state, layer, q, kvnew, idxm):
    n, wb = state.shape[1], state.shape[2]
    l_pad = wb + LANE
    return pl.pallas_call(
        functools.partial(_attn_win_kernel, wb=wb),
        grid=(n,),
        in_specs=[pl.BlockSpec((None, None, wb, KV_W), lambda b: (layer, b, 0, 0)),
                  _row_spec(BRANCH_W, 0, 0), _row_spec(KV_W, 0, 0), _row_spec(LANE, 5, 0)],
        out_specs=pl.BlockSpec((None, NSA_HEADS, HEAD_DIM), lambda b: (b, 0, 0)),
        out_shape=jax.ShapeDtypeStruct((n, NSA_HEADS, HEAD_DIM), F32),
        scratch_shapes=[pltpu.VMEM((l_pad, KV_W), F32), pltpu.VMEM((NSA_KV, l_pad), F32)],
        compiler_params=_cp(("parallel",)),
        name="attn_win_decode",
    )(state, q, kvnew, idxm)


def _rope_tables(pos, head_dim):
    d_rot = head_dim // ROPE_FRACTION
    half = d_rot // 2
    inv_freq = jnp.exp(jnp.arange(half, dtype=F32) * (-2.0 * math.log(ROPE_THETA) / d_rot))
    ang = pos.astype(F32)[:, None] * inv_freq[None, :]
    cos, sin = jnp.cos(ang), jnp.sin(ang)
    lane = np.arange(LANE) % head_dim
    j = lane % half
    first = jnp.asarray(lane < half)[None, :]
    second = jnp.asarray((lane >= half) & (lane < d_rot))[None, :]
    c = jnp.where(first | second, cos[:, j], 1.0)
    s1 = jnp.where(first, -sin[:, j], 0.0)
    s2 = jnp.where(second, sin[:, j], 0.0)
    return (c, s1, s2), half


def _split_w_in(w_in):
    sizes = (NSA_HEADS * HEAD_DIM, KV_W, KV_W, KV_W, 3 * NSA_HEADS, DSA_HEADS * HEAD_DIM, KV_W,
             IDX_HEADS * IDX_DIM, IDX_DIM, IDX_HEADS, CONV_DIM, CONV_DIM, CONV_DIM, N_BRANCH * D_MODEL)
    offs = np.concatenate([[0], np.cumsum(sizes)])
    col = lambda i: w_in[:, int(offs[i]):int(offs[i + 1])]
    (q_a, cmp_kv, slc_kv, win_kv, gate_a, q_b, dsa_kv, q_i, k_i, w_i, cu, cb, cc, gm) = [col(i) for i in range(14)]
    d = w_in.shape[0]
    zeros = lambda n: jnp.zeros((d, n), w_in.dtype)
    w_q = jnp.concatenate([q_a, q_b], axis=1).astype(BF16)
    w_kv = jnp.concatenate([cmp_kv, slc_kv, win_kv, dsa_kv], axis=1).astype(BF16)
    w_idx = jnp.concatenate([q_i, k_i, zeros(LANE - IDX_DIM), gate_a, w_i, zeros(LANE - 3 * NSA_HEADS - IDX_HEADS)],
                            axis=1).astype(BF16)
    w_cg = jnp.concatenate([cu, cb, cc, gm], axis=1).astype(BF16)
    return w_q, w_kv, w_idx, w_cg


def _cmp_weights(w1, w2, pe):
    half = CMP_STRIDE * HEAD_DIM
    wa = w1[:, :half].reshape(2, CMP_STRIDE, HEAD_DIM, HEAD_DIM)
    wb = w1[:, half:].reshape(2, CMP_STRIDE, HEAD_DIM, HEAD_DIM)
    w1cat = jnp.concatenate([wa, wb], axis=-1).astype(BF16)
    pe8 = jnp.zeros((2, CMP_STRIDE, 8, HEAD_DIM), F32)
    pe8 = pe8.at[:, :, 0, :].set(pe[:, :CMP_STRIDE]).at[:, :, 1, :].set(pe[:, CMP_STRIDE:])
    return w1cat, w2.astype(BF16), pe8.astype(BF16)


def _overlap(n_cmp_rows, n_slc, seq_len):
    n_cmp = seq_len // CMP_STRIDE - 1
    c = np.arange(n_cmp_rows)
    c_start = c * CMP_STRIDE
    s_start = np.arange(n_slc) * SEL_BLOCK
    ov = ((c_start[:, None] < s_start[None, :] + SEL_BLOCK) & (c_start[:, None] + CMP_LEN > s_start[None, :])
          & (c[:, None] < n_cmp))
    return ov.astype(np.float32)


def _project(xn, wts, tabs128, half128, tabs64, half64):
    w_q, w_kv, w_idx, w_cg = wts
    q = proj_rope(xn, w_q, tabs128, bn=1024, half=half128, rope_blocks=(True,) * 8, stacked=False)
    kv4 = proj_rope(xn, w_kv, tabs128, bn=KV_W, half=half128, rope_blocks=(True, True, False, False), stacked=True)
    idxm = proj_rope(xn, w_idx, tabs64, bn=w_idx.shape[1], half=half64,
                     rope_blocks=(True,) * 5 + (False,), stacked=False)
    cg = matmul(xn, w_cg, bn=1024)
    return q, kv4, idxm, cg


def _finish_layer(x, branches, oc, cg, lw, g_next):
    a1, a2, a3, ob = branches
    merged = merge_branches(a1, a2, a3, ob, oc, lw["w_branch"], cg)
    x_mid, hn = outproj_residual(merged, lw["w_out"], x, lw["g_mix_post"], lw["g_ffn_pre"])
    act = ffn_gate_up(hn, lw["w_gu"])
    return ffn_down_residual(act, lw["w_down"], x_mid, lw["g_ffn_post"], g_next)


def _prompt_layer(x, xn, lw, n, s, consts, g_next):
    q, kv4, idxm, cg = _project(xn, lw["w_in"], *consts["rope_p"])
    cmp_x = kv4.reshape(4 * n * s // CMP_STRIDE, CMP_FEAT)
    a1, p_slc = cmp_prompt(cmp_x, q, idxm, *lw["cmp"], consts["ov_p_t"], n, s)
    a2 = slc_prompt(q, kv4, p_slc, idxm, n, s)
    a3 = win_prompt(q, kv4, idxm, n, s)
    ob = dsa_prompt(q, kv4, idxm, n, s)
    oc, conv_state = conv_prompt(cg, lw["conv_w"], n, s)
    y, xn_next = _finish_layer(x, (a1, a2, a3, ob), oc, cg, lw, g_next)
    kv5 = lambda a: a.reshape(n, s, 2, NSA_KV, HEAD_DIM)
    keep = min(WINDOW, s)
    state = (kv5(kv4[0]), kv5(kv4[1]), kv5(kv4[2])[:, s - keep:], kv5(kv4[3]),
             idxm[:, IDX_HEADS * IDX_DIM:IDX_HEADS * IDX_DIM + IDX_DIM].reshape(n, s, IDX_DIM), conv_state)
    return y, xn_next, state


def _sample_layer(x, xn, lw, layer, caches, page_table, consts, g_next):
    n = x.shape[0]
    p0 = consts["p0"]
    q, kv4, idxm, cg = _project(xn, lw["w_in"], *consts["rope_s"])
    cache_cmp_x, cache_slc, state_win, cache_dsa, cache_idx, state_conv = caches
    q3 = q.reshape(n, 1, -1)
    idx3 = idxm.reshape(n, 1, -1)
    new_row = lambda i: kv4[i].reshape(n, 1, KV_W)
    o_cmp, p_slc = cmp_decode(page_table, cache_cmp_x, layer, q3, idx3, *lw["cmp"], consts["ov_s"], p0)
    slc_mask = slc_mask_decode(p_slc, consts["expand"], consts["n_slc_s"], p0)
    o_slc = attn_paged_decode(page_table, cache_slc, layer, q3, 0, new_row(1), slc_mask, idx3, 1)
    o_win = attn_win_decode(state_win, layer, q3, new_row(2), idx3)
    score = idx_score_decode(page_table, cache_idx, layer, idx3, p0)
    dsa_mask = dsa_mask_decode(score, min(DSA_TOPK, (p0 + 1) // 4), p0)
    o_dsa = attn_paged_decode(page_table, cache_dsa, layer, q3, 1, new_row(3), dsa_mask, idx3, None)
    oc, conv_state = conv_decode(cg, state_conv[layer], lw["conv_w"])
    flat = lambda a: a.reshape(n, BRANCH_W)
    y, xn_next = _finish_layer(x, (flat(o_cmp), flat(o_slc), flat(o_win), flat(o_dsa)), oc, cg, lw, g_next)
    kv5 = lambda a: a.reshape(n, 1, 2, NSA_KV, HEAD_DIM)
    win_all = jnp.concatenate([state_win[layer].reshape(n, -1, 2, NSA_KV, HEAD_DIM), kv5(kv4[2])], axis=1)
    keep = min(WINDOW, win_all.shape[1])
    state = (kv5(kv4[0]), kv5(kv4[1]), win_all[:, win_all.shape[1] - keep:], kv5(kv4[3]),
             idxm[:, IDX_HEADS * IDX_DIM:IDX_HEADS * IDX_DIM + IDX_DIM].reshape(n, 1, IDX_DIM), conv_state)
    return y, xn_next, state


def kernel(x_prompt, x_sample, cache_nsa_cmp_kv, cache_nsa_slc_kv, state_nsa_win_kv, cache_dsa_kv, cache_dsa_idx_k, state_conv, page_table, norm_mix_pre, norm_mix_post, norm_ffn_pre, norm_ffn_post, w_in, cmp_w1, cmp_w2, cmp_pe, conv_w, w_branch, w_out, ffn_w_gate_up, ffn_w_down):
    n_p, s, d = x_prompt.shape
    n_s = x_sample.shape[0]
    depth = w_in.shape[0]
    n_pages = page_table.shape[1]
    n_pool = cache_nsa_cmp_kv.shape[1]
    p0 = n_pages * PAGE_SIZE
    l_s = p0 + 1
    l_pad = p0 + LANE
    n_slc_s = -(-l_s // SEL_BLOCK)
    n_slc_pad = -(-n_slc_s // LANE) * LANE
    nc_s = p0 // CMP_STRIDE

    tabs128_p, half128 = _rope_tables(jnp.arange(s, dtype=jnp.int32), HEAD_DIM)
    tabs64_p, half64 = _rope_tables(jnp.arange(s, dtype=jnp.int32), IDX_DIM)
    tabs128_s, _ = _rope_tables(jnp.full((n_s,), p0, jnp.int32), HEAD_DIM)
    tabs64_s, _ = _rope_tables(jnp.full((n_s,), p0, jnp.int32), IDX_DIM)
    key_block = np.arange(l_pad) // SEL_BLOCK
    expand = ((key_block[None, :] == np.arange(n_slc_pad)[:, None]) & (np.arange(l_pad)[None, :] <= p0))
    consts = {
        "p0": p0,
        "n_slc_s": n_slc_s,
        "rope_p": (tabs128_p, half128, tabs64_p, half64),
        "rope_s": (tabs128_s, half128, tabs64_s, half64),
        "ov_p_t": jnp.asarray(_overlap(s // CMP_STRIDE, -(-s // SEL_BLOCK), s).T),
        "ov_s": jnp.asarray(_overlap(nc_s, n_slc_pad, l_s) * (np.arange(n_slc_pad) < n_slc_s)[None, :]),
        "expand": jnp.asarray(expand.astype(np.float32)).astype(BF16),
    }
    caches = (cache_nsa_cmp_kv.reshape(depth, n_pool, PAGE_SIZE // CMP_STRIDE, CMP_FEAT),
              cache_nsa_slc_kv.reshape(depth, n_pool, PAGE_SIZE, KV_W),
              state_nsa_win_kv.reshape(depth, n_s, -1, KV_W),
              cache_dsa_kv.reshape(depth, n_pool, PAGE_SIZE, KV_W),
              cache_dsa_idx_k, state_conv)

    x_p = x_prompt.reshape(n_p * s, d)
    x_s = x_sample.reshape(n_s, d)
    xn_p = rmsnorm(x_p, norm_mix_pre[0])
    xn_s = rmsnorm(x_s, norm_mix_pre[0])
    new_p, new_s = [], []
    for l in range(depth):
        lw = {
            "w_in": _split_w_in(w_in[l]),
            "cmp": _cmp_weights(cmp_w1[l], cmp_w2[l], cmp_pe[l]),
            "conv_w": conv_w[l],
            "w_branch": w_branch[l].astype(BF16),
            "w_out": w_out[l].astype(BF16),
            "w_gu": ffn_w_gate_up[l].astype(BF16),
            "w_down": ffn_w_down[l].astype(BF16),
            "g_mix_post": norm_mix_post[l], "g_ffn_pre": norm_ffn_pre[l], "g_ffn_post": norm_ffn_post[l],
        }
        g_next = norm_mix_pre[l + 1] if l + 1 < depth else norm_mix_pre[l]
        x_p, xn_p, st_p = _prompt_layer(x_p, xn_p, lw, n_p, s, consts, g_next)
        x_s, xn_s, st_s = _sample_layer(x_s, xn_s, lw, l, caches, page_table, consts, g_next)
        new_p.append(st_p)
        new_s.append(st_s)
    p_out = [jnp.stack([st[i] for st in new_p]) for i in range(6)]
    s_out = [jnp.stack([st[i] for st in new_s]) for i in range(6)]
    return (x_p.reshape(n_p, s, d), x_s.reshape(n_s, 1, d), *p_out, *s_out)
```

```python
import functools
import math

import numpy as np
import jax
import jax.numpy as jnp
from jax import lax
from jax.experimental import pallas as pl
from jax.experimental.pallas import tpu as pltpu

D_MODEL = 2048
HEAD_DIM = 128
BRANCH_W = D_MODEL // 2
N_BRANCH = 3
NSA_HEADS = BRANCH_W // HEAD_DIM
NSA_KV = 2
NSA_REP = NSA_HEADS // NSA_KV
CMP_STRIDE = 16
CMP_LEN = 2 * CMP_STRIDE
SEL_BLOCK = 64
N_SEL = 16
WINDOW = 512
DSA_HEADS = BRANCH_W // HEAD_DIM
DSA_KV = 2
IDX_HEADS = 8
IDX_DIM = 64
DSA_TOPK = 256
CONV_DIM = BRANCH_W
CONV_WIDTH = 3
FFN_HIDDEN = ((8 * D_MODEL + 3 * 256 - 1) // (3 * 256)) * 256
ROPE_THETA = 500000.0
ROPE_FRACTION = 4
RMS_EPS = 1e-6
ATTN_SCALE = HEAD_DIM ** -0.5
IDX_SCALE = IDX_DIM ** -0.5
NEG_INF = -1e30
FORCE_SCORE = 1e30
BELOW_ALL = -3.0e38
PAGE_SIZE = 128

LANE = 128
KV_W = 2 * NSA_KV * HEAD_DIM
CMP_FEAT = CMP_STRIDE * KV_W
VMEM_LIMIT = 60 * 1024 * 1024
INT_MIN = -(2 ** 31)

F32 = jnp.float32
BF16 = jnp.bfloat16


def _cp(sem, vmem=VMEM_LIMIT):
    return pltpu.CompilerParams(dimension_semantics=sem, vmem_limit_bytes=vmem)


def _pick(n, pref, mult=8):
    if n <= pref:
        return n
    for b in range(pref, 0, -1):
        if n % b == 0 and b % mult == 0:
            return b
    return n


def _sigmoid(x):
    return 1.0 / (1.0 + jnp.exp(-x))


def _rms(x, g):
    return x * lax.rsqrt(jnp.mean(x * x, axis=-1, keepdims=True) + RMS_EPS) * g


def _rmsnorm_kernel(x_ref, g_ref, o_ref):
    o_ref[...] = _rms(x_ref[...], g_ref[...]).astype(o_ref.dtype)


def rmsnorm(x, g, out_dtype=BF16):
    t, d = x.shape
    bm = _pick(t, 512)
    return pl.pallas_call(
        _rmsnorm_kernel,
        grid=(t // bm,),
        in_specs=[pl.BlockSpec((bm, d), lambda i: (i, 0)), pl.BlockSpec((1, d), lambda i: (0, 0))],
        out_specs=pl.BlockSpec((bm, d), lambda i: (i, 0)),
        out_shape=jax.ShapeDtypeStruct((t, d), out_dtype),
        compiler_params=_cp(("parallel",)),
        name="rmsnorm",
    )(x, g.reshape(1, d))


def _mm_rope_kernel(x_ref, w_ref, c_ref, s1_ref, s2_ref, o_ref, *, half, rope_blocks, interleave):
    y = jnp.dot(x_ref[...], w_ref[...], preferred_element_type=F32)
    bm = y.shape[0]
    nh = len(rope_blocks)
    c, s1, s2 = c_ref[...], s1_ref[...], s2_ref[...]
    for h, roped in enumerate(rope_blocks):
        yh = y[:, h * LANE:(h + 1) * LANE]
        if roped:
            yh = yh * c + pltpu.roll(yh, LANE - half, 1) * s1 + pltpu.roll(yh, half, 1) * s2
        if interleave:
            o_ref[pl.ds(h, bm, stride=nh), :] = yh
        else:
            o_ref[:, h * LANE:(h + 1) * LANE] = yh


def proj_rope(xn, w, tabs, *, bn, half, rope_blocks, stacked):
    t, k = xn.shape
    n = w.shape[1]
    tab_rows = tabs[0].shape[0]
    bm = _pick(math.gcd(t, tab_rows), 1024)
    tab_blocks = tab_rows // bm
    kern = functools.partial(_mm_rope_kernel, half=half, rope_blocks=rope_blocks, interleave=stacked)
    tab_spec = pl.BlockSpec((bm, LANE), lambda i, j: (i % tab_blocks, 0))
    if stacked:
        nh = bn // LANE
        out_shape = jax.ShapeDtypeStruct((n // bn, t * nh, LANE), F32)
        out_spec = pl.BlockSpec((None, bm * nh, LANE), lambda i, j: (j, i, 0))
    else:
        out_shape = jax.ShapeDtypeStruct((t, n), F32)
        out_spec = pl.BlockSpec((bm, bn), lambda i, j: (i, j))
    return pl.pallas_call(
        kern,
        grid=(t // bm, n // bn),
        in_specs=[pl.BlockSpec((bm, k), lambda i, j: (i, 0)), pl.BlockSpec((k, bn), lambda i, j: (0, j)),
                  tab_spec, tab_spec, tab_spec],
        out_specs=out_spec,
        out_shape=out_shape,
        compiler_params=_cp(("parallel", "arbitrary")),
        name="proj_rope",
    )(xn, w, *tabs)


def _mm_kernel(x_ref, w_ref, o_ref):
    o_ref[...] = jnp.dot(x_ref[...], w_ref[...], preferred_element_type=F32).astype(o_ref.dtype)


def matmul(x, w, *, bn, out_dtype=F32):
    t, k = x.shape
    n = w.shape[1]
    bm = _pick(t, 1024)
    return pl.pallas_call(
        _mm_kernel,
        grid=(t // bm, n // bn),
        in_specs=[pl.BlockSpec((bm, k), lambda i, j: (i, 0)), pl.BlockSpec((k, bn), lambda i, j: (0, j))],
        out_specs=pl.BlockSpec((bm, bn), lambda i, j: (i, j)),
        out_shape=jax.ShapeDtypeStruct((t, n), out_dtype),
        compiler_params=_cp(("parallel", "arbitrary")),
        name="proj_plain",
    )(x, w)


def _merge_kernel(a1_ref, a2_ref, a3_ref, ob_ref, oc_ref, w_ref, g0_ref, g1_ref, g2_ref, o_ref):
    xa = (a1_ref[...] + a2_ref[...] + a3_ref[...]).astype(BF16)
    acc = _sigmoid(g0_ref[...]) * jnp.dot(xa, w_ref[0], preferred_element_type=F32)
    acc += _sigmoid(g1_ref[...]) * jnp.dot(ob_ref[...].astype(BF16), w_ref[1], preferred_element_type=F32)
    acc += _sigmoid(g2_ref[...]) * jnp.dot(oc_ref[...], w_ref[2], preferred_element_type=F32)
    o_ref[...] = acc.astype(o_ref.dtype)


def merge_branches(a1, a2, a3, ob, oc, wb, cg):
    t = a1.shape[0]
    bm = _pick(t, 512)
    bn = 1024
    nj = D_MODEL // bn
    g_base = 3 * CONV_DIM // bn
    xs = pl.BlockSpec((bm, BRANCH_W), lambda i, j: (i, 0))

    def gspec(br):
        return pl.BlockSpec((bm, bn), lambda i, j: (i, g_base + br * nj + j))

    return pl.pallas_call(
        _merge_kernel,
        grid=(t // bm, nj),
        in_specs=[xs, xs, xs, xs, xs, pl.BlockSpec((N_BRANCH, BRANCH_W, bn), lambda i, j: (0, 0, j)),
                  gspec(0), gspec(1), gspec(2)],
        out_specs=pl.BlockSpec((bm, bn), lambda i, j: (i, j)),
        out_shape=jax.ShapeDtypeStruct((t, D_MODEL), BF16),
        compiler_params=_cp(("parallel", "arbitrary")),
        name="merge",
    )(a1, a2, a3, ob, oc, wb, cg, cg, cg)


def _outproj_kernel(m_ref, w_ref, x_ref, gpost_ref, gpre_ref, xo_ref, hn_ref):
    y = jnp.dot(m_ref[...], w_ref[...], preferred_element_type=F32)
    xn = x_ref[...] + _rms(y, gpost_ref[...])
    xo_ref[...] = xn
    hn_ref[...] = _rms(xn, gpre_ref[...]).astype(hn_ref.dtype)


def outproj_residual(merged, w_out, x, g_post, g_ffn_pre):
    t, d = x.shape
    bm = _pick(t, 512)
    row = pl.BlockSpec((bm, d), lambda i: (i, 0))
    vec = pl.BlockSpec((1, d), lambda i: (0, 0))
    return pl.pallas_call(
        _outproj_kernel,
        grid=(t // bm,),
        in_specs=[row, pl.BlockSpec((d, d), lambda i: (0, 0)), row, vec, vec],
        out_specs=[row, row],
        out_shape=[jax.ShapeDtypeStruct((t, d), F32), jax.ShapeDtypeStruct((t, d), BF16)],
        compiler_params=_cp(("parallel",)),
        name="outproj",
    )(merged, w_out, x, g_post.reshape(1, d), g_ffn_pre.reshape(1, d))


def _ffn_gu_kernel(h_ref, wg_ref, wu_ref, o_ref):
    h = h_ref[...]
    g = jnp.dot(h, wg_ref[...], preferred_element_type=F32)
    u = jnp.dot(h, wu_ref[...], preferred_element_type=F32)
    o_ref[...] = (g * _sigmoid(g) * u).astype(o_ref.dtype)


def ffn_gate_up(hn, w_gu):
    t, d = hn.shape
    bm = _pick(t, 1024)
    bn = 512
    nj = FFN_HIDDEN // bn
    return pl.pallas_call(
        _ffn_gu_kernel,
        grid=(t // bm, nj),
        in_specs=[pl.BlockSpec((bm, d), lambda i, j: (i, 0)), pl.BlockSpec((d, bn), lambda i, j: (0, j)),
                  pl.BlockSpec((d, bn), lambda i, j: (0, nj + j))],
        out_specs=pl.BlockSpec((bm, bn), lambda i, j: (i, j)),
        out_shape=jax.ShapeDtypeStruct((t, FFN_HIDDEN), BF16),
        compiler_params=_cp(("parallel", "arbitrary")),
        name="ffn_gate_up",
    )(hn, w_gu, w_gu)


def _ffn_down_kernel(a_ref, w_ref, x_ref, gpost_ref, gnext_ref, y_ref, xn_ref, acc_ref, *, nk):
    k = pl.program_id(1)

    @pl.when(k == 0)
    def _():
        acc_ref[...] = jnp.zeros_like(acc_ref)

    acc_ref[...] += jnp.dot(a_ref[...], w_ref[...], preferred_element_type=F32)

    @pl.when(k == nk - 1)
    def _():
        y = x_ref[...] + _rms(acc_ref[...], gpost_ref[...])
        y_ref[...] = y
        xn_ref[...] = _rms(y, gnext_ref[...]).astype(xn_ref.dtype)


def ffn_down_residual(act, w_down, x, g_post, g_next):
    t, d = x.shape
    bm = _pick(t, 512)
    bk = 1408
    nk = FFN_HIDDEN // bk
    row = pl.BlockSpec((bm, d), lambda i, k: (i, 0))
    vec = pl.BlockSpec((1, d), lambda i, k: (0, 0))
    return pl.pallas_call(
        functools.partial(_ffn_down_kernel, nk=nk),
        grid=(t // bm, nk),
        in_specs=[pl.BlockSpec((bm, bk), lambda i, k: (i, k)), pl.BlockSpec((bk, d), lambda i, k: (k, 0)),
                  row, vec, vec],
        out_specs=[row, row],
        out_shape=[jax.ShapeDtypeStruct((t, d), F32), jax.ShapeDtypeStruct((t, d), BF16)],
        scratch_shapes=[pltpu.VMEM((bm, d), F32)],
        compiler_params=_cp(("parallel", "arbitrary")),
        name="ffn_down",
    )(act, w_down, x, g_post.reshape(1, d), g_next.reshape(1, d))


def _conv_kernel(u_ref, b_ref, c_ref, w_ref, o_ref, st_ref):
    v = c_ref[...] * u_ref[...]
    s = v.shape[0]
    row = lax.broadcasted_iota(jnp.int32, v.shape, 0)
    v1 = jnp.where(row >= 1, pltpu.roll(v, 1, 0), 0.0)
    v2 = jnp.where(row >= 2, pltpu.roll(v, 2, 0), 0.0)
    w = w_ref[...]
    y = w[0:1] * v2 + w[1:2] * v1 + w[2:3] * v
    o_ref[...] = (b_ref[...] * y).astype(o_ref.dtype)
    st_ref[...] = v[s - (CONV_WIDTH - 1):, :]


def conv_prompt(cg, conv_w, n, s):
    bc = 256
    nj = CONV_DIM // bc
    return pl.pallas_call(
        _conv_kernel,
        grid=(n, nj),
        in_specs=[pl.BlockSpec((s, bc), lambda b, j: (b, j)), pl.BlockSpec((s, bc), lambda b, j: (b, nj + j)),
                  pl.BlockSpec((s, bc), lambda b, j: (b, 2 * nj + j)),
                  pl.BlockSpec((CONV_WIDTH, bc), lambda b, j: (0, j))],
        out_specs=[pl.BlockSpec((s, bc), lambda b, j: (b, j)),
                   pl.BlockSpec((None, CONV_WIDTH - 1, bc), lambda b, j: (b, 0, j))],
        out_shape=[jax.ShapeDtypeStruct((n * s, CONV_DIM), BF16),
                   jax.ShapeDtypeStruct((n, CONV_WIDTH - 1, CONV_DIM), F32)],
        compiler_params=_cp(("parallel", "arbitrary")),
        name="conv_prompt",
    )(cg, cg, cg, conv_w)


def _conv_dec_kernel(u_ref, b_ref, c_ref, buf_ref, w_ref, o_ref, st_ref):
    v = c_ref[...] * u_ref[...]
    b0 = buf_ref[:, 0, :]
    b1 = buf_ref[:, 1, :]
    w = w_ref[...]
    y = w[0:1] * b0 + w[1:2] * b1 + w[2:3] * v
    o_ref[...] = (b_ref[...] * y).astype(o_ref.dtype)
    st_ref[:, 0, :] = b1
    st_ref[:, 1, :] = v


def conv_decode(cg, buf, conv_w):
    n = cg.shape[0]
    blk = lambda j: pl.BlockSpec((n, CONV_DIM), lambda i: (0, j))
    full3 = pl.BlockSpec((n, CONV_WIDTH - 1, CONV_DIM), lambda i: (0, 0, 0))
    return pl.pallas_call(
        _conv_dec_kernel,
        grid=(1,),
        in_specs=[blk(0), blk(1), blk(2), full3, pl.BlockSpec((CONV_WIDTH, CONV_DIM), lambda i: (0, 0))],
        out_specs=[pl.BlockSpec((n, CONV_DIM), lambda i: (0, 0)), full3],
        out_shape=[jax.ShapeDtypeStruct((n, CONV_DIM), BF16),
                   jax.ShapeDtypeStruct((n, CONV_WIDTH - 1, CONV_DIM), F32)],
        compiler_params=_cp(("arbitrary",)),
        name="conv_decode",
    )(cg, cg, cg, buf, conv_w)


def _compress(read_x, w1_ref, w2_ref, pe_ref, nc):
    out = []
    for kv in range(2):
        acc = jnp.zeros((NSA_KV * nc, 2 * HEAD_DIM), F32)
        bias = jnp.zeros((8, 2 * HEAD_DIM), F32)
        for t in range(CMP_STRIDE):
            xt = jnp.concatenate([read_x(t, kv * NSA_KV + g) for g in range(NSA_KV)], axis=0).astype(BF16)
            wt = w1_ref[kv, t]
            acc += jnp.dot(xt, wt, preferred_element_type=F32)
            bias += jnp.dot(pe_ref[kv, t], wt, preferred_element_type=F32)
        pe_bias = bias[0:1, :HEAD_DIM] + bias[1:2, HEAD_DIM:]
        per_group = []
        for g in range(NSA_KV):
            a = acc[g * nc:(g + 1) * nc, :HEAD_DIM]
            b = acc[g * nc:(g + 1) * nc, HEAD_DIM:]
            pre = a + pltpu.roll(b, nc - 1, 0) + pe_bias
            hid = pre * _sigmoid(pre)
            per_group.append(jnp.dot(hid.astype(BF16), w2_ref[kv], preferred_element_type=F32))
        out.append(per_group)
    return out[0], out[1]


def _gate_rows(gates_blk):
    return gates_blk.T


def _stack_heads(q_ref, g, bq):
    return jnp.concatenate(
        [q_ref[:, (g * NSA_REP + r) * HEAD_DIM:(g * NSA_REP + r + 1) * HEAD_DIM] for r in range(NSA_REP)],
        axis=0).astype(BF16)


KV_SLOTS = 2 * NSA_KV


def _kv_rows(kv_ref, k0, n, slot):
    return kv_ref[pl.ds(k0 * KV_SLOTS + slot, n, stride=KV_SLOTS), :]


EXP2_SCALE = ATTN_SCALE * math.log2(math.e)
FLASH_KC = 256


def _flash_scratch(bq):
    nq = NSA_REP * bq
    return [pltpu.VMEM((NSA_KV, nq, HEAD_DIM), BF16), pltpu.VMEM((NSA_KV, 1, nq), F32),
            pltpu.VMEM((NSA_KV, 1, nq), F32), pltpu.VMEM((NSA_KV, HEAD_DIM, nq), F32)]


def _flash_t(st, q_ref, kv_ref, o_ref, lo, hi, kc, bias_fn, bq, gate_t, branch):
    qs_ref, m_ref, l_ref, acc_ref = st
    for g in range(NSA_KV):
        qs_ref[g] = _stack_heads(q_ref, g, bq)
    m_ref[...] = jnp.full(m_ref.shape, NEG_INF, F32)
    l_ref[...] = jnp.zeros(l_ref.shape, F32)
    acc_ref[...] = jnp.zeros(acc_ref.shape, F32)

    def body(c, _):
        k0 = pl.multiple_of(c * kc, kc)
        for g in range(NSA_KV):
            kb = _kv_rows(kv_ref, k0, kc, g).astype(BF16)
            vb = _kv_rows(kv_ref, k0, kc, NSA_KV + g).astype(BF16)
            bias = bias_fn(k0, g)
            s = lax.dot_general(kb, qs_ref[g], (((1,), (1,)), ((), ())), preferred_element_type=F32)
            s = s + jnp.concatenate([bias] * NSA_REP, axis=1)
            m = m_ref[g]
            m_new = jnp.maximum(m, jnp.max(s, axis=0, keepdims=True))
            alpha = jnp.exp2((m - m_new) * EXP2_SCALE)
            p = jnp.exp2((s - m_new) * EXP2_SCALE)
            m_ref[g] = m_new
            l_ref[g] = alpha * l_ref[g] + jnp.sum(p, axis=0, keepdims=True)
            pv = lax.dot_general(vb, p.astype(BF16), (((0,), (0,)), ((), ())), preferred_element_type=F32)
            acc_ref[g] = alpha * acc_ref[g] + pv
        return 0

    lax.fori_loop(lo, hi, body, 0)
    for g in range(NSA_KV):
        ot = jnp.where(m_ref[g] > 0.5 * NEG_INF, acc_ref[g] / jnp.maximum(l_ref[g], 1e-30), 0.0)
        _store_heads(o_ref, [ot[:, r * bq:(r + 1) * bq] for r in range(NSA_REP)], g, gate_t, branch)


def _store_heads(o_ref, heads, g, gate_t, branch):
    for r, oh in enumerate(heads):
        h = g * NSA_REP + r
        if gate_t is not None:
            oh = oh * _sigmoid(gate_t[h * 3 + branch:h * 3 + branch + 1, :])
        o_ref[:, h * HEAD_DIM:(h + 1) * HEAD_DIM] = oh.T


def _cmp_prompt_kernel(x_ref, q_ref, gates_ref, w1_ref, w2_ref, pe_ref, ov_ref, o_ref, ps_ref, kc_ref, vc_ref,
                       *, nc, bq):
    qi = pl.program_id(1)

    @pl.when(qi == 0)
    def _():
        def read_x(t, kg):
            return x_ref[pl.ds(t * KV_SLOTS + kg, nc, stride=CMP_STRIDE * KV_SLOTS), :]
        k_c, v_c = _compress(read_x, w1_ref, w2_ref, pe_ref, nc)
        for g in range(NSA_KV):
            kc_ref[g] = k_c[g].astype(BF16)
            vc_ref[g] = v_c[g].astype(BF16)

    q0 = qi * bq
    nq = NSA_REP * bq
    gate_t = _gate_rows(gates_ref[...])
    pos = q0 + lax.broadcasted_iota(jnp.int32, (nc, bq), 1)
    blk_end = lax.broadcasted_iota(jnp.int32, (nc, bq), 0) * CMP_STRIDE + (CMP_LEN - 1)
    mk1 = blk_end <= pos
    mk = jnp.concatenate([mk1] * NSA_REP, axis=1)
    for g in range(NSA_KV):
        qs = _stack_heads(q_ref, g, bq)
        s = lax.dot_general(kc_ref[g], qs, (((1,), (1,)), ((), ())), preferred_element_type=F32) * ATTN_SCALE
        s = jnp.where(mk, s, NEG_INF)
        m = jnp.max(s, axis=0, keepdims=True)
        e = jnp.where(mk, jnp.exp(s - m), 0.0)
        p = e / jnp.maximum(jnp.sum(e, axis=0, keepdims=True), 1e-30)
        ot = lax.dot_general(vc_ref[g], p.astype(BF16), (((0,), (0,)), ((), ())), preferred_element_type=F32)
        _store_heads(o_ref, [ot[:, r * bq:(r + 1) * bq] for r in range(NSA_REP)], g, gate_t, 0)
        psum = p[:, 0:bq]
        for r in range(1, NSA_REP):
            psum = psum + p[:, r * bq:(r + 1) * bq]
        ps_ref[g] = jnp.dot(ov_ref[...], psum, precision=lax.Precision.HIGHEST, preferred_element_type=F32)


def cmp_prompt(cmp_x, q, idxm, w1cat, w2, pe8, ov_t, n, s):
    nc = s // CMP_STRIDE
    bq = 256
    nq = s // bq
    n_slc = ov_t.shape[0]
    t = n * s
    full = lambda a: pl.BlockSpec(a.shape, lambda b, i: (0,) * a.ndim)
    return pl.pallas_call(
        functools.partial(_cmp_prompt_kernel, nc=nc, bq=bq),
        grid=(n, nq),
        in_specs=[pl.BlockSpec((None, s * KV_SLOTS, HEAD_DIM), lambda b, i: (0, b, 0)),
                  pl.BlockSpec((bq, BRANCH_W), lambda b, i: (b * nq + i, 0)),
                  pl.BlockSpec((bq, LANE), lambda b, i: (b * nq + i, 5)),
                  full(w1cat), full(w2), full(pe8), full(ov_t)],
        out_specs=[pl.BlockSpec((bq, BRANCH_W), lambda b, i: (b * nq + i, 0)),
                   pl.BlockSpec((NSA_KV, n_slc, bq), lambda b, i: (0, 0, b * nq + i))],
        out_shape=[jax.ShapeDtypeStruct((t, BRANCH_W), F32), jax.ShapeDtypeStruct((NSA_KV, n_slc, t), F32)],
        scratch_shapes=[pltpu.VMEM((NSA_KV, nc, HEAD_DIM), BF16), pltpu.VMEM((NSA_KV, nc, HEAD_DIM), BF16)],
        compiler_params=_cp(("arbitrary", "arbitrary")),
        name="nsa_cmp_prompt",
    )(cmp_x, q, idxm, w1cat, w2, pe8, ov_t)


def _topk_rank_rows(sc, n_rows, k):
    j = lax.broadcasted_iota(jnp.int32, sc.shape, 0)
    rank = jnp.zeros(sc.shape, jnp.int32)
    for i in range(n_rows):
        si = sc[i:i + 1, :]
        beats = jnp.where(si > sc, 1, jnp.where((si == sc) & (j > i), 1, 0))
        rank = rank + beats
    return jnp.where(rank < k, 1.0, 0.0)


def _slc_prompt_kernel(q_ref, kv_ref, ps_ref, gates_ref, o_ref, bias_ref, *st, n_slc, bq, kc):
    qi = pl.program_id(1)
    q0 = qi * bq
    gate_t = _gate_rows(gates_ref[...])
    blk = lax.broadcasted_iota(jnp.int32, (n_slc, bq), 0)
    pos = q0 + lax.broadcasted_iota(jnp.int32, (n_slc, bq), 1)
    cur = pos // SEL_BLOCK
    forced = (blk == 0) | (blk == cur) | (blk == cur - 1)
    visible = blk * SEL_BLOCK <= pos
    qpos = q0 + lax.broadcasted_iota(jnp.int32, (SEL_BLOCK, bq), 1)
    krow = lax.broadcasted_iota(jnp.int32, (SEL_BLOCK, bq), 0)
    for g in range(NSA_KV):
        sc = jnp.where(forced, FORCE_SCORE, jnp.where(visible, ps_ref[g], NEG_INF))
        sel = _topk_rank_rows(sc, n_slc, min(N_SEL, n_slc))
        for j in range(n_slc):
            keep = (jnp.broadcast_to(sel[j:j + 1, :], (SEL_BLOCK, bq)) > 0.5) & (j * SEL_BLOCK + krow <= qpos)
            bias_ref[g, j * SEL_BLOCK:(j + 1) * SEL_BLOCK, :] = jnp.where(keep, 0.0, NEG_INF)

    _flash_t(st, q_ref, kv_ref, o_ref, 0, (q0 + bq + kc - 1) // kc, kc,
             lambda k0, g: bias_ref[g, pl.ds(k0, kc), :], bq, gate_t, 1)


def slc_prompt(q, kv4, p_slc, idxm, n, s):
    bq, kc = 256, FLASH_KC
    nq = s // bq
    n_slc = p_slc.shape[1]
    t = n * s
    return pl.pallas_call(
        functools.partial(_slc_prompt_kernel, n_slc=n_slc, bq=bq, kc=kc),
        grid=(n, nq),
        in_specs=[pl.BlockSpec((bq, BRANCH_W), lambda b, i: (b * nq + i, 0)),
                  pl.BlockSpec((None, s * KV_SLOTS, HEAD_DIM), lambda b, i: (1, b, 0)),
                  pl.BlockSpec((NSA_KV, n_slc, bq), lambda b, i: (0, 0, b * nq + i)),
                  pl.BlockSpec((bq, LANE), lambda b, i: (b * nq + i, 5))],
        out_specs=pl.BlockSpec((bq, BRANCH_W), lambda b, i: (b * nq + i, 0)),
        out_shape=jax.ShapeDtypeStruct((t, BRANCH_W), F32),
        scratch_shapes=[pltpu.VMEM((NSA_KV, s, bq), F32)] + _flash_scratch(bq),
        compiler_params=_cp(("parallel", "arbitrary")),
        name="nsa_slc_prompt",
    )(q, kv4, p_slc, idxm)


def _win_prompt_kernel(q_ref, kv_ref, gates_ref, o_ref, *st, bq, kc):
    qi = pl.program_id(1)
    q0 = qi * bq
    gate_t = _gate_rows(gates_ref[...])
    qpos = q0 + lax.broadcasted_iota(jnp.int32, (kc, bq), 1)
    krow = lax.broadcasted_iota(jnp.int32, (kc, bq), 0)

    def bias_fn(k0, g):
        rel = qpos - (k0 + krow)
        return jnp.where((rel >= 0) & (rel <= WINDOW), 0.0, NEG_INF)

    lo = jnp.maximum(q0 - WINDOW, 0) // kc
    hi = (q0 + bq + kc - 1) // kc
    _flash_t(st, q_ref, kv_ref, o_ref, lo, hi, kc, bias_fn, bq, gate_t, 2)


def win_prompt(q, kv4, idxm, n, s):
    bq, kc = 256, FLASH_KC
    nq = s // bq
    t = n * s
    return pl.pallas_call(
        functools.partial(_win_prompt_kernel, bq=bq, kc=kc),
        grid=(n, nq),
        in_specs=[pl.BlockSpec((bq, BRANCH_W), lambda b, i: (b * nq + i, 0)),
                  pl.BlockSpec((None, s * KV_SLOTS, HEAD_DIM), lambda b, i: (2, b, 0)),
                  pl.BlockSpec((bq, LANE), lambda b, i: (b * nq + i, 5))],
        out_specs=pl.BlockSpec((bq, BRANCH_W), lambda b, i: (b * nq + i, 0)),
        out_shape=jax.ShapeDtypeStruct((t, BRANCH_W), F32),
        scratch_shapes=_flash_scratch(bq),
        compiler_params=_cp(("parallel", "arbitrary")),
        name="nsa_win_prompt",
    )(q, kv4, idxm)


def _order_key(x):
    b = pltpu.bitcast(x + 0.0, jnp.int32)
    return b ^ ((b >> 31) & jnp.int32(0x7FFFFFFF))


def _radix_kth(count_ge, k, shape):
    zero = jnp.zeros(shape, jnp.int32)
    base = jnp.where(count_ge(zero) >= k, zero, jnp.full(shape, INT_MIN, jnp.int32))

    def body(i, base):
        cand = base | jnp.left_shift(jnp.int32(1), 30 - i)
        return jnp.where(count_ge(cand) >= k, cand, base)

    return lax.fori_loop(0, 31, body, base)


def _tie_cut(count_eq_below, need, shape, n_bits):
    def body(i, m):
        cand = m | jnp.left_shift(jnp.int32(1), n_bits - 1 - i)
        return jnp.where(count_eq_below(cand) <= need, cand, m)

    return lax.fori_loop(0, n_bits, body, jnp.zeros(shape, jnp.int32))


def _dsa_prompt_kernel(q_ref, qi_ref, kidx_ref, kv_ref, gates_ref, o_ref, key_ref, bias_ref, *st,
                       bq, kc, kf, s_len):
    qi = pl.program_id(1)
    q0 = qi * bq
    n_chunks = (q0 + bq + kc - 1) // kc
    gate_t = _gate_rows(gates_ref[...])
    w_row = jnp.concatenate([gate_t[24 + h:25 + h, :] for h in range(IDX_HEADS)], axis=1) * (
        IDX_HEADS ** -0.5 * IDX_SCALE)
    qis = jnp.concatenate([qi_ref[:, h * IDX_DIM:(h + 1) * IDX_DIM] for h in range(IDX_HEADS)], axis=0).astype(BF16)
    qpos = q0 + lax.broadcasted_iota(jnp.int32, (kc, bq), 1)
    krow = lax.broadcasted_iota(jnp.int32, (kc, bq), 0)

    def score_body(c, _):
        k0 = pl.multiple_of(c * kc, kc)
        kb = kidx_ref[pl.ds(k0, kc), 0:IDX_DIM].astype(BF16)
        lg = lax.dot_general(kb, qis, (((1,), (1,)), ((), ())), preferred_element_type=F32)
        wl = jnp.maximum(lg, 0.0) * w_row
        sc = wl[:, 0:bq]
        for h in range(1, IDX_HEADS):
            sc = sc + wl[:, h * bq:(h + 1) * bq]
        sc = jnp.where(k0 + krow <= qpos, sc, NEG_INF)
        key_ref[pl.ds(k0, kc), :] = _order_key(sc)
        return 0

    lax.fori_loop(0, n_chunks, score_body, 0)

    def count(pred):
        def body(c, acc):
            k0 = pl.multiple_of(c * kc, kc)
            hit = jnp.where(pred(key_ref[pl.ds(k0, kc), :], k0 + krow), 1, 0)
            return acc + jnp.sum(hit.reshape(kc // 8, 8, bq), axis=0)
        acc = lax.fori_loop(0, n_chunks, body, jnp.zeros((8, bq), jnp.int32))
        return jnp.sum(acc, axis=0, keepdims=True)

    k_top = min(DSA_TOPK, s_len // 4)
    row1 = (1, bq)

    def select(_):
        thr = _radix_kth(lambda cand: count(lambda key, idx: key >= cand), k_top, row1)
        need = k_top - count(lambda key, idx: key > thr)
        n_eq = count(lambda key, idx: key == thr)
        n_bits = max(1, int(s_len).bit_length())
        cut = lax.cond(
            jnp.any(n_eq != need),
            lambda _: _tie_cut(lambda m: count(lambda key, idx: (key == thr) & (idx < m)), need, row1, n_bits),
            lambda _: jnp.full(row1, s_len, jnp.int32), 0)
        return thr, cut

    thr, cut = lax.cond(q0 + bq > k_top, select,
                        lambda _: (jnp.full(row1, INT_MIN, jnp.int32), jnp.full(row1, s_len, jnp.int32)), 0)

    def mask_body(c, _):
        k0 = pl.multiple_of(c * kc, kc)
        key = key_ref[pl.ds(k0, kc), :]
        idx = k0 + krow
        sel = ((key > thr) | ((key == thr) & (idx < cut))) & (idx <= qpos)
        bias_ref[pl.ds(k0, kc), :] = jnp.where(sel, 0.0, NEG_INF)
        return 0

    lax.fori_loop(0, n_chunks, mask_body, 0)

    _flash_t(st, q_ref, kv_ref, o_ref, 0, (q0 + bq + kf - 1) // kf, kf,
             lambda k0, g: bias_ref[pl.ds(k0, kf), :], bq, None, 0)


def dsa_prompt(q, kv4, idxm, n, s):
    bq = kc = 256
    nq = s // bq
    t = n * s
    return pl.pallas_call(
        functools.partial(_dsa_prompt_kernel, bq=bq, kc=kc, kf=FLASH_KC, s_len=s),
        grid=(n, nq),
        in_specs=[pl.BlockSpec((bq, BRANCH_W), lambda b, i: (b * nq + i, 1)),
                  pl.BlockSpec((bq, IDX_HEADS * IDX_DIM), lambda b, i: (b * nq + i, 0)),
                  pl.BlockSpec((s, LANE), lambda b, i: (b, 4)),
                  pl.BlockSpec((None, s * KV_SLOTS, HEAD_DIM), lambda b, i: (3, b, 0)),
                  pl.BlockSpec((bq, LANE), lambda b, i: (b * nq + i, 5))],
        out_specs=pl.BlockSpec((bq, BRANCH_W), lambda b, i: (b * nq + i, 0)),
        out_shape=jax.ShapeDtypeStruct((t, BRANCH_W), F32),
        scratch_shapes=[pltpu.VMEM((s, bq), jnp.int32), pltpu.VMEM((s, bq), F32)] + _flash_scratch(bq),
        compiler_params=_cp(("parallel", "arbitrary")),
        name="dsa_prompt",
    )(q, idxm, idxm, kv4, idxm)


def _page_copies(pt_ref, cache_ref, layer, buf_ref, sem_ref, seq, slot, n_pages, rows, on_lanes):
    def each(fn):
        def body(p, _):
            win = pl.ds(p * rows, rows)
            dst = buf_ref.at[slot, :, win] if on_lanes else buf_ref.at[slot, win]
            fn(pltpu.make_async_copy(cache_ref.at[layer, pt_ref[seq, p]], dst, sem_ref.at[slot]))
            return 0
        lax.fori_loop(0, n_pages, body, 0)
    return each


def _gather_step(pt_ref, cache_ref, layer, buf_ref, sem_ref, n_pages, rows, on_lanes=False):
    b = pl.program_id(0)
    nb = pl.num_programs(0)
    slot = b % 2
    copies = functools.partial(_page_copies, pt_ref, cache_ref, layer, buf_ref, sem_ref,
                               n_pages=n_pages, rows=rows, on_lanes=on_lanes)

    @pl.when(b == 0)
    def _():
        copies(seq=0, slot=0)(lambda cp: cp.start())

    @pl.when(b + 1 < nb)
    def _():
        copies(seq=b + 1, slot=1 - slot)(lambda cp: cp.start())

    copies(seq=b, slot=slot)(lambda cp: cp.wait())
    return slot


def _head_column(row, offset, stride):
    lane = lax.broadcasted_iota(jnp.int32, (8, LANE), 1)
    h = lax.broadcasted_iota(jnp.int32, (8, LANE), 0)
    return jnp.sum(jnp.where(lane == offset + stride * h, jnp.broadcast_to(row, (8, LANE)), 0.0), axis=1, keepdims=True)


def _q8(q_ref, width):
    return jnp.concatenate([q_ref[0:1, h * width:(h + 1) * width] for h in range(8)], axis=0)


def _row_spec(width, blk, n_extra):
    if n_extra:
        return pl.BlockSpec((None, 1, width), lambda b, pt: (b, 0, blk))
    return pl.BlockSpec((None, 1, width), lambda b: (b, 0, blk))


def _cmp_dec_kernel(pt_ref, cache_ref, q_ref, gates_ref, w1_ref, w2_ref, pe_ref, ov_ref, o_ref, ps_ref,
                    buf_ref, sem_ref, *, layer, n_pages, p0):
    nc = n_pages * PAGE_SIZE // CMP_STRIDE
    slot = _gather_step(pt_ref, cache_ref, layer, buf_ref, sem_ref, n_pages, PAGE_SIZE * KV_SLOTS)

    def read_x(t, kg):
        return buf_ref[slot, pl.ds(t * KV_SLOTS + kg, nc, stride=CMP_STRIDE * KV_SLOTS), :]

    k_c, v_c = _compress(read_x, w1_ref, w2_ref, pe_ref, nc)
    q8 = _q8(q_ref, HEAD_DIM).astype(BF16)
    head = lax.broadcasted_iota(jnp.int32, (8, 1), 0)
    blk_end = lax.broadcasted_iota(jnp.int32, (8, nc), 1) * CMP_STRIDE + (CMP_LEN - 1)
    mk = blk_end <= p0
    s = jnp.zeros((8, nc), F32)
    for g in range(NSA_KV):
        sg = lax.dot_general(q8, k_c[g].astype(BF16), (((1,), (1,)), ((), ())), preferred_element_type=F32)
        s = jnp.where(head // NSA_REP == g, sg, s)
    s = jnp.where(mk, s * ATTN_SCALE, NEG_INF)
    m = jnp.max(s, axis=1, keepdims=True)
    e = jnp.where(mk, jnp.exp(s - m), 0.0)
    p = e / jnp.maximum(jnp.sum(e, axis=1, keepdims=True), 1e-30)
    o = jnp.zeros((8, HEAD_DIM), F32)
    for g in range(NSA_KV):
        og = jnp.dot(p.astype(BF16), v_c[g].astype(BF16), preferred_element_type=F32)
        o = jnp.where(head // NSA_REP == g, og, o)
        psum = jnp.sum(jnp.where(head // NSA_REP == g, p, 0.0), axis=0, keepdims=True)
        ps8 = jnp.dot(jnp.broadcast_to(psum, (8, nc)), ov_ref[...], precision=lax.Precision.HIGHEST,
                      preferred_element_type=F32)
        ps_ref[g:g + 1, :] = ps8[0:1, :]
    gate = _sigmoid(_head_column(gates_ref[...], 0, 3))
    o_ref[...] = o * gate


def cmp_decode(page_table, cache_x, layer, q, idxm, w1cat, w2, pe8, ov, p0):
    n, n_pages = page_table.shape
    n_slc_pad = ov.shape[1]
    full = lambda a: pl.BlockSpec(a.shape, lambda b, pt: (0,) * a.ndim)
    gs = pltpu.PrefetchScalarGridSpec(
        num_scalar_prefetch=1,
        grid=(n,),
        in_specs=[pl.BlockSpec(memory_space=pl.ANY), _row_spec(BRANCH_W, 0, 1), _row_spec(LANE, 5, 1),
                  full(w1cat), full(w2), full(pe8), full(ov)],
        out_specs=[pl.BlockSpec((None, NSA_HEADS, HEAD_DIM), lambda b, pt: (b, 0, 0)),
                   pl.BlockSpec((None, NSA_KV, n_slc_pad), lambda b, pt: (b, 0, 0))],
        scratch_shapes=[pltpu.VMEM((2, n_pages * PAGE_SIZE * KV_SLOTS, HEAD_DIM), F32),
                        pltpu.SemaphoreType.DMA((2,))],
    )
    return pl.pallas_call(
        functools.partial(_cmp_dec_kernel, layer=layer, n_pages=n_pages, p0=p0),
        grid_spec=gs,
        out_shape=[jax.ShapeDtypeStruct((n, NSA_HEADS, HEAD_DIM), F32),
                   jax.ShapeDtypeStruct((n, NSA_KV, n_slc_pad), F32)],
        compiler_params=_cp(("arbitrary",)),
        name="nsa_cmp_decode",
    )(page_table, cache_x, q, idxm, w1cat, w2, pe8, ov)


def _slc_mask_kernel(ps_ref, e_ref, o_ref, *, n_slc, p0):
    sc = ps_ref[...]
    j = lax.broadcasted_iota(jnp.int32, sc.shape, 1)
    cur = p0 // SEL_BLOCK
    forced = (j == 0) | (j == cur) | (j == cur - 1)
    visible = j * SEL_BLOCK <= p0
    sc = jnp.where(j >= n_slc, BELOW_ALL, jnp.where(forced, FORCE_SCORE, jnp.where(visible, sc, NEG_INF)))
    rank = jnp.zeros(sc.shape, jnp.int32)
    for i in range(n_slc):
        si = sc[:, i:i + 1]
        rank = rank + jnp.where(si > sc, 1, jnp.where((si == sc) & (j > i), 1, 0))
    sel = jnp.where((rank < min(N_SEL, n_slc)) & (j < n_slc), 1.0, 0.0).astype(BF16)
    o_ref[...] = jnp.dot(sel, e_ref[...], preferred_element_type=F32)


def slc_mask_decode(p_slc, expand, n_slc, p0):
    n, g, w = p_slc.shape
    l_pad = expand.shape[1]
    out = pl.pallas_call(
        functools.partial(_slc_mask_kernel, n_slc=n_slc, p0=p0),
        grid=(1,),
        in_specs=[pl.BlockSpec((n * g, w), lambda i: (0, 0)), pl.BlockSpec(expand.shape, lambda i: (0, 0))],
        out_specs=pl.BlockSpec((n * g, l_pad), lambda i: (0, 0)),
        out_shape=jax.ShapeDtypeStruct((n * g, l_pad), F32),
        compiler_params=_cp(("arbitrary",)),
        name="nsa_slc_mask_decode",
    )(p_slc.reshape(n * g, w), expand)
    return out.reshape(n, g, l_pad)


def _idx_score_kernel(pt_ref, cache_ref, qi_ref, knew_ref, gates_ref, o_ref, buf_ref, sem_ref,
                      *, layer, n_pages, p0):
    past = n_pages * PAGE_SIZE
    l_pad = past + LANE
    slot = _gather_step(pt_ref, cache_ref, layer, buf_ref, sem_ref, n_pages, PAGE_SIZE, on_lanes=True)
    d = lax.broadcasted_iota(jnp.int32, (IDX_DIM, LANE), 0)
    lane = lax.broadcasted_iota(jnp.int32, (IDX_DIM, LANE), 1)
    k_row = jnp.broadcast_to(knew_ref[...], (IDX_DIM, LANE))
    k_col = jnp.sum(jnp.where(lane == d, k_row, 0.0), axis=1, keepdims=True)
    buf_ref[slot, :, past:l_pad] = jnp.where(lane == 0, k_col, 0.0)
    q8 = _q8(qi_ref, IDX_DIM).astype(BF16)
    lg = jnp.dot(q8, buf_ref[slot].astype(BF16), preferred_element_type=F32) * IDX_SCALE
    w_col = _head_column(gates_ref[...], 24, 1) * (IDX_HEADS ** -0.5)
    sc = jnp.sum(jnp.maximum(lg, 0.0) * w_col, axis=0, keepdims=True)
    key = lax.broadcasted_iota(jnp.int32, (1, l_pad), 1)
    o_ref[...] = jnp.where(key <= p0, sc, BELOW_ALL)


def idx_score_decode(page_table, cache_idx, layer, idxm, p0):
    n, n_pages = page_table.shape
    l_pad = n_pages * PAGE_SIZE + LANE
    gs = pltpu.PrefetchScalarGridSpec(
        num_scalar_prefetch=1,
        grid=(n,),
        in_specs=[pl.BlockSpec(memory_space=pl.ANY), _row_spec(IDX_HEADS * IDX_DIM, 0, 1),
                  _row_spec(LANE, 4, 1), _row_spec(LANE, 5, 1)],
        out_specs=pl.BlockSpec((None, 1, l_pad), lambda b, pt: (b, 0, 0)),
        scratch_shapes=[pltpu.VMEM((2, IDX_DIM, l_pad), F32), pltpu.SemaphoreType.DMA((2,))],
    )
    return pl.pallas_call(
        functools.partial(_idx_score_kernel, layer=layer, n_pages=n_pages, p0=p0),
        grid_spec=gs,
        out_shape=jax.ShapeDtypeStruct((n, 1, l_pad), F32),
        compiler_params=_cp(("arbitrary",)),
        name="dsa_idx_score_decode",
    )(page_table, cache_idx, idxm, idxm, idxm)


def _dsa_mask_kernel(sc_ref, o_ref, *, k_top, p0):
    key = _order_key(sc_ref[...])
    n, l_pad = key.shape
    idx = lax.broadcasted_iota(jnp.int32, key.shape, 1)
    col = (n, 1)

    def count(pred):
        return jnp.sum(jnp.where(pred, 1, 0), axis=1, keepdims=True)

    thr = _radix_kth(lambda cand: count(key >= cand), k_top, col)
    need = k_top - count(key > thr)
    cut = _tie_cut(lambda m: count((key == thr) & (idx < m)), need, col, max(1, int(l_pad).bit_length()))
    sel = ((key > thr) | ((key == thr) & (idx < cut))) & (idx <= p0)
    mask = jnp.where(sel, 1.0, 0.0)
    for g in range(DSA_KV):
        o_ref[:, g, :] = mask


def dsa_mask_decode(score, k_top, p0):
    n, _, l_pad = score.shape
    return pl.pallas_call(
        functools.partial(_dsa_mask_kernel, k_top=k_top, p0=p0),
        grid=(1,),
        in_specs=[pl.BlockSpec((n, l_pad), lambda i: (0, 0))],
        out_specs=pl.BlockSpec((n, DSA_KV, l_pad), lambda i: (0, 0, 0)),
        out_shape=jax.ShapeDtypeStruct((n, DSA_KV, l_pad), F32),
        compiler_params=_cp(("arbitrary",)),
        name="dsa_mask_decode",
    )(score.reshape(n, l_pad))


def _attend_rows(q8, kv_rows, mask_ref, n_keys):
    head = lax.broadcasted_iota(jnp.int32, (8, 1), 0)
    o = jnp.zeros((8, HEAD_DIM), F32)
    for g in range(NSA_KV):
        kb = _kv_rows(kv_rows, 0, n_keys, g).astype(BF16)
        vb = _kv_rows(kv_rows, 0, n_keys, NSA_KV + g).astype(BF16)
        s = lax.dot_general(q8, kb, (((1,), (1,)), ((), ())), preferred_element_type=F32) * ATTN_SCALE
        mk = mask_ref[g:g + 1, :] > 0.5
        s = jnp.where(mk, s, NEG_INF)
        m = jnp.max(s, axis=1, keepdims=True)
        e = jnp.where(mk, jnp.exp(s - m), 0.0)
        p = e / jnp.maximum(jnp.sum(e, axis=1, keepdims=True), 1e-30)
        og = jnp.dot(p.astype(BF16), vb, preferred_element_type=F32)
        o = jnp.where(head // NSA_REP == g, og, o)
    return o


def _attn_paged_kernel(pt_ref, cache_ref, q_ref, kvnew_ref, mask_ref, gates_ref, o_ref, buf_ref, sem_ref,
                       *, layer, n_pages, branch):
    past = n_pages * PAGE_SIZE
    l_pad = past + LANE
    slot = _gather_step(pt_ref, cache_ref, layer, buf_ref, sem_ref, n_pages, PAGE_SIZE * KV_SLOTS)
    buf_ref[slot, past * KV_SLOTS:l_pad * KV_SLOTS, :] = jnp.zeros((LANE * KV_SLOTS, HEAD_DIM), F32)
    buf_ref[slot, past * KV_SLOTS:(past + 1) * KV_SLOTS, :] = kvnew_ref[...]
    o = _attend_rows(_q8(q_ref, HEAD_DIM).astype(BF16), buf_ref.at[slot], mask_ref, l_pad)
    if branch is not None:
        o = o * _sigmoid(_head_column(gates_ref[...], branch, 3))
    o_ref[...] = o


def attn_paged_decode(page_table, cache, layer, q, q_blk, kvnew, mask, idxm, branch):
    n, n_pages = page_table.shape
    l_pad = n_pages * PAGE_SIZE + LANE
    gs = pltpu.PrefetchScalarGridSpec(
        num_scalar_prefetch=1,
        grid=(n,),
        in_specs=[pl.BlockSpec(memory_space=pl.ANY), _row_spec(BRANCH_W, q_blk, 1),
                  pl.BlockSpec((None, KV_SLOTS, HEAD_DIM), lambda b, pt: (b, 0, 0)),
                  pl.BlockSpec((None, NSA_KV, l_pad), lambda b, pt: (b, 0, 0)), _row_spec(LANE, 5, 1)],
        out_specs=pl.BlockSpec((None, NSA_HEADS, HEAD_DIM), lambda b, pt: (b, 0, 0)),
        scratch_shapes=[pltpu.VMEM((2, l_pad * KV_SLOTS, HEAD_DIM), F32), pltpu.SemaphoreType.DMA((2,))],
    )
    return pl.pallas_call(
        functools.partial(_attn_paged_kernel, layer=layer, n_pages=n_pages, branch=branch),
        grid_spec=gs,
        out_shape=jax.ShapeDtypeStruct((n, NSA_HEADS, HEAD_DIM), F32),
        compiler_params=_cp(("arbitrary",)),
        name="attn_paged_decode",
    )(page_table, cache, q, kvnew, mask, idxm)


def _attn_win_kernel(st_ref, q_ref, kvnew_ref, gates_ref, o_ref, buf_ref, mask_ref, *, wb):
    l_pad = wb + LANE
    buf_ref[0:wb * KV_SLOTS, :] = st_ref[...]
    buf_ref[wb * KV_SLOTS:l_pad * KV_SLOTS, :] = jnp.zeros((LANE * KV_SLOTS, HEAD_DIM), F32)
    buf_ref[wb * KV_SLOTS:(wb + 1) * KV_SLOTS, :] = kvnew_ref[...]
    key = lax.broadcasted_iota(jnp.int32, (NSA_KV, l_pad), 1)
    mask_ref[...] = jnp.where(key <= wb, 1.0, 0.0)
    o = _attend_rows(_q8(q_ref, HEAD_DIM).astype(BF16), buf_ref, mask_ref, l_pad)
    o_ref[...] = o * _sigmoid(_head_column(gates_ref[...], 2, 3))


def attn_win_decode(state, layer, q, kvnew, idxm):
    n, wb = state.shape[1], state.shape[2] // KV_SLOTS
    l_pad = wb + LANE
    return pl.pallas_call(
        functools.partial(_attn_win_kernel, wb=wb),
        grid=(n,),
        in_specs=[pl.BlockSpec((None, None, wb * KV_SLOTS, HEAD_DIM), lambda b: (layer, b, 0, 0)),
                  _row_spec(BRANCH_W, 0, 0), pl.BlockSpec((None, KV_SLOTS, HEAD_DIM), lambda b: (b, 0, 0)),
                  _row_spec(LANE, 5, 0)],
        out_specs=pl.BlockSpec((None, NSA_HEADS, HEAD_DIM), lambda b: (b, 0, 0)),
        out_shape=jax.ShapeDtypeStruct((n, NSA_HEADS, HEAD_DIM), F32),
        scratch_shapes=[pltpu.VMEM((l_pad * KV_SLOTS, HEAD_DIM), F32), pltpu.VMEM((NSA_KV, l_pad), F32)],
        compiler_params=_cp(("parallel",)),
        name="attn_win_decode",
    )(state, q, kvnew, idxm)


def _rope_tables(pos, head_dim):
    d_rot = head_dim // ROPE_FRACTION
    half = d_rot // 2
    inv_freq = jnp.exp(jnp.arange(half, dtype=F32) * (-2.0 * math.log(ROPE_THETA) / d_rot))
    ang = pos.astype(F32)[:, None] * inv_freq[None, :]
    cos, sin = jnp.cos(ang), jnp.sin(ang)
    lane = np.arange(LANE) % head_dim
    j = lane % half
    first = jnp.asarray(lane < half)[None, :]
    second = jnp.asarray((lane >= half) & (lane < d_rot))[None, :]
    c = jnp.where(first | second, cos[:, j], 1.0)
    s1 = jnp.where(first, -sin[:, j], 0.0)
    s2 = jnp.where(second, sin[:, j], 0.0)
    return (c, s1, s2), half


def _split_w_in(w_in):
    sizes = (NSA_HEADS * HEAD_DIM, KV_W, KV_W, KV_W, 3 * NSA_HEADS, DSA_HEADS * HEAD_DIM, KV_W,
             IDX_HEADS * IDX_DIM, IDX_DIM, IDX_HEADS, CONV_DIM, CONV_DIM, CONV_DIM, N_BRANCH * D_MODEL)
    offs = np.concatenate([[0], np.cumsum(sizes)])
    col = lambda i: w_in[:, int(offs[i]):int(offs[i + 1])]
    (q_a, cmp_kv, slc_kv, win_kv, gate_a, q_b, dsa_kv, q_i, k_i, w_i, cu, cb, cc, gm) = [col(i) for i in range(14)]
    d = w_in.shape[0]
    zeros = lambda n: jnp.zeros((d, n), w_in.dtype)
    w_q = jnp.concatenate([q_a, q_b], axis=1).astype(BF16)
    w_kv = jnp.concatenate([cmp_kv, slc_kv, win_kv, dsa_kv], axis=1).astype(BF16)
    w_idx = jnp.concatenate([q_i, k_i, zeros(LANE - IDX_DIM), gate_a, w_i, zeros(LANE - 3 * NSA_HEADS - IDX_HEADS)],
                            axis=1).astype(BF16)
    w_cg = jnp.concatenate([cu, cb, cc, gm], axis=1).astype(BF16)
    return w_q, w_kv, w_idx, w_cg


def _cmp_weights(w1, w2, pe):
    half = CMP_STRIDE * HEAD_DIM
    wa = w1[:, :half].reshape(2, CMP_STRIDE, HEAD_DIM, HEAD_DIM)
    wb = w1[:, half:].reshape(2, CMP_STRIDE, HEAD_DIM, HEAD_DIM)
    w1cat = jnp.concatenate([wa, wb], axis=-1).astype(BF16)
    pe8 = jnp.zeros((2, CMP_STRIDE, 8, HEAD_DIM), F32)
    pe8 = pe8.at[:, :, 0, :].set(pe[:, :CMP_STRIDE]).at[:, :, 1, :].set(pe[:, CMP_STRIDE:])
    return w1cat, w2.astype(BF16), pe8.astype(BF16)


def _overlap(n_cmp_rows, n_slc, seq_len):
    n_cmp = seq_len // CMP_STRIDE - 1
    c = np.arange(n_cmp_rows)
    c_start = c * CMP_STRIDE
    s_start = np.arange(n_slc) * SEL_BLOCK
    ov = ((c_start[:, None] < s_start[None, :] + SEL_BLOCK) & (c_start[:, None] + CMP_LEN > s_start[None, :])
          & (c[:, None] < n_cmp))
    return ov.astype(np.float32)


def _project(xn, wts, tabs128, half128, tabs64, half64):
    w_q, w_kv, w_idx, w_cg = wts
    q = proj_rope(xn, w_q, tabs128, bn=1024, half=half128, rope_blocks=(True,) * 8, stacked=False)
    kv4 = proj_rope(xn, w_kv, tabs128, bn=KV_W, half=half128, rope_blocks=(True, True, False, False), stacked=True)
    idxm = proj_rope(xn, w_idx, tabs64, bn=w_idx.shape[1], half=half64,
                     rope_blocks=(True,) * 5 + (False,), stacked=False)
    cg = matmul(xn, w_cg, bn=1024)
    return q, kv4, idxm, cg


def _finish_layer(x, branches, oc, cg, lw, g_next):
    a1, a2, a3, ob = branches
    merged = merge_branches(a1, a2, a3, ob, oc, lw["w_branch"], cg)
    x_mid, hn = outproj_residual(merged, lw["w_out"], x, lw["g_mix_post"], lw["g_ffn_pre"])
    act = ffn_gate_up(hn, lw["w_gu"])
    return ffn_down_residual(act, lw["w_down"], x_mid, lw["g_ffn_post"], g_next)


def _prompt_layer(x, xn, lw, n, s, consts, g_next):
    q, kv4, idxm, cg = _project(xn, lw["w_in"], *consts["rope_p"])
    a1, p_slc = cmp_prompt(kv4, q, idxm, *lw["cmp"], consts["ov_p_t"], n, s)
    a2 = slc_prompt(q, kv4, p_slc, idxm, n, s)
    a3 = win_prompt(q, kv4, idxm, n, s)
    ob = dsa_prompt(q, kv4, idxm, n, s)
    oc, conv_state = conv_prompt(cg, lw["conv_w"], n, s)
    y, xn_next = _finish_layer(x, (a1, a2, a3, ob), oc, cg, lw, g_next)
    kv5 = lambda a: a.reshape(n, s, 2, NSA_KV, HEAD_DIM)
    keep = min(WINDOW, s)
    state = (kv5(kv4[0]), kv5(kv4[1]), kv5(kv4[2])[:, s - keep:], kv5(kv4[3]),
             idxm[:, IDX_HEADS * IDX_DIM:IDX_HEADS * IDX_DIM + IDX_DIM].reshape(n, s, IDX_DIM), conv_state)
    return y, xn_next, state


def _sample_layer(x, xn, lw, layer, caches, page_table, consts, g_next):
    n = x.shape[0]
    p0 = consts["p0"]
    q, kv4, idxm, cg = _project(xn, lw["w_in"], *consts["rope_s"])
    cache_cmp_x, cache_slc, state_win, cache_dsa, cache_idx, state_conv = caches
    q3 = q.reshape(n, 1, -1)
    idx3 = idxm.reshape(n, 1, -1)
    new_row = lambda i: kv4[i].reshape(n, KV_SLOTS, HEAD_DIM)
    o_cmp, p_slc = cmp_decode(page_table, cache_cmp_x, layer, q3, idx3, *lw["cmp"], consts["ov_s"], p0)
    slc_mask = slc_mask_decode(p_slc, consts["expand"], consts["n_slc_s"], p0)
    o_slc = attn_paged_decode(page_table, cache_slc, layer, q3, 0, new_row(1), slc_mask, idx3, 1)
    o_win = attn_win_decode(state_win, layer, q3, new_row(2), idx3)
    score = idx_score_decode(page_table, cache_idx, layer, idx3, p0)
    dsa_mask = dsa_mask_decode(score, min(DSA_TOPK, (p0 + 1) // 4), p0)
    o_dsa = attn_paged_decode(page_table, cache_dsa, layer, q3, 1, new_row(3), dsa_mask, idx3, None)
    oc, conv_state = conv_decode(cg, state_conv[layer], lw["conv_w"])
    flat = lambda a: a.reshape(n, BRANCH_W)
    y, xn_next = _finish_layer(x, (flat(o_cmp), flat(o_slc), flat(o_win), flat(o_dsa)), oc, cg, lw, g_next)
    kv5 = lambda a: a.reshape(n, 1, 2, NSA_KV, HEAD_DIM)
    win_all = jnp.concatenate([consts["state_win"][layer], kv5(kv4[2])], axis=1)
    keep = min(WINDOW, win_all.shape[1])
    state = (kv5(kv4[0]), kv5(kv4[1]), win_all[:, win_all.shape[1] - keep:], kv5(kv4[3]),
             idxm[:, IDX_HEADS * IDX_DIM:IDX_HEADS * IDX_DIM + IDX_DIM].reshape(n, 1, IDX_DIM), conv_state)
    return y, xn_next, state


def kernel(x_prompt, x_sample, cache_nsa_cmp_kv, cache_nsa_slc_kv, state_nsa_win_kv, cache_dsa_kv, cache_dsa_idx_k, state_conv, page_table, norm_mix_pre, norm_mix_post, norm_ffn_pre, norm_ffn_post, w_in, cmp_w1, cmp_w2, cmp_pe, conv_w, w_branch, w_out, ffn_w_gate_up, ffn_w_down):
    n_p, s, d = x_prompt.shape
    n_s = x_sample.shape[0]
    depth = w_in.shape[0]
    n_pages = page_table.shape[1]
    n_pool = cache_nsa_cmp_kv.shape[1]
    p0 = n_pages * PAGE_SIZE
    l_s = p0 + 1
    l_pad = p0 + LANE
    n_slc_s = -(-l_s // SEL_BLOCK)
    n_slc_pad = -(-n_slc_s // LANE) * LANE
    nc_s = p0 // CMP_STRIDE

    tabs128_p, half128 = _rope_tables(jnp.arange(s, dtype=jnp.int32), HEAD_DIM)
    tabs64_p, half64 = _rope_tables(jnp.arange(s, dtype=jnp.int32), IDX_DIM)
    tabs128_s, _ = _rope_tables(jnp.full((n_s,), p0, jnp.int32), HEAD_DIM)
    tabs64_s, _ = _rope_tables(jnp.full((n_s,), p0, jnp.int32), IDX_DIM)
    key_block = np.arange(l_pad) // SEL_BLOCK
    expand = ((key_block[None, :] == np.arange(n_slc_pad)[:, None]) & (np.arange(l_pad)[None, :] <= p0))
    consts = {
        "p0": p0,
        "n_slc_s": n_slc_s,
        "state_win": state_nsa_win_kv,
        "rope_p": (tabs128_p, half128, tabs64_p, half64),
        "rope_s": (tabs128_s, half128, tabs64_s, half64),
        "ov_p_t": jnp.asarray(_overlap(s // CMP_STRIDE, -(-s // SEL_BLOCK), s).T),
        "ov_s": jnp.asarray(_overlap(nc_s, n_slc_pad, l_s) * (np.arange(n_slc_pad) < n_slc_s)[None, :]),
        "expand": jnp.asarray(expand.astype(np.float32)).astype(BF16),
    }
    paged = lambda c: c.reshape(depth, n_pool, PAGE_SIZE * KV_SLOTS, HEAD_DIM)
    caches = (paged(cache_nsa_cmp_kv), paged(cache_nsa_slc_kv),
              state_nsa_win_kv.reshape(depth, n_s, -1, HEAD_DIM), paged(cache_dsa_kv),
              jnp.swapaxes(cache_dsa_idx_k, 2, 3), state_conv)

    x_p = x_prompt.reshape(n_p * s, d)
    x_s = x_sample.reshape(n_s, d)
    xn_p = rmsnorm(x_p, norm_mix_pre[0])
    xn_s = rmsnorm(x_s, norm_mix_pre[0])
    new_p, new_s = [], []
    for l in range(depth):
        lw = {
            "w_in": _split_w_in(w_in[l]),
            "cmp": _cmp_weights(cmp_w1[l], cmp_w2[l], cmp_pe[l]),
            "conv_w": conv_w[l],
            "w_branch": w_branch[l].astype(BF16),
            "w_out": w_out[l].astype(BF16),
            "w_gu": ffn_w_gate_up[l].astype(BF16),
            "w_down": ffn_w_down[l].astype(BF16),
            "g_mix_post": norm_mix_post[l], "g_ffn_pre": norm_ffn_pre[l], "g_ffn_post": norm_ffn_post[l],
        }
        g_next = norm_mix_pre[l + 1] if l + 1 < depth else norm_mix_pre[l]
        x_p, xn_p, st_p = _prompt_layer(x_p, xn_p, lw, n_p, s, consts, g_next)
        x_s, xn_s, st_s = _sample_layer(x_s, xn_s, lw, l, caches, page_table, consts, g_next)
        new_p.append(st_p)
        new_s.append(st_s)
    p_out = [jnp.stack([st[i] for st in new_p]) for i in range(6)]
    s_out = [jnp.stack([st[i] for st in new_s]) for i in range(6)]
    return (x_p.reshape(n_p, s, d), x_s.reshape(n_s, 1, d), *p_out, *s_out)
```

```python
import functools
import math

import numpy as np
import jax
import jax.numpy as jnp
from jax import lax
from jax.experimental import pallas as pl
from jax.experimental.pallas import tpu as pltpu

D_MODEL = 2048
HEAD_DIM = 128
BRANCH_W = D_MODEL // 2
N_BRANCH = 3
NSA_HEADS = BRANCH_W // HEAD_DIM
NSA_KV = 2
NSA_REP = NSA_HEADS // NSA_KV
CMP_STRIDE = 16
CMP_LEN = 2 * CMP_STRIDE
SEL_BLOCK = 64
N_SEL = 16
WINDOW = 512
DSA_HEADS = BRANCH_W // HEAD_DIM
DSA_KV = 2
IDX_HEADS = 8
IDX_DIM = 64
DSA_TOPK = 256
CONV_DIM = BRANCH_W
CONV_WIDTH = 3
FFN_HIDDEN = ((8 * D_MODEL + 3 * 256 - 1) // (3 * 256)) * 256
ROPE_THETA = 500000.0
ROPE_FRACTION = 4
RMS_EPS = 1e-6
ATTN_SCALE = HEAD_DIM ** -0.5
IDX_SCALE = IDX_DIM ** -0.5
NEG_INF = -1e30
FORCE_SCORE = 1e30
BELOW_ALL = -3.0e38
PAGE_SIZE = 128

LANE = 128
KV_W = 2 * NSA_KV * HEAD_DIM
CMP_FEAT = CMP_STRIDE * KV_W
VMEM_LIMIT = 60 * 1024 * 1024
INT_MIN = -(2 ** 31)

F32 = jnp.float32
BF16 = jnp.bfloat16


def _cp(sem, vmem=VMEM_LIMIT):
    return pltpu.CompilerParams(dimension_semantics=sem, vmem_limit_bytes=vmem)


def _pick(n, pref, mult=8):
    if n <= pref:
        return n
    for b in range(pref, 0, -1):
        if n % b == 0 and b % mult == 0:
            return b
    return n


def _sigmoid(x):
    return 1.0 / (1.0 + jnp.exp(-x))


def _rms(x, g):
    return x * lax.rsqrt(jnp.mean(x * x, axis=-1, keepdims=True) + RMS_EPS) * g


def _rmsnorm_kernel(x_ref, g_ref, o_ref):
    o_ref[...] = _rms(x_ref[...], g_ref[...]).astype(o_ref.dtype)


def rmsnorm(x, g, out_dtype=BF16):
    t, d = x.shape
    bm = _pick(t, 512)
    return pl.pallas_call(
        _rmsnorm_kernel,
        grid=(t // bm,),
        in_specs=[pl.BlockSpec((bm, d), lambda i: (i, 0)), pl.BlockSpec((1, d), lambda i: (0, 0))],
        out_specs=pl.BlockSpec((bm, d), lambda i: (i, 0)),
        out_shape=jax.ShapeDtypeStruct((t, d), out_dtype),
        compiler_params=_cp(("parallel",)),
        name="rmsnorm",
    )(x, g.reshape(1, d))


def _mm_rope_kernel(x_ref, w_ref, c_ref, s1_ref, s2_ref, o_ref, *, half, rope_blocks, interleave):
    y = jnp.dot(x_ref[...], w_ref[...], preferred_element_type=F32)
    bm = y.shape[0]
    nh = len(rope_blocks)
    c, s1, s2 = c_ref[...], s1_ref[...], s2_ref[...]
    for h, roped in enumerate(rope_blocks):
        yh = y[:, h * LANE:(h + 1) * LANE]
        if roped:
            yh = yh * c + pltpu.roll(yh, LANE - half, 1) * s1 + pltpu.roll(yh, half, 1) * s2
        if interleave:
            o_ref[pl.ds(h, bm, stride=nh), :] = yh
        else:
            o_ref[:, h * LANE:(h + 1) * LANE] = yh


def proj_rope(xn, w, tabs, *, bn, half, rope_blocks, stacked):
    t, k = xn.shape
    n = w.shape[1]
    tab_rows = tabs[0].shape[0]
    bm = _pick(math.gcd(t, tab_rows), 1024)
    tab_blocks = tab_rows // bm
    kern = functools.partial(_mm_rope_kernel, half=half, rope_blocks=rope_blocks, interleave=stacked)
    tab_spec = pl.BlockSpec((bm, LANE), lambda i, j: (i % tab_blocks, 0))
    if stacked:
        nh = bn // LANE
        out_shape = jax.ShapeDtypeStruct((n // bn, t * nh, LANE), F32)
        out_spec = pl.BlockSpec((None, bm * nh, LANE), lambda i, j: (j, i, 0))
    else:
        out_shape = jax.ShapeDtypeStruct((t, n), F32)
        out_spec = pl.BlockSpec((bm, bn), lambda i, j: (i, j))
    return pl.pallas_call(
        kern,
        grid=(t // bm, n // bn),
        in_specs=[pl.BlockSpec((bm, k), lambda i, j: (i, 0)), pl.BlockSpec((k, bn), lambda i, j: (0, j)),
                  tab_spec, tab_spec, tab_spec],
        out_specs=out_spec,
        out_shape=out_shape,
        compiler_params=_cp(("parallel", "arbitrary")),
        name="proj_rope",
    )(xn, w, *tabs)


def _mm_kernel(x_ref, w_ref, o_ref):
    o_ref[...] = jnp.dot(x_ref[...], w_ref[...], preferred_element_type=F32).astype(o_ref.dtype)


def matmul(x, w, *, bn, out_dtype=F32):
    t, k = x.shape
    n = w.shape[1]
    bm = _pick(t, 1024)
    return pl.pallas_call(
        _mm_kernel,
        grid=(t // bm, n // bn),
        in_specs=[pl.BlockSpec((bm, k), lambda i, j: (i, 0)), pl.BlockSpec((k, bn), lambda i, j: (0, j))],
        out_specs=pl.BlockSpec((bm, bn), lambda i, j: (i, j)),
        out_shape=jax.ShapeDtypeStruct((t, n), out_dtype),
        compiler_params=_cp(("parallel", "arbitrary")),
        name="proj_plain",
    )(x, w)


def _merge_kernel(a1_ref, a2_ref, a3_ref, ob_ref, oc_ref, w_ref, g0_ref, g1_ref, g2_ref, o_ref):
    xa = (a1_ref[...] + a2_ref[...] + a3_ref[...]).astype(BF16)
    acc = _sigmoid(g0_ref[...]) * jnp.dot(xa, w_ref[0], preferred_element_type=F32)
    acc += _sigmoid(g1_ref[...]) * jnp.dot(ob_ref[...].astype(BF16), w_ref[1], preferred_element_type=F32)
    acc += _sigmoid(g2_ref[...]) * jnp.dot(oc_ref[...], w_ref[2], preferred_element_type=F32)
    o_ref[...] = acc.astype(o_ref.dtype)


def merge_branches(a1, a2, a3, ob, oc, wb, cg):
    t = a1.shape[0]
    bm = _pick(t, 256)
    bn = D_MODEL
    nj = D_MODEL // bn
    xs = pl.BlockSpec((bm, BRANCH_W), lambda i, j: (i, 0))

    def gspec(br):
        return pl.BlockSpec((bm, bn), lambda i, j: (i, br * nj + j))

    return pl.pallas_call(
        _merge_kernel,
        grid=(t // bm, nj),
        in_specs=[xs, xs, xs, xs, xs, pl.BlockSpec((N_BRANCH, BRANCH_W, bn), lambda i, j: (0, 0, j)),
                  gspec(0), gspec(1), gspec(2)],
        out_specs=pl.BlockSpec((bm, bn), lambda i, j: (i, j)),
        out_shape=jax.ShapeDtypeStruct((t, D_MODEL), BF16),
        compiler_params=_cp(("parallel", "arbitrary")),
        name="merge",
    )(a1, a2, a3, ob, oc, wb, cg, cg, cg)


def _outproj_kernel(m_ref, w_ref, x_ref, gpost_ref, gpre_ref, xo_ref, hn_ref):
    y = jnp.dot(m_ref[...], w_ref[...], preferred_element_type=F32)
    xn = x_ref[...] + _rms(y, gpost_ref[...])
    xo_ref[...] = xn
    hn_ref[...] = _rms(xn, gpre_ref[...]).astype(hn_ref.dtype)


def outproj_residual(merged, w_out, x, g_post, g_ffn_pre):
    t, d = x.shape
    bm = _pick(t, 512)
    row = pl.BlockSpec((bm, d), lambda i: (i, 0))
    vec = pl.BlockSpec((1, d), lambda i: (0, 0))
    return pl.pallas_call(
        _outproj_kernel,
        grid=(t // bm,),
        in_specs=[row, pl.BlockSpec((d, d), lambda i: (0, 0)), row, vec, vec],
        out_specs=[row, row],
        out_shape=[jax.ShapeDtypeStruct((t, d), F32), jax.ShapeDtypeStruct((t, d), BF16)],
        compiler_params=_cp(("parallel",)),
        name="outproj",
    )(merged, w_out, x, g_post.reshape(1, d), g_ffn_pre.reshape(1, d))


def _ffn_gu_kernel(h_ref, wg_ref, wu_ref, o_ref):
    h = h_ref[...]
    g = jnp.dot(h, wg_ref[...], preferred_element_type=F32)
    u = jnp.dot(h, wu_ref[...], preferred_element_type=F32)
    o_ref[...] = (g * _sigmoid(g) * u).astype(o_ref.dtype)


def ffn_gate_up(hn, w_gu):
    t, d = hn.shape
    bm = _pick(t, 1024)
    bn = 512
    nj = FFN_HIDDEN // bn
    return pl.pallas_call(
        _ffn_gu_kernel,
        grid=(t // bm, nj),
        in_specs=[pl.BlockSpec((bm, d), lambda i, j: (i, 0)), pl.BlockSpec((d, bn), lambda i, j: (0, j)),
                  pl.BlockSpec((d, bn), lambda i, j: (0, nj + j))],
        out_specs=pl.BlockSpec((bm, bn), lambda i, j: (i, j)),
        out_shape=jax.ShapeDtypeStruct((t, FFN_HIDDEN), BF16),
        compiler_params=_cp(("parallel", "arbitrary")),
        name="ffn_gate_up",
    )(hn, w_gu, w_gu)


def _ffn_down_kernel(a_ref, w_ref, x_ref, gpost_ref, gnext_ref, y_ref, xn_ref, acc_ref, *, nk):
    k = pl.program_id(1)

    @pl.when(k == 0)
    def _():
        acc_ref[...] = jnp.zeros_like(acc_ref)

    acc_ref[...] += jnp.dot(a_ref[...], w_ref[...], preferred_element_type=F32)

    @pl.when(k == nk - 1)
    def _():
        y = x_ref[...] + _rms(acc_ref[...], gpost_ref[...])
        y_ref[...] = y
        xn_ref[...] = _rms(y, gnext_ref[...]).astype(xn_ref.dtype)


def ffn_down_residual(act, w_down, x, g_post, g_next):
    t, d = x.shape
    bm = _pick(t, 512)
    bk = 1408
    nk = FFN_HIDDEN // bk
    row = pl.BlockSpec((bm, d), lambda i, k: (i, 0))
    vec = pl.BlockSpec((1, d), lambda i, k: (0, 0))
    return pl.pallas_call(
        functools.partial(_ffn_down_kernel, nk=nk),
        grid=(t // bm, nk),
        in_specs=[pl.BlockSpec((bm, bk), lambda i, k: (i, k)), pl.BlockSpec((bk, d), lambda i, k: (k, 0)),
                  row, vec, vec],
        out_specs=[row, row],
        out_shape=[jax.ShapeDtypeStruct((t, d), F32), jax.ShapeDtypeStruct((t, d), BF16)],
        scratch_shapes=[pltpu.VMEM((bm, d), F32)],
        compiler_params=_cp(("parallel", "arbitrary")),
        name="ffn_down",
    )(act, w_down, x, g_post.reshape(1, d), g_next.reshape(1, d))


def _conv_kernel(u_ref, b_ref, c_ref, w_ref, o_ref, st_ref):
    v = c_ref[...] * u_ref[...]
    s = v.shape[0]
    row = lax.broadcasted_iota(jnp.int32, v.shape, 0)
    v1 = jnp.where(row >= 1, pltpu.roll(v, 1, 0), 0.0)
    v2 = jnp.where(row >= 2, pltpu.roll(v, 2, 0), 0.0)
    w = w_ref[...]
    y = w[0:1] * v2 + w[1:2] * v1 + w[2:3] * v
    o_ref[...] = (b_ref[...] * y).astype(o_ref.dtype)
    st_ref[...] = v[s - (CONV_WIDTH - 1):, :]


def conv_prompt(cg, conv_w, n, s):
    bc = 256
    nj = CONV_DIM // bc
    base = N_BRANCH * D_MODEL // bc
    return pl.pallas_call(
        _conv_kernel,
        grid=(n, nj),
        in_specs=[pl.BlockSpec((s, bc), lambda b, j: (b, base + j)),
                  pl.BlockSpec((s, bc), lambda b, j: (b, base + nj + j)),
                  pl.BlockSpec((s, bc), lambda b, j: (b, base + 2 * nj + j)),
                  pl.BlockSpec((CONV_WIDTH, bc), lambda b, j: (0, j))],
        out_specs=[pl.BlockSpec((s, bc), lambda b, j: (b, j)),
                   pl.BlockSpec((None, CONV_WIDTH - 1, bc), lambda b, j: (b, 0, j))],
        out_shape=[jax.ShapeDtypeStruct((n * s, CONV_DIM), BF16),
                   jax.ShapeDtypeStruct((n, CONV_WIDTH - 1, CONV_DIM), F32)],
        compiler_params=_cp(("parallel", "arbitrary")),
        name="conv_prompt",
    )(cg, cg, cg, conv_w)


def _conv_dec_kernel(u_ref, b_ref, c_ref, buf_ref, w_ref, o_ref, st_ref):
    v = c_ref[...] * u_ref[...]
    b0 = buf_ref[:, 0, :]
    b1 = buf_ref[:, 1, :]
    w = w_ref[...]
    y = w[0:1] * b0 + w[1:2] * b1 + w[2:3] * v
    o_ref[...] = (b_ref[...] * y).astype(o_ref.dtype)
    st_ref[:, 0, :] = b1
    st_ref[:, 1, :] = v


def conv_decode(cg, buf, conv_w):
    n = cg.shape[0]
    base = N_BRANCH * D_MODEL // CONV_DIM
    blk = lambda j: pl.BlockSpec((n, CONV_DIM), lambda i: (0, base + j))
    full3 = pl.BlockSpec((n, CONV_WIDTH - 1, CONV_DIM), lambda i: (0, 0, 0))
    return pl.pallas_call(
        _conv_dec_kernel,
        grid=(1,),
        in_specs=[blk(0), blk(1), blk(2), full3, pl.BlockSpec((CONV_WIDTH, CONV_DIM), lambda i: (0, 0))],
        out_specs=[pl.BlockSpec((n, CONV_DIM), lambda i: (0, 0)), full3],
        out_shape=[jax.ShapeDtypeStruct((n, CONV_DIM), BF16),
                   jax.ShapeDtypeStruct((n, CONV_WIDTH - 1, CONV_DIM), F32)],
        compiler_params=_cp(("arbitrary",)),
        name="conv_decode",
    )(cg, cg, cg, buf, conv_w)


def _compress(read_x, w1_ref, w2_ref, pe_ref, nc):
    out = []
    for kv in range(2):
        acc = jnp.zeros((NSA_KV * nc, 2 * HEAD_DIM), F32)
        bias = jnp.zeros((8, 2 * HEAD_DIM), F32)
        for tp in range(CMP_STRIDE // 2):
            xt = jnp.concatenate(
                [jnp.concatenate([read_x(2 * tp + u, kv * NSA_KV + g) for u in range(2)], axis=1)
                 for g in range(NSA_KV)], axis=0).astype(BF16)
            wt = w1_ref[kv, tp]
            acc += jnp.dot(xt, wt, preferred_element_type=F32)
            bias += jnp.dot(pe_ref[kv, tp], wt, preferred_element_type=F32)
        pe_bias = bias[0:1, :HEAD_DIM] + bias[1:2, HEAD_DIM:]
        per_group = []
        for g in range(NSA_KV):
            a = acc[g * nc:(g + 1) * nc, :HEAD_DIM]
            b = acc[g * nc:(g + 1) * nc, HEAD_DIM:]
            pre = a + pltpu.roll(b, nc - 1, 0) + pe_bias
            hid = pre * _sigmoid(pre)
            per_group.append(jnp.dot(hid.astype(BF16), w2_ref[kv], preferred_element_type=F32))
        out.append(per_group)
    return out[0], out[1]


def _gate_rows(gates_blk):
    return gates_blk.T


def _stack_heads(q_ref, g, bq):
    return jnp.concatenate(
        [q_ref[:, (g * NSA_REP + r) * HEAD_DIM:(g * NSA_REP + r + 1) * HEAD_DIM] for r in range(NSA_REP)],
        axis=0).astype(BF16)


KV_SLOTS = 2 * NSA_KV
CHUNK_ROWS = CMP_STRIDE * KV_SLOTS
CHUNK_PITCH = CHUNK_ROWS + 8


def _kv_rows(kv_ref, k0, n, slot):
    return kv_ref[pl.ds(k0 * KV_SLOTS + slot, n, stride=KV_SLOTS), :]


EXP2_SCALE = ATTN_SCALE * math.log2(math.e)
FLASH_KC = 256


def _flash_scratch(bq):
    nq = NSA_REP * bq
    return [pltpu.VMEM((NSA_KV, nq, HEAD_DIM), BF16), pltpu.VMEM((NSA_KV, 1, nq), F32),
            pltpu.VMEM((NSA_KV, 1, nq), F32), pltpu.VMEM((NSA_KV, HEAD_DIM, nq), F32),
            pltpu.VMEM((NSA_KV, FLASH_KC, nq), F32), pltpu.VMEM((NSA_KV, FLASH_KC, nq), F32)]


def _flash_t(st, q_ref, kv_ref, o_ref, lo, hi, kc, bias_fn, bq, gate_t, branch):
    qs_ref, m_ref, l_ref, acc_ref, s_even, s_odd = st
    for g in range(NSA_KV):
        qs_ref[g] = _stack_heads(q_ref, g, bq)
    m_ref[...] = jnp.full(m_ref.shape, NEG_INF, F32)
    l_ref[...] = jnp.zeros(l_ref.shape, F32)
    acc_ref[...] = jnp.zeros(acc_ref.shape, F32)

    def scores(c, s_ref):
        k0 = pl.multiple_of(c * kc, kc)
        for g in range(NSA_KV):
            kb = _kv_rows(kv_ref, k0, kc, g).astype(BF16)
            s = lax.dot_general(kb, qs_ref[g], (((1,), (1,)), ((), ())), preferred_element_type=F32)
            s_ref[g] = s + jnp.concatenate([bias_fn(k0, g)] * NSA_REP, axis=1)

    def step(c, s_cur, s_nxt):
        scores(jnp.minimum(c + 1, hi - 1), s_nxt)
        k0 = pl.multiple_of(c * kc, kc)
        for g in range(NSA_KV):
            vb = _kv_rows(kv_ref, k0, kc, NSA_KV + g).astype(BF16)
            s = s_cur[g]
            m = m_ref[g]
            m_new = jnp.maximum(m, jnp.max(s, axis=0, keepdims=True))
            alpha = jnp.exp2((m - m_new) * EXP2_SCALE)
            p = jnp.exp2((s - m_new) * EXP2_SCALE)
            m_ref[g] = m_new
            l_ref[g] = alpha * l_ref[g] + jnp.sum(p, axis=0, keepdims=True)
            pv = lax.dot_general(vb, p.astype(BF16), (((0,), (0,)), ((), ())), preferred_element_type=F32)
            acc_ref[g] = alpha * acc_ref[g] + pv

    scores(lo, s_even)
    n = hi - lo

    def pair(i, _):
        c = lo + 2 * i
        step(c, s_even, s_odd)
        step(c + 1, s_odd, s_even)
        return 0

    lax.fori_loop(0, n // 2, pair, 0)

    @pl.when(n % 2 == 1)
    def _():
        step(hi - 1, s_even, s_odd)
    for g in range(NSA_KV):
        ot = jnp.where(m_ref[g] > 0.5 * NEG_INF, acc_ref[g] / jnp.maximum(l_ref[g], 1e-30), 0.0)
        _store_heads(o_ref, [ot[:, r * bq:(r + 1) * bq] for r in range(NSA_REP)], g, gate_t, branch)


def _store_heads(o_ref, heads, g, gate_t, branch):
    for r, oh in enumerate(heads):
        h = g * NSA_REP + r
        if gate_t is not None:
            oh = oh * _sigmoid(gate_t[h * 3 + branch:h * 3 + branch + 1, :])
        o_ref[:, h * HEAD_DIM:(h + 1) * HEAD_DIM] = oh.T


def _cmp_prompt_kernel(x_ref, q_ref, gates_ref, w1_ref, w2_ref, pe_ref, ov_ref, o_ref, ps_ref, kc_ref, vc_ref,
                       *, nc, bq):
    qi = pl.program_id(1)

    @pl.when(qi == 0)
    def _():
        def read_x(t, kg):
            return x_ref[pl.ds(t * KV_SLOTS + kg, nc, stride=CMP_STRIDE * KV_SLOTS), :]
        k_c, v_c = _compress(read_x, w1_ref, w2_ref, pe_ref, nc)
        for g in range(NSA_KV):
            kc_ref[g] = k_c[g].astype(BF16)
            vc_ref[g] = v_c[g].astype(BF16)

    q0 = qi * bq
    nq = NSA_REP * bq
    gate_t = _gate_rows(gates_ref[...])
    pos = q0 + lax.broadcasted_iota(jnp.int32, (nc, bq), 1)
    blk_end = lax.broadcasted_iota(jnp.int32, (nc, bq), 0) * CMP_STRIDE + (CMP_LEN - 1)
    mk1 = blk_end <= pos
    mk = jnp.concatenate([mk1] * NSA_REP, axis=1)
    for g in range(NSA_KV):
        qs = _stack_heads(q_ref, g, bq)
        s = lax.dot_general(kc_ref[g], qs, (((1,), (1,)), ((), ())), preferred_element_type=F32) * ATTN_SCALE
        s = jnp.where(mk, s, NEG_INF)
        m = jnp.max(s, axis=0, keepdims=True)
        e = jnp.where(mk, jnp.exp(s - m), 0.0)
        p = e / jnp.maximum(jnp.sum(e, axis=0, keepdims=True), 1e-30)
        ot = lax.dot_general(vc_ref[g], p.astype(BF16), (((0,), (0,)), ((), ())), preferred_element_type=F32)
        _store_heads(o_ref, [ot[:, r * bq:(r + 1) * bq] for r in range(NSA_REP)], g, gate_t, 0)
        psum = p[:, 0:bq]
        for r in range(1, NSA_REP):
            psum = psum + p[:, r * bq:(r + 1) * bq]
        ps_ref[g] = jnp.dot(ov_ref[...], psum, precision=lax.Precision.HIGHEST, preferred_element_type=F32)


def cmp_prompt(cmp_x, q, idxm, w1cat, w2, pe8, ov_t, n, s):
    nc = s // CMP_STRIDE
    bq = 256
    nq = s // bq
    n_slc = ov_t.shape[0]
    t = n * s
    full = lambda a: pl.BlockSpec(a.shape, lambda b, i: (0,) * a.ndim)
    return pl.pallas_call(
        functools.partial(_cmp_prompt_kernel, nc=nc, bq=bq),
        grid=(n, nq),
        in_specs=[pl.BlockSpec((None, s * KV_SLOTS, HEAD_DIM), lambda b, i: (0, b, 0)),
                  pl.BlockSpec((bq, BRANCH_W), lambda b, i: (b * nq + i, 0)),
                  pl.BlockSpec((bq, LANE), lambda b, i: (b * nq + i, 5)),
                  full(w1cat), full(w2), full(pe8), full(ov_t)],
        out_specs=[pl.BlockSpec((bq, BRANCH_W), lambda b, i: (b * nq + i, 0)),
                   pl.BlockSpec((NSA_KV, n_slc, bq), lambda b, i: (0, 0, b * nq + i))],
        out_shape=[jax.ShapeDtypeStruct((t, BRANCH_W), F32), jax.ShapeDtypeStruct((NSA_KV, n_slc, t), F32)],
        scratch_shapes=[pltpu.VMEM((NSA_KV, nc, HEAD_DIM), BF16), pltpu.VMEM((NSA_KV, nc, HEAD_DIM), BF16)],
        compiler_params=_cp(("arbitrary", "arbitrary")),
        name="nsa_cmp_prompt",
    )(cmp_x, q, idxm, w1cat, w2, pe8, ov_t)


def _topk_rank_rows(sc, n_rows, k):
    j = lax.broadcasted_iota(jnp.int32, sc.shape, 0)
    rank = jnp.zeros(sc.shape, jnp.int32)
    for i in range(n_rows):
        si = sc[i:i + 1, :]
        beats = jnp.where(si > sc, 1, jnp.where((si == sc) & (j > i), 1, 0))
        rank = rank + beats
    return jnp.where(rank < k, 1.0, 0.0)


def _slc_prompt_kernel(q_ref, kv_ref, ps_ref, gates_ref, o_ref, bias_ref, *st, n_slc, bq, kc):
    qi = pl.program_id(1)
    q0 = qi * bq
    gate_t = _gate_rows(gates_ref[...])
    blk = lax.broadcasted_iota(jnp.int32, (n_slc, bq), 0)
    pos = q0 + lax.broadcasted_iota(jnp.int32, (n_slc, bq), 1)
    cur = pos // SEL_BLOCK
    forced = (blk == 0) | (blk == cur) | (blk == cur - 1)
    visible = blk * SEL_BLOCK <= pos
    qpos = q0 + lax.broadcasted_iota(jnp.int32, (SEL_BLOCK, bq), 1)
    krow = lax.broadcasted_iota(jnp.int32, (SEL_BLOCK, bq), 0)
    for g in range(NSA_KV):
        sc = jnp.where(forced, FORCE_SCORE, jnp.where(visible, ps_ref[g], NEG_INF))
        sel = _topk_rank_rows(sc, n_slc, min(N_SEL, n_slc))
        for j in range(n_slc):
            keep = (jnp.broadcast_to(sel[j:j + 1, :], (SEL_BLOCK, bq)) > 0.5) & (j * SEL_BLOCK + krow <= qpos)
            bias_ref[g, j * SEL_BLOCK:(j + 1) * SEL_BLOCK, :] = jnp.where(keep, 0.0, NEG_INF)

    _flash_t(st, q_ref, kv_ref, o_ref, 0, (q0 + bq + kc - 1) // kc, kc,
             lambda k0, g: bias_ref[g, pl.ds(k0, kc), :], bq, gate_t, 1)


def slc_prompt(q, kv4, p_slc, idxm, n, s):
    bq, kc = 256, FLASH_KC
    nq = s // bq
    n_slc = p_slc.shape[1]
    t = n * s
    return pl.pallas_call(
        functools.partial(_slc_prompt_kernel, n_slc=n_slc, bq=bq, kc=kc),
        grid=(n, nq),
        in_specs=[pl.BlockSpec((bq, BRANCH_W), lambda b, i: (b * nq + i, 0)),
                  pl.BlockSpec((None, s * KV_SLOTS, HEAD_DIM), lambda b, i: (1, b, 0)),
                  pl.BlockSpec((NSA_KV, n_slc, bq), lambda b, i: (0, 0, b * nq + i)),
                  pl.BlockSpec((bq, LANE), lambda b, i: (b * nq + i, 5))],
        out_specs=pl.BlockSpec((bq, BRANCH_W), lambda b, i: (b * nq + i, 0)),
        out_shape=jax.ShapeDtypeStruct((t, BRANCH_W), F32),
        scratch_shapes=[pltpu.VMEM((NSA_KV, s, bq), F32)] + _flash_scratch(bq),
        compiler_params=_cp(("parallel", "arbitrary")),
        name="nsa_slc_prompt",
    )(q, kv4, p_slc, idxm)


def _win_prompt_kernel(q_ref, kv_ref, gates_ref, o_ref, *st, bq, kc):
    qi = pl.program_id(1)
    q0 = qi * bq
    gate_t = _gate_rows(gates_ref[...])
    qpos = q0 + lax.broadcasted_iota(jnp.int32, (kc, bq), 1)
    krow = lax.broadcasted_iota(jnp.int32, (kc, bq), 0)

    def bias_fn(k0, g):
        rel = qpos - (k0 + krow)
        return jnp.where((rel >= 0) & (rel <= WINDOW), 0.0, NEG_INF)

    lo = jnp.maximum(q0 - WINDOW, 0) // kc
    hi = (q0 + bq + kc - 1) // kc
    _flash_t(st, q_ref, kv_ref, o_ref, lo, hi, kc, bias_fn, bq, gate_t, 2)


def win_prompt(q, kv4, idxm, n, s):
    bq, kc = 256, FLASH_KC
    nq = s // bq
    t = n * s
    return pl.pallas_call(
        functools.partial(_win_prompt_kernel, bq=bq, kc=kc),
        grid=(n, nq),
        in_specs=[pl.BlockSpec((bq, BRANCH_W), lambda b, i: (b * nq + i, 0)),
                  pl.BlockSpec((None, s * KV_SLOTS, HEAD_DIM), lambda b, i: (2, b, 0)),
                  pl.BlockSpec((bq, LANE), lambda b, i: (b * nq + i, 5))],
        out_specs=pl.BlockSpec((bq, BRANCH_W), lambda b, i: (b * nq + i, 0)),
        out_shape=jax.ShapeDtypeStruct((t, BRANCH_W), F32),
        scratch_shapes=_flash_scratch(bq),
        compiler_params=_cp(("parallel", "arbitrary")),
        name="nsa_win_prompt",
    )(q, kv4, idxm)


def _order_key(x):
    b = pltpu.bitcast(x + 0.0, jnp.int32)
    return b ^ ((b >> 31) & jnp.int32(0x7FFFFFFF))


def _radix_kth(count_ge, k, shape):
    zero = jnp.zeros(shape, jnp.int32)
    base = jnp.where(count_ge(zero) >= k, zero, jnp.full(shape, INT_MIN, jnp.int32))

    def body(i, base):
        cand = base | jnp.left_shift(jnp.int32(1), 30 - i)
        return jnp.where(count_ge(cand) >= k, cand, base)

    return lax.fori_loop(0, 31, body, base)


def _tie_cut(count_eq_below, need, shape, n_bits):
    def body(i, m):
        cand = m | jnp.left_shift(jnp.int32(1), n_bits - 1 - i)
        return jnp.where(count_eq_below(cand) <= need, cand, m)

    return lax.fori_loop(0, n_bits, body, jnp.zeros(shape, jnp.int32))


def _dsa_prompt_kernel(q_ref, qi_ref, kidx_ref, kv_ref, gates_ref, o_ref, key_ref, bias_ref, *st,
                       bq, kc, kf, s_len):
    qi = pl.program_id(1)
    q0 = qi * bq
    n_chunks = (q0 + bq + kc - 1) // kc
    gate_t = _gate_rows(gates_ref[...])
    w_row = jnp.concatenate([gate_t[24 + h:25 + h, :] for h in range(IDX_HEADS)], axis=1) * (
        IDX_HEADS ** -0.5 * IDX_SCALE)
    qis = jnp.concatenate([qi_ref[:, h * IDX_DIM:(h + 1) * IDX_DIM] for h in range(IDX_HEADS)], axis=0).astype(BF16)
    qpos = q0 + lax.broadcasted_iota(jnp.int32, (kc, bq), 1)
    krow = lax.broadcasted_iota(jnp.int32, (kc, bq), 0)

    def score_body(c, _):
        k0 = pl.multiple_of(c * kc, kc)
        kb = kidx_ref[pl.ds(k0, kc), 0:IDX_DIM].astype(BF16)
        lg = lax.dot_general(kb, qis, (((1,), (1,)), ((), ())), preferred_element_type=F32)
        wl = jnp.maximum(lg, 0.0) * w_row
        sc = wl[:, 0:bq]
        for h in range(1, IDX_HEADS):
            sc = sc + wl[:, h * bq:(h + 1) * bq]
        sc = jnp.where(k0 + krow <= qpos, sc, NEG_INF)
        key_ref[pl.ds(k0, kc), :] = _order_key(sc)
        return 0

    lax.fori_loop(0, n_chunks, score_body, 0)

    def count(pred):
        def body(c, acc):
            k0 = pl.multiple_of(c * kc, kc)
            hit = jnp.where(pred(key_ref[pl.ds(k0, kc), :], k0 + krow), 1, 0)
            return acc + jnp.sum(hit.reshape(kc // 8, 8, bq), axis=0)
        acc = lax.fori_loop(0, n_chunks, body, jnp.zeros((8, bq), jnp.int32))
        return jnp.sum(acc, axis=0, keepdims=True)

    k_top = min(DSA_TOPK, s_len // 4)
    row1 = (1, bq)

    def select(_):
        thr = _radix_kth(lambda cand: count(lambda key, idx: key >= cand), k_top, row1)
        need = k_top - count(lambda key, idx: key > thr)
        n_eq = count(lambda key, idx: key == thr)
        n_bits = max(1, int(s_len).bit_length())
        cut = lax.cond(
            jnp.any(n_eq != need),
            lambda _: _tie_cut(lambda m: count(lambda key, idx: (key == thr) & (idx < m)), need, row1, n_bits),
            lambda _: jnp.full(row1, s_len, jnp.int32), 0)
        return thr, cut

    thr, cut = lax.cond(q0 + bq > k_top, select,
                        lambda _: (jnp.full(row1, INT_MIN, jnp.int32), jnp.full(row1, s_len, jnp.int32)), 0)

    def mask_body(c, _):
        k0 = pl.multiple_of(c * kc, kc)
        key = key_ref[pl.ds(k0, kc), :]
        idx = k0 + krow
        sel = ((key > thr) | ((key == thr) & (idx < cut))) & (idx <= qpos)
        bias_ref[pl.ds(k0, kc), :] = jnp.where(sel, 0.0, NEG_INF)
        return 0

    lax.fori_loop(0, n_chunks, mask_body, 0)

    _flash_t(st, q_ref, kv_ref, o_ref, 0, (q0 + bq + kf - 1) // kf, kf,
             lambda k0, g: bias_ref[pl.ds(k0, kf), :], bq, None, 0)


def dsa_prompt(q, kv4, idxm, n, s):
    bq = kc = 256
    nq = s // bq
    t = n * s
    return pl.pallas_call(
        functools.partial(_dsa_prompt_kernel, bq=bq, kc=kc, kf=FLASH_KC, s_len=s),
        grid=(n, nq),
        in_specs=[pl.BlockSpec((bq, BRANCH_W), lambda b, i: (b * nq + i, 1)),
                  pl.BlockSpec((bq, IDX_HEADS * IDX_DIM), lambda b, i: (b * nq + i, 0)),
                  pl.BlockSpec((s, LANE), lambda b, i: (b, 4)),
                  pl.BlockSpec((None, s * KV_SLOTS, HEAD_DIM), lambda b, i: (3, b, 0)),
                  pl.BlockSpec((bq, LANE), lambda b, i: (b * nq + i, 5))],
        out_specs=pl.BlockSpec((bq, BRANCH_W), lambda b, i: (b * nq + i, 0)),
        out_shape=jax.ShapeDtypeStruct((t, BRANCH_W), F32),
        scratch_shapes=[pltpu.VMEM((s, bq), jnp.int32), pltpu.VMEM((s, bq), F32)] + _flash_scratch(bq),
        compiler_params=_cp(("parallel", "arbitrary")),
        name="dsa_prompt",
    )(q, idxm, idxm, kv4, idxm)


def _page_copies(pt_ref, cache_ref, layer, buf_ref, sem_ref, seq, slot, n_pages, rows, mode):
    def each(fn):
        def body(p, _):
            page = pt_ref[seq, p]
            if mode == "chunks":
                for cc in range(rows):
                    dst = buf_ref.at[slot, pl.ds((p * rows + cc) * CHUNK_PITCH, CHUNK_ROWS)]
                    fn(pltpu.make_async_copy(cache_ref.at[layer, page, cc], dst, sem_ref.at[slot]))
                return 0
            win = pl.ds(p * rows, rows)
            dst = buf_ref.at[slot, :, win] if mode == "lanes" else buf_ref.at[slot, win]
            fn(pltpu.make_async_copy(cache_ref.at[layer, page], dst, sem_ref.at[slot]))
            return 0
        lax.fori_loop(0, n_pages, body, 0)
    return each


def _gather_step(pt_ref, cache_ref, layer, buf_ref, sem_ref, n_pages, rows, mode="rows"):
    b = pl.program_id(0)
    nb = pl.num_programs(0)
    slot = b % 2
    copies = functools.partial(_page_copies, pt_ref, cache_ref, layer, buf_ref, sem_ref,
                               n_pages=n_pages, rows=rows, mode=mode)

    @pl.when(b == 0)
    def _():
        copies(seq=0, slot=0)(lambda cp: cp.start())

    @pl.when(b + 1 < nb)
    def _():
        copies(seq=b + 1, slot=1 - slot)(lambda cp: cp.start())

    copies(seq=b, slot=slot)(lambda cp: cp.wait())
    return slot


def _head_column(row, offset, stride):
    lane = lax.broadcasted_iota(jnp.int32, (8, LANE), 1)
    h = lax.broadcasted_iota(jnp.int32, (8, LANE), 0)
    return jnp.sum(jnp.where(lane == offset + stride * h, jnp.broadcast_to(row, (8, LANE)), 0.0), axis=1, keepdims=True)


def _q8(q_ref, width):
    return jnp.concatenate([q_ref[0:1, h * width:(h + 1) * width] for h in range(8)], axis=0)


def _row_spec(width, blk, n_extra):
    if n_extra:
        return pl.BlockSpec((None, 1, width), lambda b, pt: (b, 0, blk))
    return pl.BlockSpec((None, 1, width), lambda b: (b, 0, blk))


def _cmp_dec_kernel(pt_ref, cache_ref, q_ref, gates_ref, w1_ref, w2_ref, pe_ref, ov_ref, o_ref, ps_ref,
                    buf_ref, sem_ref, *, layer, n_pages, p0):
    nc = n_pages * PAGE_SIZE // CMP_STRIDE
    slot = _gather_step(pt_ref, cache_ref, layer, buf_ref, sem_ref, n_pages, PAGE_SIZE // CMP_STRIDE,
                        mode="chunks")

    def read_x(t, kg):
        return buf_ref[slot, pl.ds(t * KV_SLOTS + kg, nc, stride=CHUNK_PITCH), :]

    k_c, v_c = _compress(read_x, w1_ref, w2_ref, pe_ref, nc)
    q8 = _q8(q_ref, HEAD_DIM).astype(BF16)
    head = lax.broadcasted_iota(jnp.int32, (8, 1), 0)
    blk_end = lax.broadcasted_iota(jnp.int32, (8, nc), 1) * CMP_STRIDE + (CMP_LEN - 1)
    mk = blk_end <= p0
    s = jnp.zeros((8, nc), F32)
    for g in range(NSA_KV):
        sg = lax.dot_general(q8, k_c[g].astype(BF16), (((1,), (1,)), ((), ())), preferred_element_type=F32)
        s = jnp.where(head // NSA_REP == g, sg, s)
    s = jnp.where(mk, s * ATTN_SCALE, NEG_INF)
    m = jnp.max(s, axis=1, keepdims=True)
    e = jnp.where(mk, jnp.exp(s - m), 0.0)
    p = e / jnp.maximum(jnp.sum(e, axis=1, keepdims=True), 1e-30)
    o = jnp.zeros((8, HEAD_DIM), F32)
    for g in range(NSA_KV):
        og = jnp.dot(p.astype(BF16), v_c[g].astype(BF16), preferred_element_type=F32)
        o = jnp.where(head // NSA_REP == g, og, o)
        psum = jnp.sum(jnp.where(head // NSA_REP == g, p, 0.0), axis=0, keepdims=True)
        ps8 = jnp.dot(jnp.broadcast_to(psum, (8, nc)), ov_ref[...], precision=lax.Precision.HIGHEST,
                      preferred_element_type=F32)
        ps_ref[g:g + 1, :] = ps8[0:1, :]
    gate = _sigmoid(_head_column(gates_ref[...], 0, 3))
    o_ref[...] = o * gate


def cmp_decode(page_table, cache_x, layer, q, idxm, w1cat, w2, pe8, ov, p0):
    n, n_pages = page_table.shape
    n_slc_pad = ov.shape[1]
    full = lambda a: pl.BlockSpec(a.shape, lambda b, pt: (0,) * a.ndim)
    gs = pltpu.PrefetchScalarGridSpec(
        num_scalar_prefetch=1,
        grid=(n,),
        in_specs=[pl.BlockSpec(memory_space=pl.ANY), _row_spec(BRANCH_W, 0, 1), _row_spec(LANE, 5, 1),
                  full(w1cat), full(w2), full(pe8), full(ov)],
        out_specs=[pl.BlockSpec((None, NSA_HEADS, HEAD_DIM), lambda b, pt: (b, 0, 0)),
                   pl.BlockSpec((None, NSA_KV, n_slc_pad), lambda b, pt: (b, 0, 0))],
        scratch_shapes=[pltpu.VMEM((2, n_pages * PAGE_SIZE // CMP_STRIDE * CHUNK_PITCH, HEAD_DIM), F32),
                        pltpu.SemaphoreType.DMA((2,))],
    )
    return pl.pallas_call(
        functools.partial(_cmp_dec_kernel, layer=layer, n_pages=n_pages, p0=p0),
        grid_spec=gs,
        out_shape=[jax.ShapeDtypeStruct((n, NSA_HEADS, HEAD_DIM), F32),
                   jax.ShapeDtypeStruct((n, NSA_KV, n_slc_pad), F32)],
        compiler_params=_cp(("arbitrary",)),
        name="nsa_cmp_decode",
    )(page_table, cache_x, q, idxm, w1cat, w2, pe8, ov)


def _slc_mask_kernel(ps_ref, e_ref, o_ref, *, n_slc, p0):
    sc = ps_ref[...]
    j = lax.broadcasted_iota(jnp.int32, sc.shape, 1)
    cur = p0 // SEL_BLOCK
    forced = (j == 0) | (j == cur) | (j == cur - 1)
    visible = j * SEL_BLOCK <= p0
    sc = jnp.where(j >= n_slc, BELOW_ALL, jnp.where(forced, FORCE_SCORE, jnp.where(visible, sc, NEG_INF)))
    rank = jnp.zeros(sc.shape, jnp.int32)
    for i in range(n_slc):
        si = sc[:, i:i + 1]
        rank = rank + jnp.where(si > sc, 1, jnp.where((si == sc) & (j > i), 1, 0))
    sel = jnp.where((rank < min(N_SEL, n_slc)) & (j < n_slc), 1.0, 0.0).astype(BF16)
    o_ref[...] = jnp.dot(sel, e_ref[...], preferred_element_type=F32)


def slc_mask_decode(p_slc, expand, n_slc, p0):
    n, g, w = p_slc.shape
    l_pad = expand.shape[1]
    out = pl.pallas_call(
        functools.partial(_slc_mask_kernel, n_slc=n_slc, p0=p0),
        grid=(1,),
        in_specs=[pl.BlockSpec((n * g, w), lambda i: (0, 0)), pl.BlockSpec(expand.shape, lambda i: (0, 0))],
        out_specs=pl.BlockSpec((n * g, l_pad), lambda i: (0, 0)),
        out_shape=jax.ShapeDtypeStruct((n * g, l_pad), F32),
        compiler_params=_cp(("arbitrary",)),
        name="nsa_slc_mask_decode",
    )(p_slc.reshape(n * g, w), expand)
    return out.reshape(n, g, l_pad)


def _idx_score_kernel(pt_ref, cache_ref, qi_ref, knew_ref, gates_ref, o_ref, buf_ref, sem_ref,
                      *, layer, n_pages, p0):
    past = n_pages * PAGE_SIZE
    l_pad = past + LANE
    slot = _gather_step(pt_ref, cache_ref, layer, buf_ref, sem_ref, n_pages, PAGE_SIZE, mode="lanes")
    d = lax.broadcasted_iota(jnp.int32, (IDX_DIM, LANE), 0)
    lane = lax.broadcasted_iota(jnp.int32, (IDX_DIM, LANE), 1)
    k_row = jnp.broadcast_to(knew_ref[...], (IDX_DIM, LANE))
    k_col = jnp.sum(jnp.where(lane == d, k_row, 0.0), axis=1, keepdims=True)
    buf_ref[slot, :, past:l_pad] = jnp.where(lane == 0, k_col, 0.0)
    q8 = _q8(qi_ref, IDX_DIM).astype(BF16)
    lg = jnp.dot(q8, buf_ref[slot].astype(BF16), preferred_element_type=F32) * IDX_SCALE
    w_col = _head_column(gates_ref[...], 24, 1) * (IDX_HEADS ** -0.5)
    sc = jnp.sum(jnp.maximum(lg, 0.0) * w_col, axis=0, keepdims=True)
    key = lax.broadcasted_iota(jnp.int32, (1, l_pad), 1)
    o_ref[...] = jnp.where(key <= p0, sc, BELOW_ALL)


def idx_score_decode(page_table, cache_idx, layer, idxm, p0):
    n, n_pages = page_table.shape
    l_pad = n_pages * PAGE_SIZE + LANE
    gs = pltpu.PrefetchScalarGridSpec(
        num_scalar_prefetch=1,
        grid=(n,),
        in_specs=[pl.BlockSpec(memory_space=pl.ANY), _row_spec(IDX_HEADS * IDX_DIM, 0, 1),
                  _row_spec(LANE, 4, 1), _row_spec(LANE, 5, 1)],
        out_specs=pl.BlockSpec((None, 1, l_pad), lambda b, pt: (b, 0, 0)),
        scratch_shapes=[pltpu.VMEM((2, IDX_DIM, l_pad), F32), pltpu.SemaphoreType.DMA((2,))],
    )
    return pl.pallas_call(
        functools.partial(_idx_score_kernel, layer=layer, n_pages=n_pages, p0=p0),
        grid_spec=gs,
        out_shape=jax.ShapeDtypeStruct((n, 1, l_pad), F32),
        compiler_params=_cp(("arbitrary",)),
        name="dsa_idx_score_decode",
    )(page_table, cache_idx, idxm, idxm, idxm)


def _dsa_mask_kernel(sc_ref, o_ref, *, k_top, p0):
    key = _order_key(sc_ref[...])
    n, l_pad = key.shape
    idx = lax.broadcasted_iota(jnp.int32, key.shape, 1)
    col = (n, 1)

    def count(pred):
        return jnp.sum(jnp.where(pred, 1, 0), axis=1, keepdims=True)

    thr = _radix_kth(lambda cand: count(key >= cand), k_top, col)
    need = k_top - count(key > thr)
    cut = _tie_cut(lambda m: count((key == thr) & (idx < m)), need, col, max(1, int(l_pad).bit_length()))
    sel = ((key > thr) | ((key == thr) & (idx < cut))) & (idx <= p0)
    mask = jnp.where(sel, 1.0, 0.0)
    for g in range(DSA_KV):
        o_ref[:, g, :] = mask


def dsa_mask_decode(score, k_top, p0):
    n, _, l_pad = score.shape
    return pl.pallas_call(
        functools.partial(_dsa_mask_kernel, k_top=k_top, p0=p0),
        grid=(1,),
        in_specs=[pl.BlockSpec((n, l_pad), lambda i: (0, 0))],
        out_specs=pl.BlockSpec((n, DSA_KV, l_pad), lambda i: (0, 0, 0)),
        out_shape=jax.ShapeDtypeStruct((n, DSA_KV, l_pad), F32),
        compiler_params=_cp(("arbitrary",)),
        name="dsa_mask_decode",
    )(score.reshape(n, l_pad))


def _attend_rows(q8, kv_rows, mask_ref, n_keys):
    head = lax.broadcasted_iota(jnp.int32, (8, 1), 0)
    o = jnp.zeros((8, HEAD_DIM), F32)
    for g in range(NSA_KV):
        kb = _kv_rows(kv_rows, 0, n_keys, g).astype(BF16)
        vb = _kv_rows(kv_rows, 0, n_keys, NSA_KV + g).astype(BF16)
        s = lax.dot_general(q8, kb, (((1,), (1,)), ((), ())), preferred_element_type=F32) * ATTN_SCALE
        mk = mask_ref[g:g + 1, :] > 0.5
        s = jnp.where(mk, s, NEG_INF)
        m = jnp.max(s, axis=1, keepdims=True)
        e = jnp.where(mk, jnp.exp(s - m), 0.0)
        p = e / jnp.maximum(jnp.sum(e, axis=1, keepdims=True), 1e-30)
        og = jnp.dot(p.astype(BF16), vb, preferred_element_type=F32)
        o = jnp.where(head // NSA_REP == g, og, o)
    return o


def _attn_paged_kernel(pt_ref, cache_ref, q_ref, kvnew_ref, mask_ref, gates_ref, o_ref, buf_ref, sem_ref,
                       *, layer, n_pages, branch):
    past = n_pages * PAGE_SIZE
    l_pad = past + LANE
    slot = _gather_step(pt_ref, cache_ref, layer, buf_ref, sem_ref, n_pages, PAGE_SIZE * KV_SLOTS)
    buf_ref[slot, past * KV_SLOTS:l_pad * KV_SLOTS, :] = jnp.zeros((LANE * KV_SLOTS, HEAD_DIM), F32)
    buf_ref[slot, past * KV_SLOTS:(past + 1) * KV_SLOTS, :] = kvnew_ref[...]
    o = _attend_rows(_q8(q_ref, HEAD_DIM).astype(BF16), buf_ref.at[slot], mask_ref, l_pad)
    if branch is not None:
        o = o * _sigmoid(_head_column(gates_ref[...], branch, 3))
    o_ref[...] = o


def attn_paged_decode(page_table, cache, layer, q, q_blk, kvnew, mask, idxm, branch):
    n, n_pages = page_table.shape
    l_pad = n_pages * PAGE_SIZE + LANE
    gs = pltpu.PrefetchScalarGridSpec(
        num_scalar_prefetch=1,
        grid=(n,),
        in_specs=[pl.BlockSpec(memory_space=pl.ANY), _row_spec(BRANCH_W, q_blk, 1),
                  pl.BlockSpec((None, KV_SLOTS, HEAD_DIM), lambda b, pt: (b, 0, 0)),
                  pl.BlockSpec((None, NSA_KV, l_pad), lambda b, pt: (b, 0, 0)), _row_spec(LANE, 5, 1)],
        out_specs=pl.BlockSpec((None, NSA_HEADS, HEAD_DIM), lambda b, pt: (b, 0, 0)),
        scratch_shapes=[pltpu.VMEM((2, l_pad * KV_SLOTS, HEAD_DIM), F32), pltpu.SemaphoreType.DMA((2,))],
    )
    return pl.pallas_call(
        functools.partial(_attn_paged_kernel, layer=layer, n_pages=n_pages, branch=branch),
        grid_spec=gs,
        out_shape=jax.ShapeDtypeStruct((n, NSA_HEADS, HEAD_DIM), F32),
        compiler_params=_cp(("arbitrary",)),
        name="attn_paged_decode",
    )(page_table, cache, q, kvnew, mask, idxm)


def _attn_win_kernel(st_ref, q_ref, kvnew_ref, gates_ref, o_ref, buf_ref, mask_ref, *, wb):
    l_pad = wb + LANE
    buf_ref[0:wb * KV_SLOTS, :] = st_ref[...]
    buf_ref[wb * KV_SLOTS:l_pad * KV_SLOTS, :] = jnp.zeros((LANE * KV_SLOTS, HEAD_DIM), F32)
    buf_ref[wb * KV_SLOTS:(wb + 1) * KV_SLOTS, :] = kvnew_ref[...]
    key = lax.broadcasted_iota(jnp.int32, (NSA_KV, l_pad), 1)
    mask_ref[...] = jnp.where(key <= wb, 1.0, 0.0)
    o = _attend_rows(_q8(q_ref, HEAD_DIM).astype(BF16), buf_ref, mask_ref, l_pad)
    o_ref[...] = o * _sigmoid(_head_column(gates_ref[...], 2, 3))


def attn_win_decode(state, layer, q, kvnew, idxm):
    n, wb = state.shape[1], state.shape[2] // KV_SLOTS
    l_pad = wb + LANE
    return pl.pallas_call(
        functools.partial(_attn_win_kernel, wb=wb),
        grid=(n,),
        in_specs=[pl.BlockSpec((None, None, wb * KV_SLOTS, HEAD_DIM), lambda b: (layer, b, 0, 0)),
                  _row_spec(BRANCH_W, 0, 0), pl.BlockSpec((None, KV_SLOTS, HEAD_DIM), lambda b: (b, 0, 0)),
                  _row_spec(LANE, 5, 0)],
        out_specs=pl.BlockSpec((None, NSA_HEADS, HEAD_DIM), lambda b: (b, 0, 0)),
        out_shape=jax.ShapeDtypeStruct((n, NSA_HEADS, HEAD_DIM), F32),
        scratch_shapes=[pltpu.VMEM((l_pad * KV_SLOTS, HEAD_DIM), F32), pltpu.VMEM((NSA_KV, l_pad), F32)],
        compiler_params=_cp(("parallel",)),
        name="attn_win_decode",
    )(state, q, kvnew, idxm)


def _rope_tables(pos, head_dim):
    d_rot = head_dim // ROPE_FRACTION
    half = d_rot // 2
    inv_freq = jnp.exp(jnp.arange(half, dtype=F32) * (-2.0 * math.log(ROPE_THETA) / d_rot))
    ang = pos.astype(F32)[:, None] * inv_freq[None, :]
    cos, sin = jnp.cos(ang), jnp.sin(ang)
    lane = np.arange(LANE) % head_dim
    j = lane % half
    first = jnp.asarray(lane < half)[None, :]
    second = jnp.asarray((lane >= half) & (lane < d_rot))[None, :]
    c = jnp.where(first | second, cos[:, j], 1.0)
    s1 = jnp.where(first, -sin[:, j], 0.0)
    s2 = jnp.where(second, sin[:, j], 0.0)
    return (c, s1, s2), half


def _split_w_in(w_in):
    sizes = (NSA_HEADS * HEAD_DIM, KV_W, KV_W, KV_W, 3 * NSA_HEADS, DSA_HEADS * HEAD_DIM, KV_W,
             IDX_HEADS * IDX_DIM, IDX_DIM, IDX_HEADS, CONV_DIM, CONV_DIM, CONV_DIM, N_BRANCH * D_MODEL)
    offs = np.concatenate([[0], np.cumsum(sizes)])
    col = lambda i: w_in[:, int(offs[i]):int(offs[i + 1])]
    (q_a, cmp_kv, slc_kv, win_kv, gate_a, q_b, dsa_kv, q_i, k_i, w_i, cu, cb, cc, gm) = [col(i) for i in range(14)]
    d = w_in.shape[0]
    zeros = lambda n: jnp.zeros((d, n), w_in.dtype)
    w_q = jnp.concatenate([q_a, q_b], axis=1).astype(BF16)
    w_kv = jnp.concatenate([cmp_kv, slc_kv, win_kv, dsa_kv], axis=1).astype(BF16)
    w_idx = jnp.concatenate([q_i, k_i, zeros(LANE - IDX_DIM), gate_a, w_i, zeros(LANE - 3 * NSA_HEADS - IDX_HEADS)],
                            axis=1).astype(BF16)
    w_cg = jnp.concatenate([gm, cu, cb, cc], axis=1).astype(BF16)
    return w_q, w_kv, w_idx, w_cg


def _cmp_weights(w1, w2, pe):
    half = CMP_STRIDE * HEAD_DIM
    pairs = CMP_STRIDE // 2
    wa = w1[:, :half].reshape(2, pairs, 2 * HEAD_DIM, HEAD_DIM)
    wb = w1[:, half:].reshape(2, pairs, 2 * HEAD_DIM, HEAD_DIM)
    w1cat = jnp.concatenate([wa, wb], axis=-1).astype(BF16)
    pe8 = jnp.zeros((2, pairs, 8, 2 * HEAD_DIM), F32)
    pe8 = pe8.at[:, :, 0, :].set(pe[:, :CMP_STRIDE].reshape(2, pairs, 2 * HEAD_DIM))
    pe8 = pe8.at[:, :, 1, :].set(pe[:, CMP_STRIDE:].reshape(2, pairs, 2 * HEAD_DIM))
    return w1cat, w2.astype(BF16), pe8.astype(BF16)


def _overlap(n_cmp_rows, n_slc, seq_len):
    n_cmp = seq_len // CMP_STRIDE - 1
    c = np.arange(n_cmp_rows)
    c_start = c * CMP_STRIDE
    s_start = np.arange(n_slc) * SEL_BLOCK
    ov = ((c_start[:, None] < s_start[None, :] + SEL_BLOCK) & (c_start[:, None] + CMP_LEN > s_start[None, :])
          & (c[:, None] < n_cmp))
    return ov.astype(np.float32)


def _project(xn, wts, tabs128, half128, tabs64, half64):
    w_q, w_kv, w_idx, w_cg = wts
    q = proj_rope(xn, w_q, tabs128, bn=1024, half=half128, rope_blocks=(True,) * 8, stacked=False)
    kv4 = proj_rope(xn, w_kv, tabs128, bn=KV_W, half=half128, rope_blocks=(True, True, False, False), stacked=True)
    idxm = proj_rope(xn, w_idx, tabs64, bn=w_idx.shape[1], half=half64,
                     rope_blocks=(True,) * 5 + (False,), stacked=False)
    cg = matmul(xn, w_cg, bn=1024)
    return q, kv4, idxm, cg


def _finish_layer(x, branches, oc, cg, lw, g_next):
    a1, a2, a3, ob = branches
    merged = merge_branches(a1, a2, a3, ob, oc, lw["w_branch"], cg)
    x_mid, hn = outproj_residual(merged, lw["w_out"], x, lw["g_mix_post"], lw["g_ffn_pre"])
    act = ffn_gate_up(hn, lw["w_gu"])
    return ffn_down_residual(act, lw["w_down"], x_mid, lw["g_ffn_post"], g_next)


def _prompt_layer(x, xn, lw, n, s, consts, g_next):
    q, kv4, idxm, cg = _project(xn, lw["w_in"], *consts["rope_p"])
    a1, p_slc = cmp_prompt(kv4, q, idxm, *lw["cmp"], consts["ov_p_t"], n, s)
    a2 = slc_prompt(q, kv4, p_slc, idxm, n, s)
    a3 = win_prompt(q, kv4, idxm, n, s)
    ob = dsa_prompt(q, kv4, idxm, n, s)
    oc, conv_state = conv_prompt(cg, lw["conv_w"], n, s)
    y, xn_next = _finish_layer(x, (a1, a2, a3, ob), oc, cg, lw, g_next)
    kv5 = lambda a: a.reshape(n, s, 2, NSA_KV, HEAD_DIM)
    keep = min(WINDOW, s)
    state = (kv5(kv4[0]), kv5(kv4[1]), kv5(kv4[2])[:, s - keep:], kv5(kv4[3]),
             idxm[:, IDX_HEADS * IDX_DIM:IDX_HEADS * IDX_DIM + IDX_DIM].reshape(n, s, IDX_DIM), conv_state)
    return y, xn_next, state


def _sample_layer(x, xn, lw, layer, caches, page_table, consts, g_next):
    n = x.shape[0]
    p0 = consts["p0"]
    q, kv4, idxm, cg = _project(xn, lw["w_in"], *consts["rope_s"])
    cache_cmp_x, cache_slc, state_win, cache_dsa, cache_idx, state_conv = caches
    q3 = q.reshape(n, 1, -1)
    idx3 = idxm.reshape(n, 1, -1)
    new_row = lambda i: kv4[i].reshape(n, KV_SLOTS, HEAD_DIM)
    o_cmp, p_slc = cmp_decode(page_table, cache_cmp_x, layer, q3, idx3, *lw["cmp"], consts["ov_s"], p0)
    slc_mask = slc_mask_decode(p_slc, consts["expand"], consts["n_slc_s"], p0)
    o_slc = attn_paged_decode(page_table, cache_slc, layer, q3, 0, new_row(1), slc_mask, idx3, 1)
    o_win = attn_win_decode(state_win, layer, q3, new_row(2), idx3)
    score = idx_score_decode(page_table, cache_idx, layer, idx3, p0)
    dsa_mask = dsa_mask_decode(score, min(DSA_TOPK, (p0 + 1) // 4), p0)
    o_dsa = attn_paged_decode(page_table, cache_dsa, layer, q3, 1, new_row(3), dsa_mask, idx3, None)
    oc, conv_state = conv_decode(cg, state_conv[layer], lw["conv_w"])
    flat = lambda a: a.reshape(n, BRANCH_W)
    y, xn_next = _finish_layer(x, (flat(o_cmp), flat(o_slc), flat(o_win), flat(o_dsa)), oc, cg, lw, g_next)
    kv5 = lambda a: a.reshape(n, 1, 2, NSA_KV, HEAD_DIM)
    win_all = jnp.concatenate([consts["state_win"][layer], kv5(kv4[2])], axis=1)
    keep = min(WINDOW, win_all.shape[1])
    state = (kv5(kv4[0]), kv5(kv4[1]), win_all[:, win_all.shape[1] - keep:], kv5(kv4[3]),
             idxm[:, IDX_HEADS * IDX_DIM:IDX_HEADS * IDX_DIM + IDX_DIM].reshape(n, 1, IDX_DIM), conv_state)
    return y, xn_next, state


def kernel(x_prompt, x_sample, cache_nsa_cmp_kv, cache_nsa_slc_kv, state_nsa_win_kv, cache_dsa_kv, cache_dsa_idx_k, state_conv, page_table, norm_mix_pre, norm_mix_post, norm_ffn_pre, norm_ffn_post, w_in, cmp_w1, cmp_w2, cmp_pe, conv_w, w_branch, w_out, ffn_w_gate_up, ffn_w_down):
    n_p, s, d = x_prompt.shape
    n_s = x_sample.shape[0]
    depth = w_in.shape[0]
    n_pages = page_table.shape[1]
    n_pool = cache_nsa_cmp_kv.shape[1]
    p0 = n_pages * PAGE_SIZE
    l_s = p0 + 1
    l_pad = p0 + LANE
    n_slc_s = -(-l_s // SEL_BLOCK)
    n_slc_pad = -(-n_slc_s // LANE) * LANE
    nc_s = p0 // CMP_STRIDE

    tabs128_p, half128 = _rope_tables(jnp.arange(s, dtype=jnp.int32), HEAD_DIM)
    tabs64_p, half64 = _rope_tables(jnp.arange(s, dtype=jnp.int32), IDX_DIM)
    tabs128_s, _ = _rope_tables(jnp.full((n_s,), p0, jnp.int32), HEAD_DIM)
    tabs64_s, _ = _rope_tables(jnp.full((n_s,), p0, jnp.int32), IDX_DIM)
    key_block = np.arange(l_pad) // SEL_BLOCK
    expand = ((key_block[None, :] == np.arange(n_slc_pad)[:, None]) & (np.arange(l_pad)[None, :] <= p0))
    consts = {
        "p0": p0,
        "n_slc_s": n_slc_s,
        "state_win": state_nsa_win_kv,
        "rope_p": (tabs128_p, half128, tabs64_p, half64),
        "rope_s": (tabs128_s, half128, tabs64_s, half64),
        "ov_p_t": jnp.asarray(_overlap(s // CMP_STRIDE, -(-s // SEL_BLOCK), s).T),
        "ov_s": jnp.asarray(_overlap(nc_s, n_slc_pad, l_s) * (np.arange(n_slc_pad) < n_slc_s)[None, :]),
        "expand": jnp.asarray(expand.astype(np.float32)).astype(BF16),
    }
    paged = lambda c: c.reshape(depth, n_pool, PAGE_SIZE * KV_SLOTS, HEAD_DIM)
    cmp_chunks = cache_nsa_cmp_kv.reshape(depth, n_pool, PAGE_SIZE // CMP_STRIDE, CHUNK_ROWS, HEAD_DIM)
    caches = (cmp_chunks, paged(cache_nsa_slc_kv),
              state_nsa_win_kv.reshape(depth, n_s, -1, HEAD_DIM), paged(cache_dsa_kv),
              jnp.swapaxes(cache_dsa_idx_k, 2, 3), state_conv)

    x_p = x_prompt.reshape(n_p * s, d)
    x_s = x_sample.reshape(n_s, d)
    xn_p = rmsnorm(x_p, norm_mix_pre[0])
    xn_s = rmsnorm(x_s, norm_mix_pre[0])
    new_p, new_s = [], []
    for l in range(depth):
        lw = {
            "w_in": _split_w_in(w_in[l]),
            "cmp": _cmp_weights(cmp_w1[l], cmp_w2[l], cmp_pe[l]),
            "conv_w": conv_w[l],
            "w_branch": w_branch[l].astype(BF16),
            "w_out": w_out[l].astype(BF16),
            "w_gu": ffn_w_gate_up[l].astype(BF16),
            "w_down": ffn_w_down[l].astype(BF16),
            "g_mix_post": norm_mix_post[l], "g_ffn_pre": norm_ffn_pre[l], "g_ffn_post": norm_ffn_post[l],
        }
        g_next = norm_mix_pre[l + 1] if l + 1 < depth else norm_mix_pre[l]
        x_p, xn_p, st_p = _prompt_layer(x_p, xn_p, lw, n_p, s, consts, g_next)
        x_s, xn_s, st_s = _sample_layer(x_s, xn_s, lw, l, caches, page_table, consts, g_next)
        new_p.append(st_p)
        new_s.append(st_s)
    p_out = [jnp.stack([st[i] for st in new_p]) for i in range(6)]
    s_out = [jnp.stack([st[i] for st in new_s]) for i in range(6)]
    return (x_p.reshape(n_p, s, d), x_s.reshape(n_s, 1, d), *p_out, *s_out)
```

```python
import functools
import math

import numpy as np
import jax
import jax.numpy as jnp
from jax import lax
from jax.experimental import pallas as pl
from jax.experimental.pallas import tpu as pltpu

D_MODEL = 2048
HEAD_DIM = 128
BRANCH_W = D_MODEL // 2
N_BRANCH = 3
NSA_HEADS = BRANCH_W // HEAD_DIM
NSA_KV = 2
NSA_REP = NSA_HEADS // NSA_KV
CMP_STRIDE = 16
CMP_LEN = 2 * CMP_STRIDE
SEL_BLOCK = 64
N_SEL = 16
WINDOW = 512
DSA_HEADS = BRANCH_W // HEAD_DIM
DSA_KV = 2
IDX_HEADS = 8
IDX_DIM = 64
DSA_TOPK = 256
CONV_DIM = BRANCH_W
CONV_WIDTH = 3
FFN_HIDDEN = ((8 * D_MODEL + 3 * 256 - 1) // (3 * 256)) * 256
ROPE_THETA = 500000.0
ROPE_FRACTION = 4
RMS_EPS = 1e-6
ATTN_SCALE = HEAD_DIM ** -0.5
IDX_SCALE = IDX_DIM ** -0.5
NEG_INF = -1e30
FORCE_SCORE = 1e30
BELOW_ALL = -3.0e38
PAGE_SIZE = 128

LANE = 128
KV_W = 2 * NSA_KV * HEAD_DIM
CMP_FEAT = CMP_STRIDE * KV_W
VMEM_LIMIT = 60 * 1024 * 1024
INT_MIN = -(2 ** 31)

F32 = jnp.float32
BF16 = jnp.bfloat16


def _cp(sem, vmem=VMEM_LIMIT):
    return pltpu.CompilerParams(dimension_semantics=sem, vmem_limit_bytes=vmem)


def _pick(n, pref, mult=8):
    if n <= pref:
        return n
    for b in range(pref, 0, -1):
        if n % b == 0 and b % mult == 0:
            return b
    return n


def _sigmoid(x):
    return 1.0 / (1.0 + jnp.exp(-x))


def _rms(x, g):
    return x * lax.rsqrt(jnp.mean(x * x, axis=-1, keepdims=True) + RMS_EPS) * g


def _rmsnorm_kernel(x_ref, g_ref, o_ref):
    o_ref[...] = _rms(x_ref[...], g_ref[...]).astype(o_ref.dtype)


def rmsnorm(x, g, out_dtype=BF16):
    t, d = x.shape
    bm = _pick(t, 512)
    return pl.pallas_call(
        _rmsnorm_kernel,
        grid=(t // bm,),
        in_specs=[pl.BlockSpec((bm, d), lambda i: (i, 0)), pl.BlockSpec((1, d), lambda i: (0, 0))],
        out_specs=pl.BlockSpec((bm, d), lambda i: (i, 0)),
        out_shape=jax.ShapeDtypeStruct((t, d), out_dtype),
        compiler_params=_cp(("parallel",)),
        name="rmsnorm",
    )(x, g.reshape(1, d))


def _mm_rope_kernel(x_ref, w_ref, c_ref, s1_ref, s2_ref, o_ref, *, half, rope_blocks, interleave):
    y = jnp.dot(x_ref[...], w_ref[...], preferred_element_type=F32)
    bm = y.shape[0]
    nh = len(rope_blocks)
    c, s1, s2 = c_ref[...], s1_ref[...], s2_ref[...]
    for h, roped in enumerate(rope_blocks):
        yh = y[:, h * LANE:(h + 1) * LANE]
        if roped:
            yh = yh * c + pltpu.roll(yh, LANE - half, 1) * s1 + pltpu.roll(yh, half, 1) * s2
        if interleave:
            o_ref[pl.ds(h, bm, stride=nh), :] = yh
        else:
            o_ref[:, h * LANE:(h + 1) * LANE] = yh


def proj_rope(xn, w, tabs, *, bn, half, rope_blocks, stacked):
    t, k = xn.shape
    n = w.shape[1]
    tab_rows = tabs[0].shape[0]
    bm = _pick(math.gcd(t, tab_rows), 1024)
    tab_blocks = tab_rows // bm
    kern = functools.partial(_mm_rope_kernel, half=half, rope_blocks=rope_blocks, interleave=stacked)
    tab_spec = pl.BlockSpec((bm, LANE), lambda i, j: (i % tab_blocks, 0))
    if stacked:
        nh = bn // LANE
        out_shape = jax.ShapeDtypeStruct((n // bn, t * nh, LANE), F32)
        out_spec = pl.BlockSpec((None, bm * nh, LANE), lambda i, j: (j, i, 0))
    else:
        out_shape = jax.ShapeDtypeStruct((t, n), F32)
        out_spec = pl.BlockSpec((bm, bn), lambda i, j: (i, j))
    return pl.pallas_call(
        kern,
        grid=(t // bm, n // bn),
        in_specs=[pl.BlockSpec((bm, k), lambda i, j: (i, 0)), pl.BlockSpec((k, bn), lambda i, j: (0, j)),
                  tab_spec, tab_spec, tab_spec],
        out_specs=out_spec,
        out_shape=out_shape,
        compiler_params=_cp(("parallel", "arbitrary")),
        name="proj_rope",
    )(xn, w, *tabs)


def _mm_rope_kv_kernel(x_ref, w_ref, c_ref, s1_ref, s2_ref, *refs, half, n_kinds):
    outs = refs[n_kinds:]
    j = pl.program_id(1)
    y = jnp.dot(x_ref[...], w_ref[...], preferred_element_type=F32)
    bm = y.shape[0]
    c, s1, s2 = c_ref[...], s1_ref[...], s2_ref[...]
    slabs = []
    for h in range(KV_SLOTS):
        yh = y[:, h * LANE:(h + 1) * LANE]
        if h < NSA_KV:
            yh = yh * c + pltpu.roll(yh, LANE - half, 1) * s1 + pltpu.roll(yh, half, 1) * s2
        slabs.append(yh)
    for kind in range(n_kinds):
        @pl.when(j == kind)
        def _(kind=kind):
            for h, yh in enumerate(slabs):
                outs[kind][pl.ds(h, bm, stride=KV_SLOTS), :] = yh


def proj_rope_kv(xn, w, tabs, kv_all, layer, *, half):
    t, k = xn.shape
    n_kinds = len(kv_all)
    tab_rows = tabs[0].shape[0]
    bm = _pick(math.gcd(t, tab_rows), 1024)
    tab_blocks = tab_rows // bm
    tab_spec = pl.BlockSpec((bm, LANE), lambda i, j: (i % tab_blocks, 0))
    out_spec = pl.BlockSpec((None, bm * KV_SLOTS, LANE), lambda i, j: (layer, i, 0))
    return pl.pallas_call(
        functools.partial(_mm_rope_kv_kernel, half=half, n_kinds=n_kinds),
        grid=(t // bm, n_kinds),
        in_specs=[pl.BlockSpec((bm, k), lambda i, j: (i, 0)), pl.BlockSpec((k, KV_W), lambda i, j: (0, j)),
                  tab_spec, tab_spec, tab_spec] + [pl.BlockSpec(memory_space=pl.ANY)] * n_kinds,
        out_specs=[out_spec] * n_kinds,
        out_shape=[jax.ShapeDtypeStruct(a.shape, a.dtype) for a in kv_all],
        input_output_aliases={5 + i: i for i in range(n_kinds)},
        compiler_params=_cp(("parallel", "arbitrary")),
        name="proj_rope_kv",
    )(xn, w, *tabs, *kv_all)


def _mm_kernel(x_ref, w_ref, o_ref):
    o_ref[...] = jnp.dot(x_ref[...], w_ref[...], preferred_element_type=F32).astype(o_ref.dtype)


def matmul(x, w, *, bn, out_dtype=F32):
    t, k = x.shape
    n = w.shape[1]
    bm = _pick(t, 1024)
    return pl.pallas_call(
        _mm_kernel,
        grid=(t // bm, n // bn),
        in_specs=[pl.BlockSpec((bm, k), lambda i, j: (i, 0)), pl.BlockSpec((k, bn), lambda i, j: (0, j))],
        out_specs=pl.BlockSpec((bm, bn), lambda i, j: (i, j)),
        out_shape=jax.ShapeDtypeStruct((t, n), out_dtype),
        compiler_params=_cp(("parallel", "arbitrary")),
        name="proj_plain",
    )(x, w)


def _merge_kernel(a1_ref, a2_ref, a3_ref, ob_ref, oc_ref, w_ref, g0_ref, g1_ref, g2_ref, o_ref):
    xa = (a1_ref[...] + a2_ref[...] + a3_ref[...]).astype(BF16)
    acc = _sigmoid(g0_ref[...]) * jnp.dot(xa, w_ref[0], preferred_element_type=F32)
    acc += _sigmoid(g1_ref[...]) * jnp.dot(ob_ref[...].astype(BF16), w_ref[1], preferred_element_type=F32)
    acc += _sigmoid(g2_ref[...]) * jnp.dot(oc_ref[...], w_ref[2], preferred_element_type=F32)
    o_ref[...] = acc.astype(o_ref.dtype)


def _resident(shape, index_map):
    return pl.BlockSpec(shape, index_map, pipeline_mode=pl.Buffered(1))


def merge_branches(a1, a2, a3, ob, oc, wb, layer, cg):
    t = a1.shape[0]
    bm = _pick(t, 256)
    bn = D_MODEL
    nj = D_MODEL // bn
    xs = pl.BlockSpec((bm, BRANCH_W), lambda i, j: (i, 0))

    def gspec(br):
        return pl.BlockSpec((bm, bn), lambda i, j: (i, br * nj + j))

    return pl.pallas_call(
        _merge_kernel,
        grid=(t // bm, nj),
        in_specs=[xs, xs, xs, xs, xs,
                  _resident((None, N_BRANCH, BRANCH_W, bn), lambda i, j: (layer, 0, 0, j)),
                  gspec(0), gspec(1), gspec(2)],
        out_specs=pl.BlockSpec((bm, bn), lambda i, j: (i, j)),
        out_shape=jax.ShapeDtypeStruct((t, D_MODEL), BF16),
        compiler_params=_cp(("parallel", "arbitrary")),
        name="merge",
    )(a1, a2, a3, ob, oc, wb, cg, cg, cg)


def _outproj_kernel(m_ref, w_ref, x_ref, gpost_ref, gpre_ref, xo_ref, hn_ref):
    y = jnp.dot(m_ref[...], w_ref[...], preferred_element_type=F32)
    xn = x_ref[...] + _rms(y, gpost_ref[...])
    xo_ref[...] = xn
    hn_ref[...] = _rms(xn, gpre_ref[...]).astype(hn_ref.dtype)


def outproj_residual(merged, w_out, layer, x, g_post, g_ffn_pre):
    t, d = x.shape
    bm = _pick(t, 512)
    row = pl.BlockSpec((bm, d), lambda i: (i, 0))
    vec = pl.BlockSpec((1, d), lambda i: (0, 0))
    return pl.pallas_call(
        _outproj_kernel,
        grid=(t // bm,),
        in_specs=[row, _resident((None, d, d), lambda i: (layer, 0, 0)), row, vec, vec],
        out_specs=[row, row],
        out_shape=[jax.ShapeDtypeStruct((t, d), F32), jax.ShapeDtypeStruct((t, d), BF16)],
        compiler_params=_cp(("parallel",)),
        name="outproj",
    )(merged, w_out, x, g_post.reshape(1, d), g_ffn_pre.reshape(1, d))


def _ffn_gu_kernel(h_ref, wg_ref, wu_ref, o_ref):
    h = h_ref[...]
    g = jnp.dot(h, wg_ref[...], preferred_element_type=F32)
    u = jnp.dot(h, wu_ref[...], preferred_element_type=F32)
    o_ref[...] = (g * _sigmoid(g) * u).astype(o_ref.dtype)


def ffn_gate_up(hn, w_gu, layer):
    t, d = hn.shape
    bm = _pick(t, 1024)
    bn = 512
    nj = FFN_HIDDEN // bn
    return pl.pallas_call(
        _ffn_gu_kernel,
        grid=(t // bm, nj),
        in_specs=[pl.BlockSpec((bm, d), lambda i, j: (i, 0)),
                  pl.BlockSpec((None, d, bn), lambda i, j: (layer, 0, j)),
                  pl.BlockSpec((None, d, bn), lambda i, j: (layer, 0, nj + j))],
        out_specs=pl.BlockSpec((bm, bn), lambda i, j: (i, j)),
        out_shape=jax.ShapeDtypeStruct((t, FFN_HIDDEN), BF16),
        compiler_params=_cp(("parallel", "arbitrary")),
        name="ffn_gate_up",
    )(hn, w_gu, w_gu)


def _ffn_down_kernel(a_ref, w_ref, x_ref, gpost_ref, gnext_ref, y_ref, xn_ref):
    y = x_ref[...] + _rms(jnp.dot(a_ref[...], w_ref[...], preferred_element_type=F32), gpost_ref[...])
    y_ref[...] = y
    xn_ref[...] = _rms(y, gnext_ref[...]).astype(xn_ref.dtype)


def ffn_down_residual(act, w_down, layer, x, g_post, g_next):
    t, d = x.shape
    f = act.shape[1]
    bm = _pick(t, 256)
    row = pl.BlockSpec((bm, d), lambda i: (i, 0))
    vec = pl.BlockSpec((1, d), lambda i: (0, 0))
    return pl.pallas_call(
        _ffn_down_kernel,
        grid=(t // bm,),
        in_specs=[pl.BlockSpec((bm, f), lambda i: (i, 0)), _resident((None, f, d), lambda i: (layer, 0, 0)),
                  row, vec, vec],
        out_specs=[row, row],
        out_shape=[jax.ShapeDtypeStruct((t, d), F32), jax.ShapeDtypeStruct((t, d), BF16)],
        compiler_params=_cp(("parallel",)),
        name="ffn_down",
    )(act, w_down, x, g_post.reshape(1, d), g_next.reshape(1, d))


def _conv_kernel(u_ref, b_ref, c_ref, w_ref, o_ref, st_ref):
    v = c_ref[...] * u_ref[...]
    s = v.shape[0]
    row = lax.broadcasted_iota(jnp.int32, v.shape, 0)
    v1 = jnp.where(row >= 1, pltpu.roll(v, 1, 0), 0.0)
    v2 = jnp.where(row >= 2, pltpu.roll(v, 2, 0), 0.0)
    w = w_ref[...]
    y = w[0:1] * v2 + w[1:2] * v1 + w[2:3] * v
    o_ref[...] = (b_ref[...] * y).astype(o_ref.dtype)
    st_ref[...] = v[s - (CONV_WIDTH - 1):, :]


def conv_prompt(cg, conv_w, n, s):
    bc = 256
    nj = CONV_DIM // bc
    base = N_BRANCH * D_MODEL // bc
    return pl.pallas_call(
        _conv_kernel,
        grid=(n, nj),
        in_specs=[pl.BlockSpec((s, bc), lambda b, j: (b, base + j)),
                  pl.BlockSpec((s, bc), lambda b, j: (b, base + nj + j)),
                  pl.BlockSpec((s, bc), lambda b, j: (b, base + 2 * nj + j)),
                  pl.BlockSpec((CONV_WIDTH, bc), lambda b, j: (0, j))],
        out_specs=[pl.BlockSpec((s, bc), lambda b, j: (b, j)),
                   pl.BlockSpec((None, CONV_WIDTH - 1, bc), lambda b, j: (b, 0, j))],
        out_shape=[jax.ShapeDtypeStruct((n * s, CONV_DIM), BF16),
                   jax.ShapeDtypeStruct((n, CONV_WIDTH - 1, CONV_DIM), F32)],
        compiler_params=_cp(("parallel", "arbitrary")),
        name="conv_prompt",
    )(cg, cg, cg, conv_w)


def _conv_dec_kernel(u_ref, b_ref, c_ref, buf_ref, w_ref, o_ref, st_ref):
    v = c_ref[...] * u_ref[...]
    b0 = buf_ref[:, 0, :]
    b1 = buf_ref[:, 1, :]
    w = w_ref[...]
    y = w[0:1] * b0 + w[1:2] * b1 + w[2:3] * v
    o_ref[...] = (b_ref[...] * y).astype(o_ref.dtype)
    st_ref[:, 0, :] = b1
    st_ref[:, 1, :] = v


def conv_decode(cg, buf, conv_w):
    n = cg.shape[0]
    base = N_BRANCH * D_MODEL // CONV_DIM
    blk = lambda j: pl.BlockSpec((n, CONV_DIM), lambda i: (0, base + j))
    full3 = pl.BlockSpec((n, CONV_WIDTH - 1, CONV_DIM), lambda i: (0, 0, 0))
    return pl.pallas_call(
        _conv_dec_kernel,
        grid=(1,),
        in_specs=[blk(0), blk(1), blk(2), full3, pl.BlockSpec((CONV_WIDTH, CONV_DIM), lambda i: (0, 0))],
        out_specs=[pl.BlockSpec((n, CONV_DIM), lambda i: (0, 0)), full3],
        out_shape=[jax.ShapeDtypeStruct((n, CONV_DIM), BF16),
                   jax.ShapeDtypeStruct((n, CONV_WIDTH - 1, CONV_DIM), F32)],
        compiler_params=_cp(("arbitrary",)),
        name="conv_decode",
    )(cg, cg, cg, buf, conv_w)


def _compress(read_x, w1_ref, w2_ref, pe_ref, nc):
    out = []
    for kv in range(2):
        acc = jnp.zeros((NSA_KV * nc, 2 * HEAD_DIM), F32)
        bias = jnp.zeros((8, 2 * HEAD_DIM), F32)
        for tp in range(CMP_STRIDE // 2):
            xt = jnp.concatenate(
                [jnp.concatenate([read_x(2 * tp + u, kv * NSA_KV + g) for u in range(2)], axis=1)
                 for g in range(NSA_KV)], axis=0).astype(BF16)
            wt = w1_ref[kv, tp]
            acc += jnp.dot(xt, wt, preferred_element_type=F32)
            bias += jnp.dot(pe_ref[kv, tp], wt, preferred_element_type=F32)
        pe_bias = bias[0:1, :HEAD_DIM] + bias[1:2, HEAD_DIM:]
        per_group = []
        for g in range(NSA_KV):
            a = acc[g * nc:(g + 1) * nc, :HEAD_DIM]
            b = acc[g * nc:(g + 1) * nc, HEAD_DIM:]
            pre = a + pltpu.roll(b, nc - 1, 0) + pe_bias
            hid = pre * _sigmoid(pre)
            per_group.append(jnp.dot(hid.astype(BF16), w2_ref[kv], preferred_element_type=F32))
        out.append(per_group)
    return out[0], out[1]


def _gate_rows(gates_blk):
    return gates_blk.T


def _stack_heads(q_ref, g, bq):
    return jnp.concatenate(
        [q_ref[:, (g * NSA_REP + r) * HEAD_DIM:(g * NSA_REP + r + 1) * HEAD_DIM] for r in range(NSA_REP)],
        axis=0).astype(BF16)


KV_SLOTS = 2 * NSA_KV
CHUNK_ROWS = CMP_STRIDE * KV_SLOTS
CHUNK_PITCH = CHUNK_ROWS + 8


def _kv_rows(kv_ref, k0, n, slot):
    return kv_ref[pl.ds(k0 * KV_SLOTS + slot, n, stride=KV_SLOTS), :]


EXP2_SCALE = ATTN_SCALE * math.log2(math.e)
FLASH_KC = 256


def _flash_scratch(bq):
    nq = NSA_REP * bq
    return [pltpu.VMEM((NSA_KV, nq, HEAD_DIM), BF16), pltpu.VMEM((NSA_KV, 1, nq), F32),
            pltpu.VMEM((NSA_KV, 1, nq), F32), pltpu.VMEM((NSA_KV, HEAD_DIM, nq), F32),
            pltpu.VMEM((NSA_KV, FLASH_KC, nq), F32), pltpu.VMEM((NSA_KV, FLASH_KC, nq), F32)]


def _flash_t(st, q_ref, kv_ref, o_ref, lo, hi, kc, bias_fn, bq, gate_t, branch):
    qs_ref, m_ref, l_ref, acc_ref, s_even, s_odd = st
    for g in range(NSA_KV):
        qs_ref[g] = _stack_heads(q_ref, g, bq)
    m_ref[...] = jnp.full(m_ref.shape, NEG_INF, F32)
    l_ref[...] = jnp.zeros(l_ref.shape, F32)
    acc_ref[...] = jnp.zeros(acc_ref.shape, F32)

    def scores(c, s_ref):
        k0 = pl.multiple_of(c * kc, kc)
        for g in range(NSA_KV):
            kb = _kv_rows(kv_ref, k0, kc, g).astype(BF16)
            s = lax.dot_general(kb, qs_ref[g], (((1,), (1,)), ((), ())), preferred_element_type=F32)
            s_ref[g] = s + jnp.concatenate([bias_fn(k0, g)] * NSA_REP, axis=1)

    def step(c, s_cur, s_nxt):
        scores(jnp.minimum(c + 1, hi - 1), s_nxt)
        k0 = pl.multiple_of(c * kc, kc)
        for g in range(NSA_KV):
            vb = _kv_rows(kv_ref, k0, kc, NSA_KV + g).astype(BF16)
            s = s_cur[g]
            m = m_ref[g]
            m_new = jnp.maximum(m, jnp.max(s, axis=0, keepdims=True))
            alpha = jnp.exp2((m - m_new) * EXP2_SCALE)
            p = jnp.exp2((s - m_new) * EXP2_SCALE)
            m_ref[g] = m_new
            l_ref[g] = alpha * l_ref[g] + jnp.sum(p, axis=0, keepdims=True)
            pv = lax.dot_general(vb, p.astype(BF16), (((0,), (0,)), ((), ())), preferred_element_type=F32)
            acc_ref[g] = alpha * acc_ref[g] + pv

    scores(lo, s_even)
    n = hi - lo

    def pair(i, _):
        c = lo + 2 * i
        step(c, s_even, s_odd)
        step(c + 1, s_odd, s_even)
        return 0

    lax.fori_loop(0, n // 2, pair, 0)

    @pl.when(n % 2 == 1)
    def _():
        step(hi - 1, s_even, s_odd)
    for g in range(NSA_KV):
        ot = jnp.where(m_ref[g] > 0.5 * NEG_INF, acc_ref[g] / jnp.maximum(l_ref[g], 1e-30), 0.0)
        _store_heads(o_ref, [ot[:, r * bq:(r + 1) * bq] for r in range(NSA_REP)], g, gate_t, branch)


def _store_heads(o_ref, heads, g, gate_t, branch):
    for r, oh in enumerate(heads):
        h = g * NSA_REP + r
        if gate_t is not None:
            oh = oh * _sigmoid(gate_t[h * 3 + branch:h * 3 + branch + 1, :])
        o_ref[:, h * HEAD_DIM:(h + 1) * HEAD_DIM] = oh.T


def _cmp_prompt_kernel(x_ref, q_ref, gates_ref, w1_ref, w2_ref, pe_ref, ov_ref, o_ref, ps_ref, kc_ref, vc_ref,
                       *, nc, bq):
    qi = pl.program_id(1)

    @pl.when(qi == 0)
    def _():
        def read_x(t, kg):
            return x_ref[pl.ds(t * KV_SLOTS + kg, nc, stride=CMP_STRIDE * KV_SLOTS), :]
        k_c, v_c = _compress(read_x, w1_ref, w2_ref, pe_ref, nc)
        for g in range(NSA_KV):
            kc_ref[g] = k_c[g].astype(BF16)
            vc_ref[g] = v_c[g].astype(BF16)

    q0 = qi * bq
    nq = NSA_REP * bq
    gate_t = _gate_rows(gates_ref[...])
    pos = q0 + lax.broadcasted_iota(jnp.int32, (nc, bq), 1)
    blk_end = lax.broadcasted_iota(jnp.int32, (nc, bq), 0) * CMP_STRIDE + (CMP_LEN - 1)
    mk1 = blk_end <= pos
    mk = jnp.concatenate([mk1] * NSA_REP, axis=1)
    for g in range(NSA_KV):
        qs = _stack_heads(q_ref, g, bq)
        s = lax.dot_general(kc_ref[g], qs, (((1,), (1,)), ((), ())), preferred_element_type=F32) * ATTN_SCALE
        s = jnp.where(mk, s, NEG_INF)
        m = jnp.max(s, axis=0, keepdims=True)
        e = jnp.where(mk, jnp.exp(s - m), 0.0)
        p = e / jnp.maximum(jnp.sum(e, axis=0, keepdims=True), 1e-30)
        ot = lax.dot_general(vc_ref[g], p.astype(BF16), (((0,), (0,)), ((), ())), preferred_element_type=F32)
        _store_heads(o_ref, [ot[:, r * bq:(r + 1) * bq] for r in range(NSA_REP)], g, gate_t, 0)
        psum = p[:, 0:bq]
        for r in range(1, NSA_REP):
            psum = psum + p[:, r * bq:(r + 1) * bq]
        ps_ref[g] = jnp.dot(ov_ref[...], psum, precision=lax.Precision.HIGHEST, preferred_element_type=F32)


def cmp_prompt(cmp_x, layer, q, idxm, w1cat, w2, pe8, ov_t, n, s):
    nc = s // CMP_STRIDE
    bq = 256
    nq = s // bq
    n_slc = ov_t.shape[0]
    t = n * s
    full = lambda a: pl.BlockSpec(a.shape, lambda b, i: (0,) * a.ndim)
    return pl.pallas_call(
        functools.partial(_cmp_prompt_kernel, nc=nc, bq=bq),
        grid=(n, nq),
        in_specs=[pl.BlockSpec((None, s * KV_SLOTS, HEAD_DIM), lambda b, i: (layer, b, 0)),
                  pl.BlockSpec((bq, BRANCH_W), lambda b, i: (b * nq + i, 0)),
                  pl.BlockSpec((bq, LANE), lambda b, i: (b * nq + i, 5)),
                  full(w1cat), full(w2), full(pe8), full(ov_t)],
        out_specs=[pl.BlockSpec((bq, BRANCH_W), lambda b, i: (b * nq + i, 0)),
                   pl.BlockSpec((NSA_KV, n_slc, bq), lambda b, i: (0, 0, b * nq + i))],
        out_shape=[jax.ShapeDtypeStruct((t, BRANCH_W), F32), jax.ShapeDtypeStruct((NSA_KV, n_slc, t), F32)],
        scratch_shapes=[pltpu.VMEM((NSA_KV, nc, HEAD_DIM), BF16), pltpu.VMEM((NSA_KV, nc, HEAD_DIM), BF16)],
        compiler_params=_cp(("arbitrary", "arbitrary")),
        name="nsa_cmp_prompt",
    )(cmp_x, q, idxm, w1cat, w2, pe8, ov_t)


def _topk_rank_rows(sc, n_rows, k):
    j = lax.broadcasted_iota(jnp.int32, sc.shape, 0)
    rank = jnp.zeros(sc.shape, jnp.int32)
    for i in range(n_rows):
        si = sc[i:i + 1, :]
        beats = jnp.where(si > sc, 1, jnp.where((si == sc) & (j > i), 1, 0))
        rank = rank + beats
    return jnp.where(rank < k, 1.0, 0.0)


def _slc_prompt_kernel(q_ref, kv_ref, ps_ref, gates_ref, o_ref, bias_ref, *st, n_slc, bq, kc):
    qi = pl.program_id(1)
    q0 = qi * bq
    gate_t = _gate_rows(gates_ref[...])
    blk = lax.broadcasted_iota(jnp.int32, (n_slc, bq), 0)
    pos = q0 + lax.broadcasted_iota(jnp.int32, (n_slc, bq), 1)
    cur = pos // SEL_BLOCK
    forced = (blk == 0) | (blk == cur) | (blk == cur - 1)
    visible = blk * SEL_BLOCK <= pos
    qpos = q0 + lax.broadcasted_iota(jnp.int32, (SEL_BLOCK, bq), 1)
    krow = lax.broadcasted_iota(jnp.int32, (SEL_BLOCK, bq), 0)
    for g in range(NSA_KV):
        sc = jnp.where(forced, FORCE_SCORE, jnp.where(visible, ps_ref[g], NEG_INF))
        sel = _topk_rank_rows(sc, n_slc, min(N_SEL, n_slc))
        for j in range(n_slc):
            keep = (jnp.broadcast_to(sel[j:j + 1, :], (SEL_BLOCK, bq)) > 0.5) & (j * SEL_BLOCK + krow <= qpos)
            bias_ref[g, j * SEL_BLOCK:(j + 1) * SEL_BLOCK, :] = jnp.where(keep, 0.0, NEG_INF)

    _flash_t(st, q_ref, kv_ref, o_ref, 0, (q0 + bq + kc - 1) // kc, kc,
             lambda k0, g: bias_ref[g, pl.ds(k0, kc), :], bq, gate_t, 1)


def slc_prompt(q, kv4, layer, p_slc, idxm, n, s):
    bq, kc = 256, FLASH_KC
    nq = s // bq
    n_slc = p_slc.shape[1]
    t = n * s
    return pl.pallas_call(
        functools.partial(_slc_prompt_kernel, n_slc=n_slc, bq=bq, kc=kc),
        grid=(n, nq),
        in_specs=[pl.BlockSpec((bq, BRANCH_W), lambda b, i: (b * nq + i, 0)),
                  pl.BlockSpec((None, s * KV_SLOTS, HEAD_DIM), lambda b, i: (layer, b, 0)),
                  pl.BlockSpec((NSA_KV, n_slc, bq), lambda b, i: (0, 0, b * nq + i)),
                  pl.BlockSpec((bq, LANE), lambda b, i: (b * nq + i, 5))],
        out_specs=pl.BlockSpec((bq, BRANCH_W), lambda b, i: (b * nq + i, 0)),
        out_shape=jax.ShapeDtypeStruct((t, BRANCH_W), F32),
        scratch_shapes=[pltpu.VMEM((NSA_KV, s, bq), F32)] + _flash_scratch(bq),
        compiler_params=_cp(("parallel", "arbitrary")),
        name="nsa_slc_prompt",
    )(q, kv4, p_slc, idxm)


def _win_prompt_kernel(q_ref, kv_ref, gates_ref, o_ref, *st, bq, kc):
    qi = pl.program_id(1)
    q0 = qi * bq
    gate_t = _gate_rows(gates_ref[...])
    qpos = q0 + lax.broadcasted_iota(jnp.int32, (kc, bq), 1)
    krow = lax.broadcasted_iota(jnp.int32, (kc, bq), 0)

    def bias_fn(k0, g):
        rel = qpos - (k0 + krow)
        return jnp.where((rel >= 0) & (rel <= WINDOW), 0.0, NEG_INF)

    lo = jnp.maximum(q0 - WINDOW, 0) // kc
    hi = (q0 + bq + kc - 1) // kc
    _flash_t(st, q_ref, kv_ref, o_ref, lo, hi, kc, bias_fn, bq, gate_t, 2)


def win_prompt(q, kv4, layer, idxm, n, s):
    bq, kc = 256, FLASH_KC
    nq = s // bq
    t = n * s
    return pl.pallas_call(
        functools.partial(_win_prompt_kernel, bq=bq, kc=kc),
        grid=(n, nq),
        in_specs=[pl.BlockSpec((bq, BRANCH_W), lambda b, i: (b * nq + i, 0)),
                  pl.BlockSpec((None, s * KV_SLOTS, HEAD_DIM), lambda b, i: (layer, b, 0)),
                  pl.BlockSpec((bq, LANE), lambda b, i: (b * nq + i, 5))],
        out_specs=pl.BlockSpec((bq, BRANCH_W), lambda b, i: (b * nq + i, 0)),
        out_shape=jax.ShapeDtypeStruct((t, BRANCH_W), F32),
        scratch_shapes=_flash_scratch(bq),
        compiler_params=_cp(("parallel", "arbitrary")),
        name="nsa_win_prompt",
    )(q, kv4, idxm)


def _order_key(x):
    b = pltpu.bitcast(x + 0.0, jnp.int32)
    return b ^ ((b >> 31) & jnp.int32(0x7FFFFFFF))


def _radix_kth(count_ge, k, shape):
    zero = jnp.zeros(shape, jnp.int32)
    base = jnp.where(count_ge(zero) >= k, zero, jnp.full(shape, INT_MIN, jnp.int32))

    def body(i, base):
        cand = base | jnp.left_shift(jnp.int32(1), 30 - i)
        return jnp.where(count_ge(cand) >= k, cand, base)

    return lax.fori_loop(0, 31, body, base)


def _tie_cut(count_eq_below, need, shape, n_bits):
    def body(i, m):
        cand = m | jnp.left_shift(jnp.int32(1), n_bits - 1 - i)
        return jnp.where(count_eq_below(cand) <= need, cand, m)

    return lax.fori_loop(0, n_bits, body, jnp.zeros(shape, jnp.int32))


def _dsa_prompt_kernel(q_ref, qi_ref, kidx_ref, kv_ref, gates_ref, o_ref, key_ref, bias_ref, *st,
                       bq, kc, kf, s_len):
    qi = pl.program_id(1)
    q0 = qi * bq
    n_chunks = (q0 + bq + kc - 1) // kc
    gate_t = _gate_rows(gates_ref[...])
    w_row = jnp.concatenate([gate_t[24 + h:25 + h, :] for h in range(IDX_HEADS)], axis=1) * (
        IDX_HEADS ** -0.5 * IDX_SCALE)
    qis = jnp.concatenate([qi_ref[:, h * IDX_DIM:(h + 1) * IDX_DIM] for h in range(IDX_HEADS)], axis=0).astype(BF16)
    qpos = q0 + lax.broadcasted_iota(jnp.int32, (kc, bq), 1)
    krow = lax.broadcasted_iota(jnp.int32, (kc, bq), 0)

    def score_body(c, _):
        k0 = pl.multiple_of(c * kc, kc)
        kb = kidx_ref[pl.ds(k0, kc), 0:IDX_DIM].astype(BF16)
        lg = lax.dot_general(kb, qis, (((1,), (1,)), ((), ())), preferred_element_type=F32)
        wl = jnp.maximum(lg, 0.0) * w_row
        sc = wl[:, 0:bq]
        for h in range(1, IDX_HEADS):
            sc = sc + wl[:, h * bq:(h + 1) * bq]
        sc = jnp.where(k0 + krow <= qpos, sc, NEG_INF)
        key_ref[pl.ds(k0, kc), :] = _order_key(sc)
        return 0

    lax.fori_loop(0, n_chunks, score_body, 0)

    def count(pred):
        def body(c, acc):
            k0 = pl.multiple_of(c * kc, kc)
            hit = jnp.where(pred(key_ref[pl.ds(k0, kc), :], k0 + krow), 1, 0)
            return acc + jnp.sum(hit.reshape(kc // 8, 8, bq), axis=0)
        acc = lax.fori_loop(0, n_chunks, body, jnp.zeros((8, bq), jnp.int32))
        return jnp.sum(acc, axis=0, keepdims=True)

    k_top = min(DSA_TOPK, s_len // 4)
    row1 = (1, bq)

    def select(_):
        thr = _radix_kth(lambda cand: count(lambda key, idx: key >= cand), k_top, row1)
        need = k_top - count(lambda key, idx: key > thr)
        n_eq = count(lambda key, idx: key == thr)
        n_bits = max(1, int(s_len).bit_length())
        cut = lax.cond(
            jnp.any(n_eq != need),
            lambda _: _tie_cut(lambda m: count(lambda key, idx: (key == thr) & (idx < m)), need, row1, n_bits),
            lambda _: jnp.full(row1, s_len, jnp.int32), 0)
        return thr, cut

    thr, cut = lax.cond(q0 + bq > k_top, select,
                        lambda _: (jnp.full(row1, INT_MIN, jnp.int32), jnp.full(row1, s_len, jnp.int32)), 0)

    def mask_body(c, _):
        k0 = pl.multiple_of(c * kc, kc)
        key = key_ref[pl.ds(k0, kc), :]
        idx = k0 + krow
        sel = ((key > thr) | ((key == thr) & (idx < cut))) & (idx <= qpos)
        bias_ref[pl.ds(k0, kc), :] = jnp.where(sel, 0.0, NEG_INF)
        return 0

    lax.fori_loop(0, n_chunks, mask_body, 0)

    _flash_t(st, q_ref, kv_ref, o_ref, 0, (q0 + bq + kf - 1) // kf, kf,
             lambda k0, g: bias_ref[pl.ds(k0, kf), :], bq, None, 0)


def dsa_prompt(q, kv4, layer, idxm, n, s):
    bq = kc = 256
    nq = s // bq
    t = n * s
    return pl.pallas_call(
        functools.partial(_dsa_prompt_kernel, bq=bq, kc=kc, kf=FLASH_KC, s_len=s),
        grid=(n, nq),
        in_specs=[pl.BlockSpec((bq, BRANCH_W), lambda b, i: (b * nq + i, 1)),
                  pl.BlockSpec((bq, IDX_HEADS * IDX_DIM), lambda b, i: (b * nq + i, 0)),
                  pl.BlockSpec((s, LANE), lambda b, i: (b, 4)),
                  pl.BlockSpec((None, s * KV_SLOTS, HEAD_DIM), lambda b, i: (layer, b, 0)),
                  pl.BlockSpec((bq, LANE), lambda b, i: (b * nq + i, 5))],
        out_specs=pl.BlockSpec((bq, BRANCH_W), lambda b, i: (b * nq + i, 0)),
        out_shape=jax.ShapeDtypeStruct((t, BRANCH_W), F32),
        scratch_shapes=[pltpu.VMEM((s, bq), jnp.int32), pltpu.VMEM((s, bq), F32)] + _flash_scratch(bq),
        compiler_params=_cp(("parallel", "arbitrary")),
        name="dsa_prompt",
    )(q, idxm, idxm, kv4, idxm)


def _page_copies(pt_ref, cache_ref, layer, buf_ref, sem_ref, seq, slot, n_pages, rows, mode):
    def each(fn):
        def body(p, _):
            page = pt_ref[seq, p]
            if mode == "chunks":
                for cc in range(rows):
                    dst = buf_ref.at[slot, pl.ds((p * rows + cc) * CHUNK_PITCH, CHUNK_ROWS)]
                    fn(pltpu.make_async_copy(cache_ref.at[layer, page, cc], dst, sem_ref.at[slot]))
                return 0
            win = pl.ds(p * rows, rows)
            dst = buf_ref.at[slot, :, win] if mode == "lanes" else buf_ref.at[slot, win]
            fn(pltpu.make_async_copy(cache_ref.at[layer, page], dst, sem_ref.at[slot]))
            return 0
        lax.fori_loop(0, n_pages, body, 0)
    return each


def _gather_step(pt_ref, cache_ref, layer, buf_ref, sem_ref, n_pages, rows, mode="rows"):
    b = pl.program_id(0)
    nb = pl.num_programs(0)
    slot = b % 2
    copies = functools.partial(_page_copies, pt_ref, cache_ref, layer, buf_ref, sem_ref,
                               n_pages=n_pages, rows=rows, mode=mode)

    @pl.when(b == 0)
    def _():
        copies(seq=0, slot=0)(lambda cp: cp.start())

    @pl.when(b + 1 < nb)
    def _():
        copies(seq=b + 1, slot=1 - slot)(lambda cp: cp.start())

    copies(seq=b, slot=slot)(lambda cp: cp.wait())
    return slot


def _head_column(row, offset, stride):
    lane = lax.broadcasted_iota(jnp.int32, (8, LANE), 1)
    h = lax.broadcasted_iota(jnp.int32, (8, LANE), 0)
    return jnp.sum(jnp.where(lane == offset + stride * h, jnp.broadcast_to(row, (8, LANE)), 0.0), axis=1, keepdims=True)


def _q8(q_ref, width):
    return jnp.concatenate([q_ref[0:1, h * width:(h + 1) * width] for h in range(8)], axis=0)


def _row_spec(width, blk, n_extra):
    if n_extra:
        return pl.BlockSpec((None, 1, width), lambda b, pt: (b, 0, blk))
    return pl.BlockSpec((None, 1, width), lambda b: (b, 0, blk))


def _cmp_dec_kernel(pt_ref, cache_ref, q_ref, gates_ref, w1_ref, w2_ref, pe_ref, ov_ref, o_ref, ps_ref,
                    buf_ref, sem_ref, *, layer, n_pages, p0):
    nc = n_pages * PAGE_SIZE // CMP_STRIDE
    slot = _gather_step(pt_ref, cache_ref, layer, buf_ref, sem_ref, n_pages, PAGE_SIZE // CMP_STRIDE,
                        mode="chunks")

    def read_x(t, kg):
        return buf_ref[slot, pl.ds(t * KV_SLOTS + kg, nc, stride=CHUNK_PITCH), :]

    k_c, v_c = _compress(read_x, w1_ref, w2_ref, pe_ref, nc)
    q8 = _q8(q_ref, HEAD_DIM).astype(BF16)
    head = lax.broadcasted_iota(jnp.int32, (8, 1), 0)
    blk_end = lax.broadcasted_iota(jnp.int32, (8, nc), 1) * CMP_STRIDE + (CMP_LEN - 1)
    mk = blk_end <= p0
    s = jnp.zeros((8, nc), F32)
    for g in range(NSA_KV):
        sg = lax.dot_general(q8, k_c[g].astype(BF16), (((1,), (1,)), ((), ())), preferred_element_type=F32)
        s = jnp.where(head // NSA_REP == g, sg, s)
    s = jnp.where(mk, s * ATTN_SCALE, NEG_INF)
    m = jnp.max(s, axis=1, keepdims=True)
    e = jnp.where(mk, jnp.exp(s - m), 0.0)
    p = e / jnp.maximum(jnp.sum(e, axis=1, keepdims=True), 1e-30)
    o = jnp.zeros((8, HEAD_DIM), F32)
    for g in range(NSA_KV):
        og = jnp.dot(p.astype(BF16), v_c[g].astype(BF16), preferred_element_type=F32)
        o = jnp.where(head // NSA_REP == g, og, o)
        psum = jnp.sum(jnp.where(head // NSA_REP == g, p, 0.0), axis=0, keepdims=True)
        ps8 = jnp.dot(jnp.broadcast_to(psum, (8, nc)), ov_ref[...], precision=lax.Precision.HIGHEST,
                      preferred_element_type=F32)
        ps_ref[g:g + 1, :] = ps8[0:1, :]
    gate = _sigmoid(_head_column(gates_ref[...], 0, 3))
    o_ref[...] = o * gate


def cmp_decode(page_table, cache_x, layer, q, idxm, w1cat, w2, pe8, ov, p0):
    n, n_pages = page_table.shape
    n_slc_pad = ov.shape[1]
    full = lambda a: pl.BlockSpec(a.shape, lambda b, pt: (0,) * a.ndim)
    gs = pltpu.PrefetchScalarGridSpec(
        num_scalar_prefetch=1,
        grid=(n,),
        in_specs=[pl.BlockSpec(memory_space=pl.ANY), _row_spec(BRANCH_W, 0, 1), _row_spec(LANE, 5, 1),
                  full(w1cat), full(w2), full(pe8), full(ov)],
        out_specs=[pl.BlockSpec((None, NSA_HEADS, HEAD_DIM), lambda b, pt: (b, 0, 0)),
                   pl.BlockSpec((None, NSA_KV, n_slc_pad), lambda b, pt: (b, 0, 0))],
        scratch_shapes=[pltpu.VMEM((2, n_pages * PAGE_SIZE // CMP_STRIDE * CHUNK_PITCH, HEAD_DIM), F32),
                        pltpu.SemaphoreType.DMA((2,))],
    )
    return pl.pallas_call(
        functools.partial(_cmp_dec_kernel, layer=layer, n_pages=n_pages, p0=p0),
        grid_spec=gs,
        out_shape=[jax.ShapeDtypeStruct((n, NSA_HEADS, HEAD_DIM), F32),
                   jax.ShapeDtypeStruct((n, NSA_KV, n_slc_pad), F32)],
        compiler_params=_cp(("arbitrary",)),
        name="nsa_cmp_decode",
    )(page_table, cache_x, q, idxm, w1cat, w2, pe8, ov)


def _slc_mask_kernel(ps_ref, e_ref, o_ref, *, n_slc, p0):
    sc = ps_ref[...]
    j = lax.broadcasted_iota(jnp.int32, sc.shape, 1)
    cur = p0 // SEL_BLOCK
    forced = (j == 0) | (j == cur) | (j == cur - 1)
    visible = j * SEL_BLOCK <= p0
    sc = jnp.where(j >= n_slc, BELOW_ALL, jnp.where(forced, FORCE_SCORE, jnp.where(visible, sc, NEG_INF)))
    rank = jnp.zeros(sc.shape, jnp.int32)
    for i in range(n_slc):
        si = sc[:, i:i + 1]
        rank = rank + jnp.where(si > sc, 1, jnp.where((si == sc) & (j > i), 1, 0))
    sel = jnp.where((rank < min(N_SEL, n_slc)) & (j < n_slc), 1.0, 0.0).astype(BF16)
    o_ref[...] = jnp.dot(sel, e_ref[...], preferred_element_type=F32)


def slc_mask_decode(p_slc, expand, n_slc, p0):
    n, g, w = p_slc.shape
    l_pad = expand.shape[1]
    out = pl.pallas_call(
        functools.partial(_slc_mask_kernel, n_slc=n_slc, p0=p0),
        grid=(1,),
        in_specs=[pl.BlockSpec((n * g, w), lambda i: (0, 0)), pl.BlockSpec(expand.shape, lambda i: (0, 0))],
        out_specs=pl.BlockSpec((n * g, l_pad), lambda i: (0, 0)),
        out_shape=jax.ShapeDtypeStruct((n * g, l_pad), F32),
        compiler_params=_cp(("arbitrary",)),
        name="nsa_slc_mask_decode",
    )(p_slc.reshape(n * g, w), expand)
    return out.reshape(n, g, l_pad)


def _idx_score_kernel(pt_ref, cache_ref, qi_ref, knew_ref, gates_ref, o_ref, buf_ref, sem_ref,
                      *, layer, n_pages, p0):
    past = n_pages * PAGE_SIZE
    l_pad = past + LANE
    slot = _gather_step(pt_ref, cache_ref, layer, buf_ref, sem_ref, n_pages, PAGE_SIZE, mode="lanes")
    d = lax.broadcasted_iota(jnp.int32, (IDX_DIM, LANE), 0)
    lane = lax.broadcasted_iota(jnp.int32, (IDX_DIM, LANE), 1)
    k_row = jnp.broadcast_to(knew_ref[...], (IDX_DIM, LANE))
    k_col = jnp.sum(jnp.where(lane == d, k_row, 0.0), axis=1, keepdims=True)
    buf_ref[slot, :, past:l_pad] = jnp.where(lane == 0, k_col, 0.0)
    q8 = _q8(qi_ref, IDX_DIM).astype(BF16)
    lg = jnp.dot(q8, buf_ref[slot].astype(BF16), preferred_element_type=F32) * IDX_SCALE
    w_col = _head_column(gates_ref[...], 24, 1) * (IDX_HEADS ** -0.5)
    sc = jnp.sum(jnp.maximum(lg, 0.0) * w_col, axis=0, keepdims=True)
    key = lax.broadcasted_iota(jnp.int32, (1, l_pad), 1)
    o_ref[...] = jnp.where(key <= p0, sc, BELOW_ALL)


def idx_score_decode(page_table, cache_idx, layer, idxm, p0):
    n, n_pages = page_table.shape
    l_pad = n_pages * PAGE_SIZE + LANE
    gs = pltpu.PrefetchScalarGridSpec(
        num_scalar_prefetch=1,
        grid=(n,),
        in_specs=[pl.BlockSpec(memory_space=pl.ANY), _row_spec(IDX_HEADS * IDX_DIM, 0, 1),
                  _row_spec(LANE, 4, 1), _row_spec(LANE, 5, 1)],
        out_specs=pl.BlockSpec((None, 1, l_pad), lambda b, pt: (b, 0, 0)),
        scratch_shapes=[pltpu.VMEM((2, IDX_DIM, l_pad), F32), pltpu.SemaphoreType.DMA((2,))],
    )
    return pl.pallas_call(
        functools.partial(_idx_score_kernel, layer=layer, n_pages=n_pages, p0=p0),
        grid_spec=gs,
        out_shape=jax.ShapeDtypeStruct((n, 1, l_pad), F32),
        compiler_params=_cp(("arbitrary",)),
        name="dsa_idx_score_decode",
    )(page_table, cache_idx, idxm, idxm, idxm)


def _dsa_mask_kernel(sc_ref, o_ref, *, k_top, p0):
    key = _order_key(sc_ref[...])
    n, l_pad = key.shape
    idx = lax.broadcasted_iota(jnp.int32, key.shape, 1)
    col = (n, 1)

    def count(pred):
        return jnp.sum(jnp.where(pred, 1, 0), axis=1, keepdims=True)

    thr = _radix_kth(lambda cand: count(key >= cand), k_top, col)
    need = k_top - count(key > thr)
    cut = _tie_cut(lambda m: count((key == thr) & (idx < m)), need, col, max(1, int(l_pad).bit_length()))
    sel = ((key > thr) | ((key == thr) & (idx < cut))) & (idx <= p0)
    mask = jnp.where(sel, 1.0, 0.0)
    for g in range(DSA_KV):
        o_ref[:, g, :] = mask


def dsa_mask_decode(score, k_top, p0):
    n, _, l_pad = score.shape
    return pl.pallas_call(
        functools.partial(_dsa_mask_kernel, k_top=k_top, p0=p0),
        grid=(1,),
        in_specs=[pl.BlockSpec((n, l_pad), lambda i: (0, 0))],
        out_specs=pl.BlockSpec((n, DSA_KV, l_pad), lambda i: (0, 0, 0)),
        out_shape=jax.ShapeDtypeStruct((n, DSA_KV, l_pad), F32),
        compiler_params=_cp(("arbitrary",)),
        name="dsa_mask_decode",
    )(score.reshape(n, l_pad))


def _attend_rows(q8, kv_rows, mask_ref, n_keys):
    head = lax.broadcasted_iota(jnp.int32, (8, 1), 0)
    o = jnp.zeros((8, HEAD_DIM), F32)
    for g in range(NSA_KV):
        kb = _kv_rows(kv_rows, 0, n_keys, g).astype(BF16)
        vb = _kv_rows(kv_rows, 0, n_keys, NSA_KV + g).astype(BF16)
        s = lax.dot_general(q8, kb, (((1,), (1,)), ((), ())), preferred_element_type=F32) * ATTN_SCALE
        mk = mask_ref[g:g + 1, :] > 0.5
        s = jnp.where(mk, s, NEG_INF)
        m = jnp.max(s, axis=1, keepdims=True)
        e = jnp.where(mk, jnp.exp(s - m), 0.0)
        p = e / jnp.maximum(jnp.sum(e, axis=1, keepdims=True), 1e-30)
        og = jnp.dot(p.astype(BF16), vb, preferred_element_type=F32)
        o = jnp.where(head // NSA_REP == g, og, o)
    return o


def _attn_paged_kernel(pt_ref, cache_ref, q_ref, kvnew_ref, mask_ref, gates_ref, o_ref, buf_ref, sem_ref,
                       *, layer, n_pages, branch):
    past = n_pages * PAGE_SIZE
    l_pad = past + LANE
    slot = _gather_step(pt_ref, cache_ref, layer, buf_ref, sem_ref, n_pages, PAGE_SIZE * KV_SLOTS)
    buf_ref[slot, past * KV_SLOTS:l_pad * KV_SLOTS, :] = jnp.zeros((LANE * KV_SLOTS, HEAD_DIM), F32)
    buf_ref[slot, past * KV_SLOTS:(past + 1) * KV_SLOTS, :] = kvnew_ref[...]
    o = _attend_rows(_q8(q_ref, HEAD_DIM).astype(BF16), buf_ref.at[slot], mask_ref, l_pad)
    if branch is not None:
        o = o * _sigmoid(_head_column(gates_ref[...], branch, 3))
    o_ref[...] = o


def attn_paged_decode(page_table, cache, layer, q, q_blk, kvnew, mask, idxm, branch):
    n, n_pages = page_table.shape
    l_pad = n_pages * PAGE_SIZE + LANE
    gs = pltpu.PrefetchScalarGridSpec(
        num_scalar_prefetch=1,
        grid=(n,),
        in_specs=[pl.BlockSpec(memory_space=pl.ANY), _row_spec(BRANCH_W, q_blk, 1),
                  pl.BlockSpec((None, KV_SLOTS, HEAD_DIM), lambda b, pt: (b, 0, 0)),
                  pl.BlockSpec((None, NSA_KV, l_pad), lambda b, pt: (b, 0, 0)), _row_spec(LANE, 5, 1)],
        out_specs=pl.BlockSpec((None, NSA_HEADS, HEAD_DIM), lambda b, pt: (b, 0, 0)),
        scratch_shapes=[pltpu.VMEM((2, l_pad * KV_SLOTS, HEAD_DIM), F32), pltpu.SemaphoreType.DMA((2,))],
    )
    return pl.pallas_call(
        functools.partial(_attn_paged_kernel, layer=layer, n_pages=n_pages, branch=branch),
        grid_spec=gs,
        out_shape=jax.ShapeDtypeStruct((n, NSA_HEADS, HEAD_DIM), F32),
        compiler_params=_cp(("arbitrary",)),
        name="attn_paged_decode",
    )(page_table, cache, q, kvnew, mask, idxm)


def _attn_win_kernel(st_ref, q_ref, kvnew_ref, gates_ref, o_ref, buf_ref, mask_ref, *, wb):
    l_pad = wb + LANE
    buf_ref[0:wb * KV_SLOTS, :] = st_ref[...]
    buf_ref[wb * KV_SLOTS:l_pad * KV_SLOTS, :] = jnp.zeros((LANE * KV_SLOTS, HEAD_DIM), F32)
    buf_ref[wb * KV_SLOTS:(wb + 1) * KV_SLOTS, :] = kvnew_ref[...]
    key = lax.broadcasted_iota(jnp.int32, (NSA_KV, l_pad), 1)
    mask_ref[...] = jnp.where(key <= wb, 1.0, 0.0)
    o = _attend_rows(_q8(q_ref, HEAD_DIM).astype(BF16), buf_ref, mask_ref, l_pad)
    o_ref[...] = o * _sigmoid(_head_column(gates_ref[...], 2, 3))


def attn_win_decode(state, layer, q, kvnew, idxm):
    n, wb = state.shape[1], state.shape[2] // KV_SLOTS
    l_pad = wb + LANE
    return pl.pallas_call(
        functools.partial(_attn_win_kernel, wb=wb),
        grid=(n,),
        in_specs=[pl.BlockSpec((None, None, wb * KV_SLOTS, HEAD_DIM), lambda b: (layer, b, 0, 0)),
                  _row_spec(BRANCH_W, 0, 0), pl.BlockSpec((None, KV_SLOTS, HEAD_DIM), lambda b: (b, 0, 0)),
                  _row_spec(LANE, 5, 0)],
        out_specs=pl.BlockSpec((None, NSA_HEADS, HEAD_DIM), lambda b: (b, 0, 0)),
        out_shape=jax.ShapeDtypeStruct((n, NSA_HEADS, HEAD_DIM), F32),
        scratch_shapes=[pltpu.VMEM((l_pad * KV_SLOTS, HEAD_DIM), F32), pltpu.VMEM((NSA_KV, l_pad), F32)],
        compiler_params=_cp(("parallel",)),
        name="attn_win_decode",
    )(state, q, kvnew, idxm)


def _rope_tables(pos, head_dim):
    d_rot = head_dim // ROPE_FRACTION
    half = d_rot // 2
    inv_freq = jnp.exp(jnp.arange(half, dtype=F32) * (-2.0 * math.log(ROPE_THETA) / d_rot))
    ang = pos.astype(F32)[:, None] * inv_freq[None, :]
    cos, sin = jnp.cos(ang), jnp.sin(ang)
    lane = np.arange(LANE) % head_dim
    j = lane % half
    first = jnp.asarray(lane < half)[None, :]
    second = jnp.asarray((lane >= half) & (lane < d_rot))[None, :]
    c = jnp.where(first | second, cos[:, j], 1.0)
    s1 = jnp.where(first, -sin[:, j], 0.0)
    s2 = jnp.where(second, sin[:, j], 0.0)
    return (c, s1, s2), half


def _split_w_in(w_in):
    sizes = (NSA_HEADS * HEAD_DIM, KV_W, KV_W, KV_W, 3 * NSA_HEADS, DSA_HEADS * HEAD_DIM, KV_W,
             IDX_HEADS * IDX_DIM, IDX_DIM, IDX_HEADS, CONV_DIM, CONV_DIM, CONV_DIM, N_BRANCH * D_MODEL)
    offs = np.concatenate([[0], np.cumsum(sizes)])
    col = lambda i: w_in[:, int(offs[i]):int(offs[i + 1])]
    (q_a, cmp_kv, slc_kv, win_kv, gate_a, q_b, dsa_kv, q_i, k_i, w_i, cu, cb, cc, gm) = [col(i) for i in range(14)]
    d = w_in.shape[0]
    zeros = lambda n: jnp.zeros((d, n), w_in.dtype)
    w_q = jnp.concatenate([q_a, q_b], axis=1).astype(BF16)
    w_kv = jnp.concatenate([cmp_kv, slc_kv, win_kv, dsa_kv], axis=1).astype(BF16)
    w_idx = jnp.concatenate([q_i, k_i, zeros(LANE - IDX_DIM), gate_a, w_i, zeros(LANE - 3 * NSA_HEADS - IDX_HEADS)],
                            axis=1).astype(BF16)
    w_cg = jnp.concatenate([gm, cu, cb, cc], axis=1).astype(BF16)
    return w_q, w_kv, w_idx, w_cg


def _cmp_weights(w1, w2, pe):
    half = CMP_STRIDE * HEAD_DIM
    pairs = CMP_STRIDE // 2
    wa = w1[:, :half].reshape(2, pairs, 2 * HEAD_DIM, HEAD_DIM)
    wb = w1[:, half:].reshape(2, pairs, 2 * HEAD_DIM, HEAD_DIM)
    w1cat = jnp.concatenate([wa, wb], axis=-1).astype(BF16)
    pe8 = jnp.zeros((2, pairs, 8, 2 * HEAD_DIM), F32)
    pe8 = pe8.at[:, :, 0, :].set(pe[:, :CMP_STRIDE].reshape(2, pairs, 2 * HEAD_DIM))
    pe8 = pe8.at[:, :, 1, :].set(pe[:, CMP_STRIDE:].reshape(2, pairs, 2 * HEAD_DIM))
    return w1cat, w2.astype(BF16), pe8.astype(BF16)


def _overlap(n_cmp_rows, n_slc, seq_len):
    n_cmp = seq_len // CMP_STRIDE - 1
    c = np.arange(n_cmp_rows)
    c_start = c * CMP_STRIDE
    s_start = np.arange(n_slc) * SEL_BLOCK
    ov = ((c_start[:, None] < s_start[None, :] + SEL_BLOCK) & (c_start[:, None] + CMP_LEN > s_start[None, :])
          & (c[:, None] < n_cmp))
    return ov.astype(np.float32)


def _project(xn, wts, tabs128, half128, tabs64, half64):
    w_q, _, w_idx, w_cg = wts
    q = proj_rope(xn, w_q, tabs128, bn=1024, half=half128, rope_blocks=(True,) * 8, stacked=False)
    idxm = proj_rope(xn, w_idx, tabs64, bn=w_idx.shape[1], half=half64,
                     rope_blocks=(True,) * 5 + (False,), stacked=False)
    cg = matmul(xn, w_cg, bn=1024)
    return q, idxm, cg


def _finish_layer(x, branches, oc, cg, lw, layer, g_next):
    a1, a2, a3, ob = branches
    merged = merge_branches(a1, a2, a3, ob, oc, lw["w_branch"], layer, cg)
    x_mid, hn = outproj_residual(merged, lw["w_out"], layer, x, lw["g_mix_post"], lw["g_ffn_pre"])
    act = ffn_gate_up(hn, lw["w_gu"], layer)
    return ffn_down_residual(act, lw["w_down"], layer, x_mid, lw["g_ffn_post"], g_next)


def _prompt_layer(x, xn, lw, layer, kv_all, n, s, consts, g_next):
    tabs128, half128 = consts["rope_p"][:2]
    q, idxm, cg = _project(xn, lw["w_in"], *consts["rope_p"])
    kv_all = proj_rope_kv(xn, lw["w_in"][1], tabs128, kv_all, layer, half=half128)
    a1, p_slc = cmp_prompt(kv_all[0], layer, q, idxm, *lw["cmp"], consts["ov_p_t"], n, s)
    a2 = slc_prompt(q, kv_all[1], layer, p_slc, idxm, n, s)
    a3 = win_prompt(q, kv_all[2], layer, idxm, n, s)
    ob = dsa_prompt(q, kv_all[3], layer, idxm, n, s)
    oc, conv_state = conv_prompt(cg, lw["conv_w"], n, s)
    y, xn_next = _finish_layer(x, (a1, a2, a3, ob), oc, cg, lw, layer, g_next)
    state = (idxm[:, IDX_HEADS * IDX_DIM:IDX_HEADS * IDX_DIM + IDX_DIM].reshape(n, s, IDX_DIM), conv_state)
    return y, xn_next, kv_all, state


def _sample_layer(x, xn, lw, layer, caches, page_table, consts, g_next):
    n = x.shape[0]
    p0 = consts["p0"]
    tabs128, half128 = consts["rope_s"][:2]
    q, idxm, cg = _project(xn, lw["w_in"], *consts["rope_s"])
    kv4 = proj_rope(xn, lw["w_in"][1], tabs128, bn=KV_W, half=half128,
                    rope_blocks=(True, True, False, False), stacked=True)
    cache_cmp_x, cache_slc, state_win, cache_dsa, cache_idx, state_conv = caches
    q3 = q.reshape(n, 1, -1)
    idx3 = idxm.reshape(n, 1, -1)
    new_row = lambda i: kv4[i].reshape(n, KV_SLOTS, HEAD_DIM)
    o_cmp, p_slc = cmp_decode(page_table, cache_cmp_x, layer, q3, idx3, *lw["cmp"], consts["ov_s"], p0)
    slc_mask = slc_mask_decode(p_slc, consts["expand"], consts["n_slc_s"], p0)
    o_slc = attn_paged_decode(page_table, cache_slc, layer, q3, 0, new_row(1), slc_mask, idx3, 1)
    o_win = attn_win_decode(state_win, layer, q3, new_row(2), idx3)
    score = idx_score_decode(page_table, cache_idx, layer, idx3, p0)
    dsa_mask = dsa_mask_decode(score, min(DSA_TOPK, (p0 + 1) // 4), p0)
    o_dsa = attn_paged_decode(page_table, cache_dsa, layer, q3, 1, new_row(3), dsa_mask, idx3, None)
    oc, conv_state = conv_decode(cg, state_conv[layer], lw["conv_w"])
    flat = lambda a: a.reshape(n, BRANCH_W)
    y, xn_next = _finish_layer(x, (flat(o_cmp), flat(o_slc), flat(o_win), flat(o_dsa)), oc, cg, lw, layer, g_next)
    kv5 = lambda a: a.reshape(n, 1, 2, NSA_KV, HEAD_DIM)
    win_all = jnp.concatenate([consts["state_win"][layer], kv5(kv4[2])], axis=1)
    keep = min(WINDOW, win_all.shape[1])
    state = (kv5(kv4[0]), kv5(kv4[1]), win_all[:, win_all.shape[1] - keep:], kv5(kv4[3]),
             idxm[:, IDX_HEADS * IDX_DIM:IDX_HEADS * IDX_DIM + IDX_DIM].reshape(n, 1, IDX_DIM), conv_state)
    return y, xn_next, state


def kernel(x_prompt, x_sample, cache_nsa_cmp_kv, cache_nsa_slc_kv, state_nsa_win_kv, cache_dsa_kv, cache_dsa_idx_k, state_conv, page_table, norm_mix_pre, norm_mix_post, norm_ffn_pre, norm_ffn_post, w_in, cmp_w1, cmp_w2, cmp_pe, conv_w, w_branch, w_out, ffn_w_gate_up, ffn_w_down):
    n_p, s, d = x_prompt.shape
    n_s = x_sample.shape[0]
    depth = w_in.shape[0]
    n_pages = page_table.shape[1]
    n_pool = cache_nsa_cmp_kv.shape[1]
    p0 = n_pages * PAGE_SIZE
    l_s = p0 + 1
    l_pad = p0 + LANE
    n_slc_s = -(-l_s // SEL_BLOCK)
    n_slc_pad = -(-n_slc_s // LANE) * LANE
    nc_s = p0 // CMP_STRIDE

    tabs128_p, half128 = _rope_tables(jnp.arange(s, dtype=jnp.int32), HEAD_DIM)
    tabs64_p, half64 = _rope_tables(jnp.arange(s, dtype=jnp.int32), IDX_DIM)
    tabs128_s, _ = _rope_tables(jnp.full((n_s,), p0, jnp.int32), HEAD_DIM)
    tabs64_s, _ = _rope_tables(jnp.full((n_s,), p0, jnp.int32), IDX_DIM)
    key_block = np.arange(l_pad) // SEL_BLOCK
    expand = ((key_block[None, :] == np.arange(n_slc_pad)[:, None]) & (np.arange(l_pad)[None, :] <= p0))
    consts = {
        "p0": p0,
        "n_slc_s": n_slc_s,
        "state_win": state_nsa_win_kv,
        "rope_p": (tabs128_p, half128, tabs64_p, half64),
        "rope_s": (tabs128_s, half128, tabs64_s, half64),
        "ov_p_t": jnp.asarray(_overlap(s // CMP_STRIDE, -(-s // SEL_BLOCK), s).T),
        "ov_s": jnp.asarray(_overlap(nc_s, n_slc_pad, l_s) * (np.arange(n_slc_pad) < n_slc_s)[None, :]),
        "expand": jnp.asarray(expand.astype(np.float32)).astype(BF16),
    }
    paged = lambda c: c.reshape(depth, n_pool, PAGE_SIZE * KV_SLOTS, HEAD_DIM)
    cmp_chunks = cache_nsa_cmp_kv.reshape(depth, n_pool, PAGE_SIZE // CMP_STRIDE, CHUNK_ROWS, HEAD_DIM)
    caches = (cmp_chunks, paged(cache_nsa_slc_kv),
              state_nsa_win_kv.reshape(depth, n_s, -1, HEAD_DIM), paged(cache_dsa_kv),
              jnp.swapaxes(cache_dsa_idx_k, 2, 3), state_conv)

    x_p = x_prompt.reshape(n_p * s, d)
    x_s = x_sample.reshape(n_s, d)
    xn_p = rmsnorm(x_p, norm_mix_pre[0])
    xn_s = rmsnorm(x_s, norm_mix_pre[0])
    stacked = {"w_branch": w_branch.astype(BF16), "w_out": w_out.astype(BF16),
               "w_gu": ffn_w_gate_up.astype(BF16), "w_down": ffn_w_down.astype(BF16)}
    kv_all = [jnp.zeros((depth, n_p * s * KV_SLOTS, HEAD_DIM), F32) for _ in range(4)]
    new_p, new_s = [], []
    for l in range(depth):
        lw = {
            "w_in": _split_w_in(w_in[l]),
            "cmp": _cmp_weights(cmp_w1[l], cmp_w2[l], cmp_pe[l]),
            "conv_w": conv_w[l],
            "g_mix_post": norm_mix_post[l], "g_ffn_pre": norm_ffn_pre[l], "g_ffn_post": norm_ffn_post[l],
            **stacked,
        }
        g_next = norm_mix_pre[l + 1] if l + 1 < depth else norm_mix_pre[l]
        x_p, xn_p, kv_all, st_p = _prompt_layer(x_p, xn_p, lw, l, kv_all, n_p, s, consts, g_next)
        x_s, xn_s, st_s = _sample_layer(x_s, xn_s, lw, l, caches, page_table, consts, g_next)
        new_p.append(st_p)
        new_s.append(st_s)
    kv6 = [a.reshape(depth, n_p, s, 2, NSA_KV, HEAD_DIM) for a in kv_all]
    keep = min(WINDOW, s)
    p_out = [kv6[0], kv6[1], kv6[2][:, :, s - keep:], kv6[3]] + [jnp.stack([st[i] for st in new_p]) for i in range(2)]
    s_out = [jnp.stack([st[i] for st in new_s]) for i in range(6)]
    return (x_p.reshape(n_p, s, d), x_s.reshape(n_s, 1, d), *p_out, *s_out)
```

```python
import functools
import math

import numpy as np
import jax
import jax.numpy as jnp
from jax import lax
from jax.experimental import pallas as pl
from jax.experimental.pallas import tpu as pltpu

D_MODEL = 2048
HEAD_DIM = 128
BRANCH_W = D_MODEL // 2
N_BRANCH = 3
NSA_HEADS = BRANCH_W // HEAD_DIM
NSA_KV = 2
NSA_REP = NSA_HEADS // NSA_KV
CMP_STRIDE = 16
CMP_LEN = 2 * CMP_STRIDE
SEL_BLOCK = 64
N_SEL = 16
WINDOW = 512
DSA_HEADS = BRANCH_W // HEAD_DIM
DSA_KV = 2
IDX_HEADS = 8
IDX_DIM = 64
DSA_TOPK = 256
CONV_DIM = BRANCH_W
CONV_WIDTH = 3
FFN_HIDDEN = ((8 * D_MODEL + 3 * 256 - 1) // (3 * 256)) * 256
ROPE_THETA = 500000.0
ROPE_FRACTION = 4
RMS_EPS = 1e-6
ATTN_SCALE = HEAD_DIM ** -0.5
IDX_SCALE = IDX_DIM ** -0.5
NEG_INF = -1e30
FORCE_SCORE = 1e30
BELOW_ALL = -3.0e38
PAGE_SIZE = 128

LANE = 128
KV_W = 2 * NSA_KV * HEAD_DIM
CMP_FEAT = CMP_STRIDE * KV_W
VMEM_LIMIT = 60 * 1024 * 1024
INT_MIN = -(2 ** 31)

F32 = jnp.float32
BF16 = jnp.bfloat16


def _cp(sem, vmem=VMEM_LIMIT):
    return pltpu.CompilerParams(dimension_semantics=sem, vmem_limit_bytes=vmem)


def _pick(n, pref, mult=8):
    if n <= pref:
        return n
    for b in range(pref, 0, -1):
        if n % b == 0 and b % mult == 0:
            return b
    return n


def _sigmoid(x):
    return 1.0 / (1.0 + jnp.exp(-x))


def _rms(x, g):
    return x * lax.rsqrt(jnp.mean(x * x, axis=-1, keepdims=True) + RMS_EPS) * g


def _rmsnorm_kernel(x_ref, g_ref, o_ref):
    o_ref[...] = _rms(x_ref[...], g_ref[...]).astype(o_ref.dtype)


def rmsnorm(x, g, out_dtype=BF16):
    t, d = x.shape
    bm = _pick(t, 512)
    return pl.pallas_call(
        _rmsnorm_kernel,
        grid=(t // bm,),
        in_specs=[pl.BlockSpec((bm, d), lambda i: (i, 0)), pl.BlockSpec((1, d), lambda i: (0, 0))],
        out_specs=pl.BlockSpec((bm, d), lambda i: (i, 0)),
        out_shape=jax.ShapeDtypeStruct((t, d), out_dtype),
        compiler_params=_cp(("parallel",)),
        name="rmsnorm",
    )(x, g.reshape(1, d))


def _mm_rope_kernel(x_ref, w_ref, c_ref, s1_ref, s2_ref, o_ref, *, half, rope_blocks, interleave):
    y = jnp.dot(x_ref[...], w_ref[...], preferred_element_type=F32)
    bm = y.shape[0]
    nh = len(rope_blocks)
    c, s1, s2 = c_ref[...], s1_ref[...], s2_ref[...]
    for h, roped in enumerate(rope_blocks):
        yh = y[:, h * LANE:(h + 1) * LANE]
        if roped:
            yh = yh * c + pltpu.roll(yh, LANE - half, 1) * s1 + pltpu.roll(yh, half, 1) * s2
        if interleave:
            o_ref[pl.ds(h, bm, stride=nh), :] = yh
        else:
            o_ref[:, h * LANE:(h + 1) * LANE] = yh


def proj_rope(xn, w, tabs, *, bn, half, rope_blocks, stacked):
    t, k = xn.shape
    n = w.shape[1]
    tab_rows = tabs[0].shape[0]
    bm = _pick(math.gcd(t, tab_rows), 1024)
    tab_blocks = tab_rows // bm
    kern = functools.partial(_mm_rope_kernel, half=half, rope_blocks=rope_blocks, interleave=stacked)
    tab_spec = pl.BlockSpec((bm, LANE), lambda i, j: (i % tab_blocks, 0))
    if stacked:
        nh = bn // LANE
        out_shape = jax.ShapeDtypeStruct((n // bn, t * nh, LANE), F32)
        out_spec = pl.BlockSpec((None, bm * nh, LANE), lambda i, j: (j, i, 0))
    else:
        out_shape = jax.ShapeDtypeStruct((t, n), F32)
        out_spec = pl.BlockSpec((bm, bn), lambda i, j: (i, j))
    return pl.pallas_call(
        kern,
        grid=(t // bm, n // bn),
        in_specs=[pl.BlockSpec((bm, k), lambda i, j: (i, 0)), pl.BlockSpec((k, bn), lambda i, j: (0, j)),
                  tab_spec, tab_spec, tab_spec],
        out_specs=out_spec,
        out_shape=out_shape,
        compiler_params=_cp(("parallel", "arbitrary")),
        name="proj_rope",
    )(xn, w, *tabs)


def _mm_rope_kv_kernel(x_ref, w_ref, c_ref, s1_ref, s2_ref, *refs, half, n_kinds):
    outs = refs[n_kinds:]
    j = pl.program_id(1)
    y = jnp.dot(x_ref[...], w_ref[...], preferred_element_type=F32)
    bm = y.shape[0]
    c, s1, s2 = c_ref[...], s1_ref[...], s2_ref[...]
    slabs = []
    for h in range(KV_SLOTS):
        yh = y[:, h * LANE:(h + 1) * LANE]
        if h < NSA_KV:
            yh = yh * c + pltpu.roll(yh, LANE - half, 1) * s1 + pltpu.roll(yh, half, 1) * s2
        slabs.append(yh)
    for kind in range(n_kinds):
        @pl.when(j == kind)
        def _(kind=kind):
            for h, yh in enumerate(slabs):
                outs[kind][pl.ds(h, bm, stride=KV_SLOTS), :] = yh


def proj_rope_kv(xn, w, tabs, kv_all, layer, *, half):
    t, k = xn.shape
    n_kinds = len(kv_all)
    tab_rows = tabs[0].shape[0]
    bm = _pick(math.gcd(t, tab_rows), 1024)
    tab_blocks = tab_rows // bm
    tab_spec = pl.BlockSpec((bm, LANE), lambda i, j: (i % tab_blocks, 0))
    out_spec = pl.BlockSpec((None, bm * KV_SLOTS, LANE), lambda i, j: (layer, i, 0))
    return pl.pallas_call(
        functools.partial(_mm_rope_kv_kernel, half=half, n_kinds=n_kinds),
        grid=(t // bm, n_kinds),
        in_specs=[pl.BlockSpec((bm, k), lambda i, j: (i, 0)), pl.BlockSpec((k, KV_W), lambda i, j: (0, j)),
                  tab_spec, tab_spec, tab_spec] + [pl.BlockSpec(memory_space=pl.ANY)] * n_kinds,
        out_specs=[out_spec] * n_kinds,
        out_shape=[jax.ShapeDtypeStruct(a.shape, a.dtype) for a in kv_all],
        input_output_aliases={5 + i: i for i in range(n_kinds)},
        compiler_params=_cp(("parallel", "arbitrary")),
        name="proj_rope_kv",
    )(xn, w, *tabs, *kv_all)


def _mm_kernel(x_ref, w_ref, o_ref):
    o_ref[...] = jnp.dot(x_ref[...], w_ref[...], preferred_element_type=F32).astype(o_ref.dtype)


def matmul(x, w, *, bn, out_dtype=F32):
    t, k = x.shape
    n = w.shape[1]
    bm = _pick(t, 1024)
    return pl.pallas_call(
        _mm_kernel,
        grid=(t // bm, n // bn),
        in_specs=[pl.BlockSpec((bm, k), lambda i, j: (i, 0)), pl.BlockSpec((k, bn), lambda i, j: (0, j))],
        out_specs=pl.BlockSpec((bm, bn), lambda i, j: (i, j)),
        out_shape=jax.ShapeDtypeStruct((t, n), out_dtype),
        compiler_params=_cp(("parallel", "arbitrary")),
        name="proj_plain",
    )(x, w)


def _merge_kernel(a1_ref, a2_ref, a3_ref, ob_ref, oc_ref, w_ref, g0_ref, g1_ref, g2_ref, o_ref):
    xa = (a1_ref[...] + a2_ref[...] + a3_ref[...]).astype(BF16)
    acc = _sigmoid(g0_ref[...]) * jnp.dot(xa, w_ref[0], preferred_element_type=F32)
    acc += _sigmoid(g1_ref[...]) * jnp.dot(ob_ref[...].astype(BF16), w_ref[1], preferred_element_type=F32)
    acc += _sigmoid(g2_ref[...]) * jnp.dot(oc_ref[...], w_ref[2], preferred_element_type=F32)
    o_ref[...] = acc.astype(o_ref.dtype)


def _resident(shape, index_map):
    return pl.BlockSpec(shape, index_map, pipeline_mode=pl.Buffered(1))


def merge_branches(a1, a2, a3, ob, oc, wb, layer, cg):
    t = a1.shape[0]
    bm = _pick(t, 256)
    bn = D_MODEL
    nj = D_MODEL // bn
    xs = pl.BlockSpec((bm, BRANCH_W), lambda i, j: (i, 0))

    def gspec(br):
        return pl.BlockSpec((bm, bn), lambda i, j: (i, br * nj + j))

    return pl.pallas_call(
        _merge_kernel,
        grid=(t // bm, nj),
        in_specs=[xs, xs, xs, xs, xs,
                  _resident((None, N_BRANCH, BRANCH_W, bn), lambda i, j: (layer, 0, 0, j)),
                  gspec(0), gspec(1), gspec(2)],
        out_specs=pl.BlockSpec((bm, bn), lambda i, j: (i, j)),
        out_shape=jax.ShapeDtypeStruct((t, D_MODEL), BF16),
        compiler_params=_cp(("parallel", "arbitrary")),
        name="merge",
    )(a1, a2, a3, ob, oc, wb, cg, cg, cg)


def _outproj_kernel(m_ref, w_ref, x_ref, gpost_ref, gpre_ref, xo_ref, hn_ref):
    y = jnp.dot(m_ref[...], w_ref[...], preferred_element_type=F32)
    xn = x_ref[...] + _rms(y, gpost_ref[...])
    xo_ref[...] = xn
    hn_ref[...] = _rms(xn, gpre_ref[...]).astype(hn_ref.dtype)


def outproj_residual(merged, w_out, layer, x, g_post, g_ffn_pre):
    t, d = x.shape
    bm = _pick(t, 512)
    row = pl.BlockSpec((bm, d), lambda i: (i, 0))
    vec = pl.BlockSpec((1, d), lambda i: (0, 0))
    return pl.pallas_call(
        _outproj_kernel,
        grid=(t // bm,),
        in_specs=[row, _resident((None, d, d), lambda i: (layer, 0, 0)), row, vec, vec],
        out_specs=[row, row],
        out_shape=[jax.ShapeDtypeStruct((t, d), F32), jax.ShapeDtypeStruct((t, d), BF16)],
        compiler_params=_cp(("parallel",)),
        name="outproj",
    )(merged, w_out, x, g_post.reshape(1, d), g_ffn_pre.reshape(1, d))


def _ffn_gu_kernel(h_ref, wg_ref, wu_ref, o_ref):
    h = h_ref[...]
    g = jnp.dot(h, wg_ref[...], preferred_element_type=F32)
    u = jnp.dot(h, wu_ref[...], preferred_element_type=F32)
    o_ref[...] = (g * _sigmoid(g) * u).astype(o_ref.dtype)


def ffn_gate_up(hn, w_gu, layer):
    t, d = hn.shape
    bm = _pick(t, 1024)
    bn = 512
    nj = FFN_HIDDEN // bn
    return pl.pallas_call(
        _ffn_gu_kernel,
        grid=(t // bm, nj),
        in_specs=[pl.BlockSpec((bm, d), lambda i, j: (i, 0)),
                  pl.BlockSpec((None, d, bn), lambda i, j: (layer, 0, j)),
                  pl.BlockSpec((None, d, bn), lambda i, j: (layer, 0, nj + j))],
        out_specs=pl.BlockSpec((bm, bn), lambda i, j: (i, j)),
        out_shape=jax.ShapeDtypeStruct((t, FFN_HIDDEN), BF16),
        compiler_params=_cp(("parallel", "arbitrary")),
        name="ffn_gate_up",
    )(hn, w_gu, w_gu)


def _ffn_down_kernel(a_ref, w_ref, x_ref, gpost_ref, gnext_ref, y_ref, xn_ref):
    y = x_ref[...] + _rms(jnp.dot(a_ref[...], w_ref[...], preferred_element_type=F32), gpost_ref[...])
    y_ref[...] = y
    xn_ref[...] = _rms(y, gnext_ref[...]).astype(xn_ref.dtype)


def ffn_down_residual(act, w_down, layer, x, g_post, g_next):
    t, d = x.shape
    f = act.shape[1]
    bm = _pick(t, 256)
    row = pl.BlockSpec((bm, d), lambda i: (i, 0))
    vec = pl.BlockSpec((1, d), lambda i: (0, 0))
    return pl.pallas_call(
        _ffn_down_kernel,
        grid=(t // bm,),
        in_specs=[pl.BlockSpec((bm, f), lambda i: (i, 0)), _resident((None, f, d), lambda i: (layer, 0, 0)),
                  row, vec, vec],
        out_specs=[row, row],
        out_shape=[jax.ShapeDtypeStruct((t, d), F32), jax.ShapeDtypeStruct((t, d), BF16)],
        compiler_params=_cp(("parallel",)),
        name="ffn_down",
    )(act, w_down, x, g_post.reshape(1, d), g_next.reshape(1, d))


def _conv_kernel(u_ref, b_ref, c_ref, w_ref, o_ref, st_ref):
    v = c_ref[...] * u_ref[...]
    s = v.shape[0]
    row = lax.broadcasted_iota(jnp.int32, v.shape, 0)
    v1 = jnp.where(row >= 1, pltpu.roll(v, 1, 0), 0.0)
    v2 = jnp.where(row >= 2, pltpu.roll(v, 2, 0), 0.0)
    w = w_ref[...]
    y = w[0:1] * v2 + w[1:2] * v1 + w[2:3] * v
    o_ref[...] = (b_ref[...] * y).astype(o_ref.dtype)
    st_ref[...] = v[s - (CONV_WIDTH - 1):, :]


def conv_prompt(cg, conv_w, n, s):
    bc = 256
    nj = CONV_DIM // bc
    base = N_BRANCH * D_MODEL // bc
    return pl.pallas_call(
        _conv_kernel,
        grid=(n, nj),
        in_specs=[pl.BlockSpec((s, bc), lambda b, j: (b, base + j)),
                  pl.BlockSpec((s, bc), lambda b, j: (b, base + nj + j)),
                  pl.BlockSpec((s, bc), lambda b, j: (b, base + 2 * nj + j)),
                  pl.BlockSpec((CONV_WIDTH, bc), lambda b, j: (0, j))],
        out_specs=[pl.BlockSpec((s, bc), lambda b, j: (b, j)),
                   pl.BlockSpec((None, CONV_WIDTH - 1, bc), lambda b, j: (b, 0, j))],
        out_shape=[jax.ShapeDtypeStruct((n * s, CONV_DIM), BF16),
                   jax.ShapeDtypeStruct((n, CONV_WIDTH - 1, CONV_DIM), F32)],
        compiler_params=_cp(("parallel", "arbitrary")),
        name="conv_prompt",
    )(cg, cg, cg, conv_w)


def _conv_dec_kernel(u_ref, b_ref, c_ref, buf_ref, w_ref, o_ref, st_ref):
    v = c_ref[...] * u_ref[...]
    b0 = buf_ref[:, 0, :]
    b1 = buf_ref[:, 1, :]
    w = w_ref[...]
    y = w[0:1] * b0 + w[1:2] * b1 + w[2:3] * v
    o_ref[...] = (b_ref[...] * y).astype(o_ref.dtype)
    st_ref[:, 0, :] = b1
    st_ref[:, 1, :] = v


def conv_decode(cg, buf, conv_w):
    n = cg.shape[0]
    base = N_BRANCH * D_MODEL // CONV_DIM
    blk = lambda j: pl.BlockSpec((n, CONV_DIM), lambda i: (0, base + j))
    full3 = pl.BlockSpec((n, CONV_WIDTH - 1, CONV_DIM), lambda i: (0, 0, 0))
    return pl.pallas_call(
        _conv_dec_kernel,
        grid=(1,),
        in_specs=[blk(0), blk(1), blk(2), full3, pl.BlockSpec((CONV_WIDTH, CONV_DIM), lambda i: (0, 0))],
        out_specs=[pl.BlockSpec((n, CONV_DIM), lambda i: (0, 0)), full3],
        out_shape=[jax.ShapeDtypeStruct((n, CONV_DIM), BF16),
                   jax.ShapeDtypeStruct((n, CONV_WIDTH - 1, CONV_DIM), F32)],
        compiler_params=_cp(("arbitrary",)),
        name="conv_decode",
    )(cg, cg, cg, buf, conv_w)


def _compress(read_x, w1_ref, w2_ref, pe_ref, nc):
    out = []
    for kv in range(2):
        acc = jnp.zeros((NSA_KV * nc, 2 * HEAD_DIM), F32)
        bias = jnp.zeros((8, 2 * HEAD_DIM), F32)
        for tp in range(CMP_STRIDE // 2):
            xt = jnp.concatenate(
                [jnp.concatenate([read_x(2 * tp + u, kv * NSA_KV + g) for u in range(2)], axis=1)
                 for g in range(NSA_KV)], axis=0).astype(BF16)
            wt = w1_ref[kv, tp]
            acc += jnp.dot(xt, wt, preferred_element_type=F32)
            bias += jnp.dot(pe_ref[kv, tp], wt, preferred_element_type=F32)
        pe_bias = bias[0:1, :HEAD_DIM] + bias[1:2, HEAD_DIM:]
        per_group = []
        for g in range(NSA_KV):
            a = acc[g * nc:(g + 1) * nc, :HEAD_DIM]
            b = acc[g * nc:(g + 1) * nc, HEAD_DIM:]
            pre = a + pltpu.roll(b, nc - 1, 0) + pe_bias
            hid = pre * _sigmoid(pre)
            per_group.append(jnp.dot(hid.astype(BF16), w2_ref[kv], preferred_element_type=F32))
        out.append(per_group)
    return out[0], out[1]


def _gate_rows(gates_blk):
    return gates_blk.T


def _stack_heads(q_ref, g, bq):
    return jnp.concatenate(
        [q_ref[:, (g * NSA_REP + r) * HEAD_DIM:(g * NSA_REP + r + 1) * HEAD_DIM] for r in range(NSA_REP)],
        axis=0).astype(BF16)


KV_SLOTS = 2 * NSA_KV
CHUNK_ROWS = CMP_STRIDE * KV_SLOTS
CHUNK_PITCH = CHUNK_ROWS + 8


def _kv_rows(kv_ref, k0, n, slot):
    return kv_ref[pl.ds(k0 * KV_SLOTS + slot, n, stride=KV_SLOTS), :]


EXP2_SCALE = ATTN_SCALE * math.log2(math.e)
FLASH_KC = 256


def _flash_scratch(bq):
    nq = NSA_REP * bq
    return [pltpu.VMEM((NSA_KV, nq, HEAD_DIM), BF16), pltpu.VMEM((NSA_KV, 1, nq), F32),
            pltpu.VMEM((NSA_KV, 1, nq), F32), pltpu.VMEM((NSA_KV, HEAD_DIM, nq), F32),
            pltpu.VMEM((NSA_KV, FLASH_KC, nq), F32), pltpu.VMEM((NSA_KV, FLASH_KC, nq), F32)]


def _flash_t(st, q_ref, kv_ref, o_ref, lo, hi, kc, bias_fn, bq, gate_t, branch):
    qs_ref, m_ref, l_ref, acc_ref, s_even, s_odd = st
    for g in range(NSA_KV):
        qs_ref[g] = _stack_heads(q_ref, g, bq)
    m_ref[...] = jnp.full(m_ref.shape, NEG_INF, F32)
    l_ref[...] = jnp.zeros(l_ref.shape, F32)
    acc_ref[...] = jnp.zeros(acc_ref.shape, F32)

    def scores(c, s_ref):
        k0 = pl.multiple_of(c * kc, kc)
        for g in range(NSA_KV):
            kb = _kv_rows(kv_ref, k0, kc, g).astype(BF16)
            s = lax.dot_general(kb, qs_ref[g], (((1,), (1,)), ((), ())), preferred_element_type=F32)
            s_ref[g] = s + jnp.concatenate([bias_fn(k0, g)] * NSA_REP, axis=1)

    def step(c, s_cur, s_nxt):
        scores(jnp.minimum(c + 1, hi - 1), s_nxt)
        k0 = pl.multiple_of(c * kc, kc)
        for g in range(NSA_KV):
            vb = _kv_rows(kv_ref, k0, kc, NSA_KV + g).astype(BF16)
            s = s_cur[g]
            m = m_ref[g]
            m_new = jnp.maximum(m, jnp.max(s, axis=0, keepdims=True))
            alpha = jnp.exp2((m - m_new) * EXP2_SCALE)
            p = jnp.exp2((s - m_new) * EXP2_SCALE)
            m_ref[g] = m_new
            l_ref[g] = alpha * l_ref[g] + jnp.sum(p, axis=0, keepdims=True)
            pv = lax.dot_general(vb, p.astype(BF16), (((0,), (0,)), ((), ())), preferred_element_type=F32)
            acc_ref[g] = alpha * acc_ref[g] + pv

    scores(lo, s_even)
    n = hi - lo

    def pair(i, _):
        c = lo + 2 * i
        step(c, s_even, s_odd)
        step(c + 1, s_odd, s_even)
        return 0

    lax.fori_loop(0, n // 2, pair, 0)

    @pl.when(n % 2 == 1)
    def _():
        step(hi - 1, s_even, s_odd)
    for g in range(NSA_KV):
        ot = jnp.where(m_ref[g] > 0.5 * NEG_INF, acc_ref[g] / jnp.maximum(l_ref[g], 1e-30), 0.0)
        _store_heads(o_ref, [ot[:, r * bq:(r + 1) * bq] for r in range(NSA_REP)], g, gate_t, branch)


def _store_heads(o_ref, heads, g, gate_t, branch):
    for r, oh in enumerate(heads):
        h = g * NSA_REP + r
        if gate_t is not None:
            oh = oh * _sigmoid(gate_t[h * 3 + branch:h * 3 + branch + 1, :])
        o_ref[:, h * HEAD_DIM:(h + 1) * HEAD_DIM] = oh.T


def _cmp_prompt_kernel(x_ref, q_ref, gates_ref, w1_ref, w2_ref, pe_ref, ov_ref, o_ref, ps_ref, kc_ref, vc_ref,
                       *, nc, bq):
    qi = pl.program_id(1)

    @pl.when(qi == 0)
    def _():
        def read_x(t, kg):
            return x_ref[pl.ds(t * KV_SLOTS + kg, nc, stride=CMP_STRIDE * KV_SLOTS), :]
        k_c, v_c = _compress(read_x, w1_ref, w2_ref, pe_ref, nc)
        for g in range(NSA_KV):
            kc_ref[g] = k_c[g].astype(BF16)
            vc_ref[g] = v_c[g].astype(BF16)

    q0 = qi * bq
    nq = NSA_REP * bq
    gate_t = _gate_rows(gates_ref[...])
    pos = q0 + lax.broadcasted_iota(jnp.int32, (nc, bq), 1)
    blk_end = lax.broadcasted_iota(jnp.int32, (nc, bq), 0) * CMP_STRIDE + (CMP_LEN - 1)
    mk1 = blk_end <= pos
    mk = jnp.concatenate([mk1] * NSA_REP, axis=1)
    for g in range(NSA_KV):
        qs = _stack_heads(q_ref, g, bq)
        s = lax.dot_general(kc_ref[g], qs, (((1,), (1,)), ((), ())), preferred_element_type=F32) * ATTN_SCALE
        s = jnp.where(mk, s, NEG_INF)
        m = jnp.max(s, axis=0, keepdims=True)
        e = jnp.where(mk, jnp.exp(s - m), 0.0)
        p = e / jnp.maximum(jnp.sum(e, axis=0, keepdims=True), 1e-30)
        ot = lax.dot_general(vc_ref[g], p.astype(BF16), (((0,), (0,)), ((), ())), preferred_element_type=F32)
        _store_heads(o_ref, [ot[:, r * bq:(r + 1) * bq] for r in range(NSA_REP)], g, gate_t, 0)
        psum = p[:, 0:bq]
        for r in range(1, NSA_REP):
            psum = psum + p[:, r * bq:(r + 1) * bq]
        ps_ref[g] = jnp.dot(ov_ref[...], psum, precision=lax.Precision.HIGHEST, preferred_element_type=F32)


def cmp_prompt(cmp_x, layer, q, idxm, w1cat, w2, pe8, ov_t, n, s):
    nc = s // CMP_STRIDE
    bq = 256
    nq = s // bq
    n_slc = ov_t.shape[0]
    t = n * s
    full = lambda a: pl.BlockSpec(a.shape, lambda b, i: (0,) * a.ndim)
    return pl.pallas_call(
        functools.partial(_cmp_prompt_kernel, nc=nc, bq=bq),
        grid=(n, nq),
        in_specs=[pl.BlockSpec((None, s * KV_SLOTS, HEAD_DIM), lambda b, i: (layer, b, 0)),
                  pl.BlockSpec((bq, BRANCH_W), lambda b, i: (b * nq + i, 0)),
                  pl.BlockSpec((bq, LANE), lambda b, i: (b * nq + i, 5)),
                  full(w1cat), full(w2), full(pe8), full(ov_t)],
        out_specs=[pl.BlockSpec((bq, BRANCH_W), lambda b, i: (b * nq + i, 0)),
                   pl.BlockSpec((NSA_KV, n_slc, bq), lambda b, i: (0, 0, b * nq + i))],
        out_shape=[jax.ShapeDtypeStruct((t, BRANCH_W), F32), jax.ShapeDtypeStruct((NSA_KV, n_slc, t), F32)],
        scratch_shapes=[pltpu.VMEM((NSA_KV, nc, HEAD_DIM), BF16), pltpu.VMEM((NSA_KV, nc, HEAD_DIM), BF16)],
        compiler_params=_cp(("arbitrary", "arbitrary")),
        name="nsa_cmp_prompt",
    )(cmp_x, q, idxm, w1cat, w2, pe8, ov_t)


def _topk_rank_rows(sc, n_rows, k):
    j = lax.broadcasted_iota(jnp.int32, sc.shape, 0)
    rank = jnp.zeros(sc.shape, jnp.int32)
    for i in range(n_rows):
        si = sc[i:i + 1, :]
        beats = jnp.where(si > sc, 1, jnp.where((si == sc) & (j > i), 1, 0))
        rank = rank + beats
    return jnp.where(rank < k, 1.0, 0.0)


def _slc_prompt_kernel(q_ref, kv_ref, ps_ref, gates_ref, o_ref, bias_ref, *st, n_slc, bq, kc):
    qi = pl.program_id(1)
    q0 = qi * bq
    gate_t = _gate_rows(gates_ref[...])
    blk = lax.broadcasted_iota(jnp.int32, (n_slc, bq), 0)
    pos = q0 + lax.broadcasted_iota(jnp.int32, (n_slc, bq), 1)
    cur = pos // SEL_BLOCK
    forced = (blk == 0) | (blk == cur) | (blk == cur - 1)
    visible = blk * SEL_BLOCK <= pos
    qpos = q0 + lax.broadcasted_iota(jnp.int32, (SEL_BLOCK, bq), 1)
    krow = lax.broadcasted_iota(jnp.int32, (SEL_BLOCK, bq), 0)
    for g in range(NSA_KV):
        sc = jnp.where(forced, FORCE_SCORE, jnp.where(visible, ps_ref[g], NEG_INF))
        sel = _topk_rank_rows(sc, n_slc, min(N_SEL, n_slc))
        for j in range(n_slc):
            keep = (jnp.broadcast_to(sel[j:j + 1, :], (SEL_BLOCK, bq)) > 0.5) & (j * SEL_BLOCK + krow <= qpos)
            bias_ref[g, j * SEL_BLOCK:(j + 1) * SEL_BLOCK, :] = jnp.where(keep, 0.0, NEG_INF)

    _flash_t(st, q_ref, kv_ref, o_ref, 0, (q0 + bq + kc - 1) // kc, kc,
             lambda k0, g: bias_ref[g, pl.ds(k0, kc), :], bq, gate_t, 1)


def slc_prompt(q, kv4, layer, p_slc, idxm, n, s):
    bq, kc = 256, FLASH_KC
    nq = s // bq
    n_slc = p_slc.shape[1]
    t = n * s
    return pl.pallas_call(
        functools.partial(_slc_prompt_kernel, n_slc=n_slc, bq=bq, kc=kc),
        grid=(n, nq),
        in_specs=[pl.BlockSpec((bq, BRANCH_W), lambda b, i: (b * nq + i, 0)),
                  pl.BlockSpec((None, s * KV_SLOTS, HEAD_DIM), lambda b, i: (layer, b, 0)),
                  pl.BlockSpec((NSA_KV, n_slc, bq), lambda b, i: (0, 0, b * nq + i)),
                  pl.BlockSpec((bq, LANE), lambda b, i: (b * nq + i, 5))],
        out_specs=pl.BlockSpec((bq, BRANCH_W), lambda b, i: (b * nq + i, 0)),
        out_shape=jax.ShapeDtypeStruct((t, BRANCH_W), F32),
        scratch_shapes=[pltpu.VMEM((NSA_KV, s, bq), F32)] + _flash_scratch(bq),
        compiler_params=_cp(("parallel", "arbitrary")),
        name="nsa_slc_prompt",
    )(q, kv4, p_slc, idxm)


def _win_prompt_kernel(q_ref, kv_ref, gates_ref, o_ref, *st, bq, kc):
    qi = pl.program_id(1)
    q0 = qi * bq
    gate_t = _gate_rows(gates_ref[...])
    qpos = q0 + lax.broadcasted_iota(jnp.int32, (kc, bq), 1)
    krow = lax.broadcasted_iota(jnp.int32, (kc, bq), 0)

    def bias_fn(k0, g):
        rel = qpos - (k0 + krow)
        return jnp.where((rel >= 0) & (rel <= WINDOW), 0.0, NEG_INF)

    lo = jnp.maximum(q0 - WINDOW, 0) // kc
    hi = (q0 + bq + kc - 1) // kc
    _flash_t(st, q_ref, kv_ref, o_ref, lo, hi, kc, bias_fn, bq, gate_t, 2)


def win_prompt(q, kv4, layer, idxm, n, s):
    bq, kc = 256, FLASH_KC
    nq = s // bq
    t = n * s
    return pl.pallas_call(
        functools.partial(_win_prompt_kernel, bq=bq, kc=kc),
        grid=(n, nq),
        in_specs=[pl.BlockSpec((bq, BRANCH_W), lambda b, i: (b * nq + i, 0)),
                  pl.BlockSpec((None, s * KV_SLOTS, HEAD_DIM), lambda b, i: (layer, b, 0)),
                  pl.BlockSpec((bq, LANE), lambda b, i: (b * nq + i, 5))],
        out_specs=pl.BlockSpec((bq, BRANCH_W), lambda b, i: (b * nq + i, 0)),
        out_shape=jax.ShapeDtypeStruct((t, BRANCH_W), F32),
        scratch_shapes=_flash_scratch(bq),
        compiler_params=_cp(("parallel", "arbitrary")),
        name="nsa_win_prompt",
    )(q, kv4, idxm)


def _order_key(x):
    b = pltpu.bitcast(x + 0.0, jnp.int32)
    return b ^ ((b >> 31) & jnp.int32(0x7FFFFFFF))


def _radix_kth(count_ge, k, shape):
    zero = jnp.zeros(shape, jnp.int32)
    base = jnp.where(count_ge(zero) >= k, zero, jnp.full(shape, INT_MIN, jnp.int32))

    def body(i, base):
        cand = base | jnp.left_shift(jnp.int32(1), 30 - i)
        return jnp.where(count_ge(cand) >= k, cand, base)

    return lax.fori_loop(0, 31, body, base)


def _tie_cut(count_eq_below, need, shape, n_bits):
    def body(i, m):
        cand = m | jnp.left_shift(jnp.int32(1), n_bits - 1 - i)
        return jnp.where(count_eq_below(cand) <= need, cand, m)

    return lax.fori_loop(0, n_bits, body, jnp.zeros(shape, jnp.int32))


def _dsa_prompt_kernel(q_ref, qi_ref, kidx_ref, kv_ref, gates_ref, o_ref, key_ref, bias_ref, *st,
                       bq, kc, kf, s_len):
    qi = pl.program_id(1)
    q0 = qi * bq
    n_chunks = (q0 + bq + kc - 1) // kc
    gate_t = _gate_rows(gates_ref[...])
    n_tiles = IDX_HEADS // 2
    w_scale = IDX_HEADS ** -0.5 * IDX_SCALE
    w_rows = [jnp.concatenate([gate_t[24 + 2 * t + odd:25 + 2 * t + odd, :] for t in range(n_tiles)], axis=1) * w_scale
              for odd in range(2)]
    q_tiles = jnp.concatenate([qi_ref[:, t * LANE:(t + 1) * LANE] for t in range(n_tiles)], axis=0).astype(BF16)
    qpos = q0 + lax.broadcasted_iota(jnp.int32, (kc, bq), 1)
    krow = lax.broadcasted_iota(jnp.int32, (kc, bq), 0)

    def score_body(c, _):
        k0 = pl.multiple_of(c * kc, kc)
        kb = kidx_ref[pl.ds(k0, kc), :]
        sc = jnp.zeros((kc, bq), F32)
        for odd in range(2):
            kh = (pltpu.roll(kb, IDX_DIM, 1) if odd else kb).astype(BF16)
            lg = lax.dot_general(kh, q_tiles, (((1,), (1,)), ((), ())), preferred_element_type=F32)
            wl = jnp.maximum(lg, 0.0) * w_rows[odd]
            for t in range(n_tiles):
                sc = sc + wl[:, t * bq:(t + 1) * bq]
        sc = jnp.where(k0 + krow <= qpos, sc, NEG_INF)
        key_ref[pl.ds(k0, kc), :] = _order_key(sc)
        return 0

    lax.fori_loop(0, n_chunks, score_body, 0)

    def count(pred):
        def body(c, acc):
            k0 = pl.multiple_of(c * kc, kc)
            hit = jnp.where(pred(key_ref[pl.ds(k0, kc), :], k0 + krow), 1, 0)
            return acc + jnp.sum(hit.reshape(kc // 8, 8, bq), axis=0)
        acc = lax.fori_loop(0, n_chunks, body, jnp.zeros((8, bq), jnp.int32))
        return jnp.sum(acc, axis=0, keepdims=True)

    k_top = min(DSA_TOPK, s_len // 4)
    row1 = (1, bq)

    def select(_):
        thr = _radix_kth(lambda cand: count(lambda key, idx: key >= cand), k_top, row1)
        need = k_top - count(lambda key, idx: key > thr)
        n_eq = count(lambda key, idx: key == thr)
        n_bits = max(1, int(s_len).bit_length())
        cut = lax.cond(
            jnp.any(n_eq != need),
            lambda _: _tie_cut(lambda m: count(lambda key, idx: (key == thr) & (idx < m)), need, row1, n_bits),
            lambda _: jnp.full(row1, s_len, jnp.int32), 0)
        return thr, cut

    thr, cut = lax.cond(q0 + bq > k_top, select,
                        lambda _: (jnp.full(row1, INT_MIN, jnp.int32), jnp.full(row1, s_len, jnp.int32)), 0)

    def mask_body(c, _):
        k0 = pl.multiple_of(c * kc, kc)
        key = key_ref[pl.ds(k0, kc), :]
        idx = k0 + krow
        sel = ((key > thr) | ((key == thr) & (idx < cut))) & (idx <= qpos)
        bias_ref[pl.ds(k0, kc), :] = jnp.where(sel, 0.0, NEG_INF)
        return 0

    lax.fori_loop(0, n_chunks, mask_body, 0)

    _flash_t(st, q_ref, kv_ref, o_ref, 0, (q0 + bq + kf - 1) // kf, kf,
             lambda k0, g: bias_ref[pl.ds(k0, kf), :], bq, None, 0)


def dsa_prompt(q, kv4, layer, idxm, n, s):
    bq = kc = 256
    nq = s // bq
    t = n * s
    return pl.pallas_call(
        functools.partial(_dsa_prompt_kernel, bq=bq, kc=kc, kf=FLASH_KC, s_len=s),
        grid=(n, nq),
        in_specs=[pl.BlockSpec((bq, BRANCH_W), lambda b, i: (b * nq + i, 1)),
                  pl.BlockSpec((bq, IDX_HEADS * IDX_DIM), lambda b, i: (b * nq + i, 0)),
                  pl.BlockSpec((s, LANE), lambda b, i: (b, 4)),
                  pl.BlockSpec((None, s * KV_SLOTS, HEAD_DIM), lambda b, i: (layer, b, 0)),
                  pl.BlockSpec((bq, LANE), lambda b, i: (b * nq + i, 5))],
        out_specs=pl.BlockSpec((bq, BRANCH_W), lambda b, i: (b * nq + i, 0)),
        out_shape=jax.ShapeDtypeStruct((t, BRANCH_W), F32),
        scratch_shapes=[pltpu.VMEM((s, bq), jnp.int32), pltpu.VMEM((s, bq), F32)] + _flash_scratch(bq),
        compiler_params=_cp(("parallel", "arbitrary")),
        name="dsa_prompt",
    )(q, idxm, idxm, kv4, idxm)


def _page_copies(pt_ref, cache_ref, layer, buf_ref, sem_ref, seq, slot, n_pages, rows, mode, need_ref=None):
    def each(fn):
        def copy_page(p):
            page = pt_ref[seq, p]
            if mode == "chunks":
                for cc in range(rows):
                    dst = buf_ref.at[slot, pl.ds((p * rows + cc) * CHUNK_PITCH, CHUNK_ROWS)]
                    fn(pltpu.make_async_copy(cache_ref.at[layer, page, cc], dst, sem_ref.at[slot]))
                return
            win = pl.ds(p * rows, rows)
            dst = buf_ref.at[slot, :, win] if mode == "lanes" else buf_ref.at[slot, win]
            fn(pltpu.make_async_copy(cache_ref.at[layer, page], dst, sem_ref.at[slot]))

        def body(p, _):
            if need_ref is None:
                copy_page(p)
            else:
                pl.when(need_ref[seq, p] > 0)(lambda: copy_page(p))
            return 0
        lax.fori_loop(0, n_pages, body, 0)
    return each


def _gather_step(pt_ref, cache_ref, layer, buf_ref, sem_ref, n_pages, rows, mode="rows", need_ref=None):
    b = pl.program_id(0)
    nb = pl.num_programs(0)
    slot = b % 2
    copies = functools.partial(_page_copies, pt_ref, cache_ref, layer, buf_ref, sem_ref,
                               n_pages=n_pages, rows=rows, mode=mode, need_ref=need_ref)

    @pl.when(b == 0)
    def _():
        if need_ref is not None:
            buf_ref[...] = jnp.zeros(buf_ref.shape, buf_ref.dtype)
        copies(seq=0, slot=0)(lambda cp: cp.start())

    @pl.when(b + 1 < nb)
    def _():
        copies(seq=b + 1, slot=1 - slot)(lambda cp: cp.start())

    copies(seq=b, slot=slot)(lambda cp: cp.wait())
    return slot


def _head_column(row, offset, stride):
    lane = lax.broadcasted_iota(jnp.int32, (8, LANE), 1)
    h = lax.broadcasted_iota(jnp.int32, (8, LANE), 0)
    return jnp.sum(jnp.where(lane == offset + stride * h, jnp.broadcast_to(row, (8, LANE)), 0.0), axis=1, keepdims=True)


def _q8(q_ref, width):
    return jnp.concatenate([q_ref[0:1, h * width:(h + 1) * width] for h in range(8)], axis=0)


def _row_spec(width, blk, n_extra):
    del n_extra
    return pl.BlockSpec((None, 1, width), lambda b, *_: (b, 0, blk))


def _cmp_dec_kernel(pt_ref, cache_ref, q_ref, gates_ref, w1_ref, w2_ref, pe_ref, ov_ref, o_ref, ps_ref,
                    buf_ref, sem_ref, *, layer, n_pages, p0):
    nc = n_pages * PAGE_SIZE // CMP_STRIDE
    slot = _gather_step(pt_ref, cache_ref, layer, buf_ref, sem_ref, n_pages, PAGE_SIZE // CMP_STRIDE,
                        mode="chunks")

    def read_x(t, kg):
        return buf_ref[slot, pl.ds(t * KV_SLOTS + kg, nc, stride=CHUNK_PITCH), :]

    k_c, v_c = _compress(read_x, w1_ref, w2_ref, pe_ref, nc)
    q8 = _q8(q_ref, HEAD_DIM).astype(BF16)
    head = lax.broadcasted_iota(jnp.int32, (8, 1), 0)
    blk_end = lax.broadcasted_iota(jnp.int32, (8, nc), 1) * CMP_STRIDE + (CMP_LEN - 1)
    mk = blk_end <= p0
    s = jnp.zeros((8, nc), F32)
    for g in range(NSA_KV):
        sg = lax.dot_general(q8, k_c[g].astype(BF16), (((1,), (1,)), ((), ())), preferred_element_type=F32)
        s = jnp.where(head // NSA_REP == g, sg, s)
    s = jnp.where(mk, s * ATTN_SCALE, NEG_INF)
    m = jnp.max(s, axis=1, keepdims=True)
    e = jnp.where(mk, jnp.exp(s - m), 0.0)
    p = e / jnp.maximum(jnp.sum(e, axis=1, keepdims=True), 1e-30)
    o = jnp.zeros((8, HEAD_DIM), F32)
    for g in range(NSA_KV):
        og = jnp.dot(p.astype(BF16), v_c[g].astype(BF16), preferred_element_type=F32)
        o = jnp.where(head // NSA_REP == g, og, o)
        psum = jnp.sum(jnp.where(head // NSA_REP == g, p, 0.0), axis=0, keepdims=True)
        ps8 = jnp.dot(jnp.broadcast_to(psum, (8, nc)), ov_ref[...], precision=lax.Precision.HIGHEST,
                      preferred_element_type=F32)
        ps_ref[g:g + 1, :] = ps8[0:1, :]
    gate = _sigmoid(_head_column(gates_ref[...], 0, 3))
    o_ref[...] = o * gate


def cmp_decode(page_table, cache_x, layer, q, idxm, w1cat, w2, pe8, ov, p0):
    n, n_pages = page_table.shape
    n_slc_pad = ov.shape[1]
    full = lambda a: pl.BlockSpec(a.shape, lambda b, pt: (0,) * a.ndim)
    gs = pltpu.PrefetchScalarGridSpec(
        num_scalar_prefetch=1,
        grid=(n,),
        in_specs=[pl.BlockSpec(memory_space=pl.ANY), _row_spec(BRANCH_W, 0, 1), _row_spec(LANE, 5, 1),
                  full(w1cat), full(w2), full(pe8), full(ov)],
        out_specs=[pl.BlockSpec((None, NSA_HEADS, HEAD_DIM), lambda b, pt: (b, 0, 0)),
                   pl.BlockSpec((None, NSA_KV, n_slc_pad), lambda b, pt: (b, 0, 0))],
        scratch_shapes=[pltpu.VMEM((2, n_pages * PAGE_SIZE // CMP_STRIDE * CHUNK_PITCH, HEAD_DIM), F32),
                        pltpu.SemaphoreType.DMA((2,))],
    )
    return pl.pallas_call(
        functools.partial(_cmp_dec_kernel, layer=layer, n_pages=n_pages, p0=p0),
        grid_spec=gs,
        out_shape=[jax.ShapeDtypeStruct((n, NSA_HEADS, HEAD_DIM), F32),
                   jax.ShapeDtypeStruct((n, NSA_KV, n_slc_pad), F32)],
        compiler_params=_cp(("arbitrary",)),
        name="nsa_cmp_decode",
    )(page_table, cache_x, q, idxm, w1cat, w2, pe8, ov)


def _slc_mask_kernel(ps_ref, e_ref, pg_ref, o_ref, need_ref, *, n_slc, p0):
    sc = ps_ref[...]
    j = lax.broadcasted_iota(jnp.int32, sc.shape, 1)
    cur = p0 // SEL_BLOCK
    forced = (j == 0) | (j == cur) | (j == cur - 1)
    visible = j * SEL_BLOCK <= p0
    sc = jnp.where(j >= n_slc, BELOW_ALL, jnp.where(forced, FORCE_SCORE, jnp.where(visible, sc, NEG_INF)))
    rank = jnp.zeros(sc.shape, jnp.int32)
    for i in range(n_slc):
        si = sc[:, i:i + 1]
        rank = rank + jnp.where(si > sc, 1, jnp.where((si == sc) & (j > i), 1, 0))
    sel = jnp.where((rank < min(N_SEL, n_slc)) & (j < n_slc), 1.0, 0.0).astype(BF16)
    o_ref[...] = jnp.dot(sel, e_ref[...], preferred_element_type=F32)
    need_ref[...] = jnp.dot(sel, pg_ref[...], preferred_element_type=F32)


def slc_mask_decode(p_slc, expand, block_page, n_slc, n_pages, p0):
    n, g, w = p_slc.shape
    l_pad = expand.shape[1]
    assert n_pages <= LANE, "one lane per cache page in the page-need output"
    out, need = pl.pallas_call(
        functools.partial(_slc_mask_kernel, n_slc=n_slc, p0=p0),
        grid=(1,),
        in_specs=[pl.BlockSpec((n * g, w), lambda i: (0, 0)), pl.BlockSpec(expand.shape, lambda i: (0, 0)),
                  pl.BlockSpec(block_page.shape, lambda i: (0, 0))],
        out_specs=[pl.BlockSpec((n * g, l_pad), lambda i: (0, 0)), pl.BlockSpec((n * g, LANE), lambda i: (0, 0))],
        out_shape=[jax.ShapeDtypeStruct((n * g, l_pad), F32), jax.ShapeDtypeStruct((n * g, LANE), F32)],
        compiler_params=_cp(("arbitrary",)),
        name="nsa_slc_mask_decode",
    )(p_slc.reshape(n * g, w), expand, block_page)
    need = (need.reshape(n, g, LANE).sum(axis=1) > 0.5).astype(jnp.int32)[:, :n_pages]
    return out.reshape(n, g, l_pad), need


def _idx_score_kernel(pt_ref, cache_ref, qi_ref, knew_ref, gates_ref, o_ref, buf_ref, sem_ref,
                      *, layer, n_pages, p0):
    past = n_pages * PAGE_SIZE
    l_pad = past + LANE
    slot = _gather_step(pt_ref, cache_ref, layer, buf_ref, sem_ref, n_pages, PAGE_SIZE, mode="lanes")
    d = lax.broadcasted_iota(jnp.int32, (IDX_DIM, LANE), 0)
    lane = lax.broadcasted_iota(jnp.int32, (IDX_DIM, LANE), 1)
    k_row = jnp.broadcast_to(knew_ref[...], (IDX_DIM, LANE))
    k_col = jnp.sum(jnp.where(lane == d, k_row, 0.0), axis=1, keepdims=True)
    buf_ref[slot, :, past:l_pad] = jnp.where(lane == 0, k_col, 0.0)
    q8 = _q8(qi_ref, IDX_DIM).astype(BF16)
    lg = jnp.dot(q8, buf_ref[slot].astype(BF16), preferred_element_type=F32) * IDX_SCALE
    w_col = _head_column(gates_ref[...], 24, 1) * (IDX_HEADS ** -0.5)
    sc = jnp.sum(jnp.maximum(lg, 0.0) * w_col, axis=0, keepdims=True)
    key = lax.broadcasted_iota(jnp.int32, (1, l_pad), 1)
    o_ref[...] = jnp.where(key <= p0, sc, BELOW_ALL)


def idx_score_decode(page_table, cache_idx, layer, idxm, p0):
    n, n_pages = page_table.shape
    l_pad = n_pages * PAGE_SIZE + LANE
    gs = pltpu.PrefetchScalarGridSpec(
        num_scalar_prefetch=1,
        grid=(n,),
        in_specs=[pl.BlockSpec(memory_space=pl.ANY), _row_spec(IDX_HEADS * IDX_DIM, 0, 1),
                  _row_spec(LANE, 4, 1), _row_spec(LANE, 5, 1)],
        out_specs=pl.BlockSpec((None, 1, l_pad), lambda b, pt: (b, 0, 0)),
        scratch_shapes=[pltpu.VMEM((2, IDX_DIM, l_pad), F32), pltpu.SemaphoreType.DMA((2,))],
    )
    return pl.pallas_call(
        functools.partial(_idx_score_kernel, layer=layer, n_pages=n_pages, p0=p0),
        grid_spec=gs,
        out_shape=jax.ShapeDtypeStruct((n, 1, l_pad), F32),
        compiler_params=_cp(("arbitrary",)),
        name="dsa_idx_score_decode",
    )(page_table, cache_idx, idxm, idxm, idxm)


def _dsa_mask_kernel(sc_ref, o_ref, *, k_top, p0):
    key = _order_key(sc_ref[...])
    n, l_pad = key.shape
    idx = lax.broadcasted_iota(jnp.int32, key.shape, 1)
    col = (n, 1)

    def count(pred):
        return jnp.sum(jnp.where(pred, 1, 0), axis=1, keepdims=True)

    thr = _radix_kth(lambda cand: count(key >= cand), k_top, col)
    need = k_top - count(key > thr)
    cut = _tie_cut(lambda m: count((key == thr) & (idx < m)), need, col, max(1, int(l_pad).bit_length()))
    sel = ((key > thr) | ((key == thr) & (idx < cut))) & (idx <= p0)
    mask = jnp.where(sel, 1.0, 0.0)
    for g in range(DSA_KV):
        o_ref[:, g, :] = mask


def dsa_mask_decode(score, k_top, p0):
    n, _, l_pad = score.shape
    return pl.pallas_call(
        functools.partial(_dsa_mask_kernel, k_top=k_top, p0=p0),
        grid=(1,),
        in_specs=[pl.BlockSpec((n, l_pad), lambda i: (0, 0))],
        out_specs=pl.BlockSpec((n, DSA_KV, l_pad), lambda i: (0, 0, 0)),
        out_shape=jax.ShapeDtypeStruct((n, DSA_KV, l_pad), F32),
        compiler_params=_cp(("arbitrary",)),
        name="dsa_mask_decode",
    )(score.reshape(n, l_pad))


def _attend_rows(q8, kv_rows, mask_ref, n_keys):
    head = lax.broadcasted_iota(jnp.int32, (8, 1), 0)
    o = jnp.zeros((8, HEAD_DIM), F32)
    for g in range(NSA_KV):
        kb = _kv_rows(kv_rows, 0, n_keys, g).astype(BF16)
        vb = _kv_rows(kv_rows, 0, n_keys, NSA_KV + g).astype(BF16)
        s = lax.dot_general(q8, kb, (((1,), (1,)), ((), ())), preferred_element_type=F32) * ATTN_SCALE
        mk = mask_ref[g:g + 1, :] > 0.5
        s = jnp.where(mk, s, NEG_INF)
        m = jnp.max(s, axis=1, keepdims=True)
        e = jnp.where(mk, jnp.exp(s - m), 0.0)
        p = e / jnp.maximum(jnp.sum(e, axis=1, keepdims=True), 1e-30)
        og = jnp.dot(p.astype(BF16), vb, preferred_element_type=F32)
        o = jnp.where(head // NSA_REP == g, og, o)
    return o


def _attn_paged_kernel(pt_ref, need_ref, cache_ref, q_ref, kvnew_ref, mask_ref, gates_ref, o_ref, buf_ref, sem_ref,
                       *, layer, n_pages, branch):
    past = n_pages * PAGE_SIZE
    l_pad = past + LANE
    slot = _gather_step(pt_ref, cache_ref, layer, buf_ref, sem_ref, n_pages, PAGE_SIZE * KV_SLOTS,
                        need_ref=need_ref)
    buf_ref[slot, past * KV_SLOTS:l_pad * KV_SLOTS, :] = jnp.zeros((LANE * KV_SLOTS, HEAD_DIM), F32)
    buf_ref[slot, past * KV_SLOTS:(past + 1) * KV_SLOTS, :] = kvnew_ref[...]
    o = _attend_rows(_q8(q_ref, HEAD_DIM).astype(BF16), buf_ref.at[slot], mask_ref, l_pad)
    if branch is not None:
        o = o * _sigmoid(_head_column(gates_ref[...], branch, 3))
    o_ref[...] = o


def attn_paged_decode(page_table, need, cache, layer, q, q_blk, kvnew, mask, idxm, branch):
    n, n_pages = page_table.shape
    l_pad = n_pages * PAGE_SIZE + LANE
    gs = pltpu.PrefetchScalarGridSpec(
        num_scalar_prefetch=2,
        grid=(n,),
        in_specs=[pl.BlockSpec(memory_space=pl.ANY), _row_spec(BRANCH_W, q_blk, 2),
                  pl.BlockSpec((None, KV_SLOTS, HEAD_DIM), lambda b, *_: (b, 0, 0)),
                  pl.BlockSpec((None, NSA_KV, l_pad), lambda b, *_: (b, 0, 0)), _row_spec(LANE, 5, 2)],
        out_specs=pl.BlockSpec((None, NSA_HEADS, HEAD_DIM), lambda b, *_: (b, 0, 0)),
        scratch_shapes=[pltpu.VMEM((2, l_pad * KV_SLOTS, HEAD_DIM), F32), pltpu.SemaphoreType.DMA((2,))],
    )
    return pl.pallas_call(
        functools.partial(_attn_paged_kernel, layer=layer, n_pages=n_pages, branch=branch),
        grid_spec=gs,
        out_shape=jax.ShapeDtypeStruct((n, NSA_HEADS, HEAD_DIM), F32),
        compiler_params=_cp(("arbitrary",)),
        name="attn_paged_decode",
    )(page_table, need, cache, q, kvnew, mask, idxm)


def _attn_win_kernel(st_ref, q_ref, kvnew_ref, gates_ref, o_ref, buf_ref, mask_ref, *, wb):
    l_pad = wb + LANE
    buf_ref[0:wb * KV_SLOTS, :] = st_ref[...]
    buf_ref[wb * KV_SLOTS:l_pad * KV_SLOTS, :] = jnp.zeros((LANE * KV_SLOTS, HEAD_DIM), F32)
    buf_ref[wb * KV_SLOTS:(wb + 1) * KV_SLOTS, :] = kvnew_ref[...]
    key = lax.broadcasted_iota(jnp.int32, (NSA_KV, l_pad), 1)
    mask_ref[...] = jnp.where(key <= wb, 1.0, 0.0)
    o = _attend_rows(_q8(q_ref, HEAD_DIM).astype(BF16), buf_ref, mask_ref, l_pad)
    o_ref[...] = o * _sigmoid(_head_column(gates_ref[...], 2, 3))


def attn_win_decode(state, layer, q, kvnew, idxm):
    n, wb = state.shape[1], state.shape[2] // KV_SLOTS
    l_pad = wb + LANE
    return pl.pallas_call(
        functools.partial(_attn_win_kernel, wb=wb),
        grid=(n,),
        in_specs=[pl.BlockSpec((None, None, wb * KV_SLOTS, HEAD_DIM), lambda b: (layer, b, 0, 0)),
                  _row_spec(BRANCH_W, 0, 0), pl.BlockSpec((None, KV_SLOTS, HEAD_DIM), lambda b: (b, 0, 0)),
                  _row_spec(LANE, 5, 0)],
        out_specs=pl.BlockSpec((None, NSA_HEADS, HEAD_DIM), lambda b: (b, 0, 0)),
        out_shape=jax.ShapeDtypeStruct((n, NSA_HEADS, HEAD_DIM), F32),
        scratch_shapes=[pltpu.VMEM((l_pad * KV_SLOTS, HEAD_DIM), F32), pltpu.VMEM((NSA_KV, l_pad), F32)],
        compiler_params=_cp(("parallel",)),
        name="attn_win_decode",
    )(state, q, kvnew, idxm)


def _rope_tables(pos, head_dim):
    d_rot = head_dim // ROPE_FRACTION
    half = d_rot // 2
    inv_freq = jnp.exp(jnp.arange(half, dtype=F32) * (-2.0 * math.log(ROPE_THETA) / d_rot))
    ang = pos.astype(F32)[:, None] * inv_freq[None, :]
    cos, sin = jnp.cos(ang), jnp.sin(ang)
    lane = np.arange(LANE) % head_dim
    j = lane % half
    first = jnp.asarray(lane < half)[None, :]
    second = jnp.asarray((lane >= half) & (lane < d_rot))[None, :]
    c = jnp.where(first | second, cos[:, j], 1.0)
    s1 = jnp.where(first, -sin[:, j], 0.0)
    s2 = jnp.where(second, sin[:, j], 0.0)
    return (c, s1, s2), half


def _split_w_in(w_in):
    sizes = (NSA_HEADS * HEAD_DIM, KV_W, KV_W, KV_W, 3 * NSA_HEADS, DSA_HEADS * HEAD_DIM, KV_W,
             IDX_HEADS * IDX_DIM, IDX_DIM, IDX_HEADS, CONV_DIM, CONV_DIM, CONV_DIM, N_BRANCH * D_MODEL)
    offs = np.concatenate([[0], np.cumsum(sizes)])
    col = lambda i: w_in[:, int(offs[i]):int(offs[i + 1])]
    (q_a, cmp_kv, slc_kv, win_kv, gate_a, q_b, dsa_kv, q_i, k_i, w_i, cu, cb, cc, gm) = [col(i) for i in range(14)]
    d = w_in.shape[0]
    zeros = lambda n: jnp.zeros((d, n), w_in.dtype)
    w_q = jnp.concatenate([q_a, q_b], axis=1).astype(BF16)
    w_kv = jnp.concatenate([cmp_kv, slc_kv, win_kv, dsa_kv], axis=1).astype(BF16)
    w_idx = jnp.concatenate([q_i, k_i, zeros(LANE - IDX_DIM), gate_a, w_i, zeros(LANE - 3 * NSA_HEADS - IDX_HEADS)],
                            axis=1).astype(BF16)
    w_cg = jnp.concatenate([gm, cu, cb, cc], axis=1).astype(BF16)
    return w_q, w_kv, w_idx, w_cg


def _cmp_weights(w1, w2, pe):
    half = CMP_STRIDE * HEAD_DIM
    pairs = CMP_STRIDE // 2
    wa = w1[:, :half].reshape(2, pairs, 2 * HEAD_DIM, HEAD_DIM)
    wb = w1[:, half:].reshape(2, pairs, 2 * HEAD_DIM, HEAD_DIM)
    w1cat = jnp.concatenate([wa, wb], axis=-1).astype(BF16)
    pe8 = jnp.zeros((2, pairs, 8, 2 * HEAD_DIM), F32)
    pe8 = pe8.at[:, :, 0, :].set(pe[:, :CMP_STRIDE].reshape(2, pairs, 2 * HEAD_DIM))
    pe8 = pe8.at[:, :, 1, :].set(pe[:, CMP_STRIDE:].reshape(2, pairs, 2 * HEAD_DIM))
    return w1cat, w2.astype(BF16), pe8.astype(BF16)


def _overlap(n_cmp_rows, n_slc, seq_len):
    n_cmp = seq_len // CMP_STRIDE - 1
    c = np.arange(n_cmp_rows)
    c_start = c * CMP_STRIDE
    s_start = np.arange(n_slc) * SEL_BLOCK
    ov = ((c_start[:, None] < s_start[None, :] + SEL_BLOCK) & (c_start[:, None] + CMP_LEN > s_start[None, :])
          & (c[:, None] < n_cmp))
    return ov.astype(np.float32)


def _project(xn, wts, tabs128, half128, tabs64, half64):
    w_q, _, w_idx, w_cg = wts
    q = proj_rope(xn, w_q, tabs128, bn=1024, half=half128, rope_blocks=(True,) * 8, stacked=False)
    idxm = proj_rope(xn, w_idx, tabs64, bn=w_idx.shape[1], half=half64,
                     rope_blocks=(True,) * 5 + (False,), stacked=False)
    cg = matmul(xn, w_cg, bn=1024)
    return q, idxm, cg


def _finish_layer(x, branches, oc, cg, lw, layer, g_next):
    a1, a2, a3, ob = branches
    merged = merge_branches(a1, a2, a3, ob, oc, lw["w_branch"], layer, cg)
    x_mid, hn = outproj_residual(merged, lw["w_out"], layer, x, lw["g_mix_post"], lw["g_ffn_pre"])
    act = ffn_gate_up(hn, lw["w_gu"], layer)
    return ffn_down_residual(act, lw["w_down"], layer, x_mid, lw["g_ffn_post"], g_next)


def _prompt_layer(x, xn, lw, layer, kv_all, n, s, consts, g_next):
    tabs128, half128 = consts["rope_p"][:2]
    q, idxm, cg = _project(xn, lw["w_in"], *consts["rope_p"])
    kv_all = proj_rope_kv(xn, lw["w_in"][1], tabs128, kv_all, layer, half=half128)
    a1, p_slc = cmp_prompt(kv_all[0], layer, q, idxm, *lw["cmp"], consts["ov_p_t"], n, s)
    a2 = slc_prompt(q, kv_all[1], layer, p_slc, idxm, n, s)
    a3 = win_prompt(q, kv_all[2], layer, idxm, n, s)
    ob = dsa_prompt(q, kv_all[3], layer, idxm, n, s)
    oc, conv_state = conv_prompt(cg, lw["conv_w"], n, s)
    y, xn_next = _finish_layer(x, (a1, a2, a3, ob), oc, cg, lw, layer, g_next)
    state = (idxm[:, IDX_HEADS * IDX_DIM:IDX_HEADS * IDX_DIM + IDX_DIM].reshape(n, s, IDX_DIM), conv_state)
    return y, xn_next, kv_all, state


def _sample_layer(x, xn, lw, layer, caches, page_table, consts, g_next):
    n = x.shape[0]
    p0 = consts["p0"]
    tabs128, half128 = consts["rope_s"][:2]
    q, idxm, cg = _project(xn, lw["w_in"], *consts["rope_s"])
    kv4 = proj_rope(xn, lw["w_in"][1], tabs128, bn=KV_W, half=half128,
                    rope_blocks=(True, True, False, False), stacked=True)
    cache_cmp_x, cache_slc, state_win, cache_dsa, cache_idx, state_conv = caches
    q3 = q.reshape(n, 1, -1)
    idx3 = idxm.reshape(n, 1, -1)
    new_row = lambda i: kv4[i].reshape(n, KV_SLOTS, HEAD_DIM)
    o_cmp, p_slc = cmp_decode(page_table, cache_cmp_x, layer, q3, idx3, *lw["cmp"], consts["ov_s"], p0)
    slc_mask, slc_need = slc_mask_decode(p_slc, consts["expand"], consts["block_page"], consts["n_slc_s"],
                                         page_table.shape[1], p0)
    o_slc = attn_paged_decode(page_table, slc_need, cache_slc, layer, q3, 0, new_row(1), slc_mask, idx3, 1)
    o_win = attn_win_decode(state_win, layer, q3, new_row(2), idx3)
    score = idx_score_decode(page_table, cache_idx, layer, idx3, p0)
    dsa_mask = dsa_mask_decode(score, min(DSA_TOPK, (p0 + 1) // 4), p0)
    o_dsa = attn_paged_decode(page_table, jnp.ones_like(page_table), cache_dsa, layer, q3, 1, new_row(3),
                              dsa_mask, idx3, None)
    oc, conv_state = conv_decode(cg, state_conv[layer], lw["conv_w"])
    flat = lambda a: a.reshape(n, BRANCH_W)
    y, xn_next = _finish_layer(x, (flat(o_cmp), flat(o_slc), flat(o_win), flat(o_dsa)), oc, cg, lw, layer, g_next)
    kv5 = lambda a: a.reshape(n, 1, 2, NSA_KV, HEAD_DIM)
    win_all = jnp.concatenate([consts["state_win"][layer], kv5(kv4[2])], axis=1)
    keep = min(WINDOW, win_all.shape[1])
    state = (kv5(kv4[0]), kv5(kv4[1]), win_all[:, win_all.shape[1] - keep:], kv5(kv4[3]),
             idxm[:, IDX_HEADS * IDX_DIM:IDX_HEADS * IDX_DIM + IDX_DIM].reshape(n, 1, IDX_DIM), conv_state)
    return y, xn_next, state


def kernel(x_prompt, x_sample, cache_nsa_cmp_kv, cache_nsa_slc_kv, state_nsa_win_kv, cache_dsa_kv, cache_dsa_idx_k, state_conv, page_table, norm_mix_pre, norm_mix_post, norm_ffn_pre, norm_ffn_post, w_in, cmp_w1, cmp_w2, cmp_pe, conv_w, w_branch, w_out, ffn_w_gate_up, ffn_w_down):
    n_p, s, d = x_prompt.shape
    n_s = x_sample.shape[0]
    depth = w_in.shape[0]
    n_pages = page_table.shape[1]
    n_pool = cache_nsa_cmp_kv.shape[1]
    p0 = n_pages * PAGE_SIZE
    l_s = p0 + 1
    l_pad = p0 + LANE
    n_slc_s = -(-l_s // SEL_BLOCK)
    n_slc_pad = -(-n_slc_s // LANE) * LANE
    nc_s = p0 // CMP_STRIDE

    tabs128_p, half128 = _rope_tables(jnp.arange(s, dtype=jnp.int32), HEAD_DIM)
    tabs64_p, half64 = _rope_tables(jnp.arange(s, dtype=jnp.int32), IDX_DIM)
    tabs128_s, _ = _rope_tables(jnp.full((n_s,), p0, jnp.int32), HEAD_DIM)
    tabs64_s, _ = _rope_tables(jnp.full((n_s,), p0, jnp.int32), IDX_DIM)
    key_block = np.arange(l_pad) // SEL_BLOCK
    expand = ((key_block[None, :] == np.arange(n_slc_pad)[:, None]) & (np.arange(l_pad)[None, :] <= p0))
    consts = {
        "p0": p0,
        "n_slc_s": n_slc_s,
        "state_win": state_nsa_win_kv,
        "rope_p": (tabs128_p, half128, tabs64_p, half64),
        "rope_s": (tabs128_s, half128, tabs64_s, half64),
        "ov_p_t": jnp.asarray(_overlap(s // CMP_STRIDE, -(-s // SEL_BLOCK), s).T),
        "ov_s": jnp.asarray(_overlap(nc_s, n_slc_pad, l_s) * (np.arange(n_slc_pad) < n_slc_s)[None, :]),
        "expand": jnp.asarray(expand.astype(np.float32)).astype(BF16),
        "block_page": jnp.asarray((np.arange(n_slc_pad)[:, None] // (PAGE_SIZE // SEL_BLOCK)
                                   == np.arange(LANE)[None, :]).astype(np.float32)).astype(BF16),
    }
    paged = lambda c: c.reshape(depth, n_pool, PAGE_SIZE * KV_SLOTS, HEAD_DIM)
    cmp_chunks = cache_nsa_cmp_kv.reshape(depth, n_pool, PAGE_SIZE // CMP_STRIDE, CHUNK_ROWS, HEAD_DIM)
    caches = (cmp_chunks, paged(cache_nsa_slc_kv),
              state_nsa_win_kv.reshape(depth, n_s, -1, HEAD_DIM), paged(cache_dsa_kv),
              jnp.swapaxes(cache_dsa_idx_k, 2, 3), state_conv)

    x_p = x_prompt.reshape(n_p * s, d)
    x_s = x_sample.reshape(n_s, d)
    xn_p = rmsnorm(x_p, norm_mix_pre[0])
    xn_s = rmsnorm(x_s, norm_mix_pre[0])
    stacked = {"w_branch": w_branch.astype(BF16), "w_out": w_out.astype(BF16),
               "w_gu": ffn_w_gate_up.astype(BF16), "w_down": ffn_w_down.astype(BF16)}
    kv_all = [jnp.zeros((depth, n_p * s * KV_SLOTS, HEAD_DIM), F32) for _ in range(4)]
    new_p, new_s = [], []
    for l in range(depth):
        lw = {
            "w_in": _split_w_in(w_in[l]),
            "cmp": _cmp_weights(cmp_w1[l], cmp_w2[l], cmp_pe[l]),
            "conv_w": conv_w[l],
            "g_mix_post": norm_mix_post[l], "g_ffn_pre": norm_ffn_pre[l], "g_ffn_post": norm_ffn_post[l],
            **stacked,
        }
        g_next = norm_mix_pre[l + 1] if l + 1 < depth else norm_mix_pre[l]
        x_p, xn_p, kv_all, st_p = _prompt_layer(x_p, xn_p, lw, l, kv_all, n_p, s, consts, g_next)
        x_s, xn_s, st_s = _sample_layer(x_s, xn_s, lw, l, caches, page_table, consts, g_next)
        new_p.append(st_p)
        new_s.append(st_s)
    kv6 = [a.reshape(depth, n_p, s, 2, NSA_KV, HEAD_DIM) for a in kv_all]
    keep = min(WINDOW, s)
    p_out = [kv6[0], kv6[1], kv6[2][:, :, s - keep:], kv6[3]] + [jnp.stack([st[i] for st in new_p]) for i in range(2)]
    s_out = [jnp.stack([st[i] for st in new_s]) for i in range(6)]
    return (x_p.reshape(n_p, s, d), x_s.reshape(n_s, 1, d), *p_out, *s_out)
```

```python
import functools
import math

import numpy as np
import jax
import jax.numpy as jnp
from jax import lax
from jax.experimental import pallas as pl
from jax.experimental.pallas import tpu as pltpu

D_MODEL = 2048
HEAD_DIM = 128
BRANCH_W = D_MODEL // 2
N_BRANCH = 3
NSA_HEADS = BRANCH_W // HEAD_DIM
NSA_KV = 2
NSA_REP = NSA_HEADS // NSA_KV
CMP_STRIDE = 16
CMP_LEN = 2 * CMP_STRIDE
SEL_BLOCK = 64
N_SEL = 16
WINDOW = 512
DSA_HEADS = BRANCH_W // HEAD_DIM
DSA_KV = 2
IDX_HEADS = 8
IDX_DIM = 64
DSA_TOPK = 256
CONV_DIM = BRANCH_W
CONV_WIDTH = 3
FFN_HIDDEN = ((8 * D_MODEL + 3 * 256 - 1) // (3 * 256)) * 256
ROPE_THETA = 500000.0
ROPE_FRACTION = 4
RMS_EPS = 1e-6
ATTN_SCALE = HEAD_DIM ** -0.5
IDX_SCALE = IDX_DIM ** -0.5
NEG_INF = -1e30
FORCE_SCORE = 1e30
BELOW_ALL = -3.0e38
PAGE_SIZE = 128

LANE = 128
KV_W = 2 * NSA_KV * HEAD_DIM
CMP_FEAT = CMP_STRIDE * KV_W
VMEM_LIMIT = 60 * 1024 * 1024
INT_MIN = -(2 ** 31)

F32 = jnp.float32
BF16 = jnp.bfloat16


def _cp(sem, vmem=VMEM_LIMIT):
    return pltpu.CompilerParams(dimension_semantics=sem, vmem_limit_bytes=vmem)


def _pick(n, pref, mult=8):
    if n <= pref:
        return n
    for b in range(pref, 0, -1):
        if n % b == 0 and b % mult == 0:
            return b
    return n


def _sigmoid(x):
    return 1.0 / (1.0 + jnp.exp(-x))


def _rms(x, g):
    return x * lax.rsqrt(jnp.mean(x * x, axis=-1, keepdims=True) + RMS_EPS) * g


def _rmsnorm_kernel(x_ref, g_ref, o_ref):
    o_ref[...] = _rms(x_ref[...], g_ref[...]).astype(o_ref.dtype)


def rmsnorm(x, g, out_dtype=BF16):
    t, d = x.shape
    bm = _pick(t, 512)
    return pl.pallas_call(
        _rmsnorm_kernel,
        grid=(t // bm,),
        in_specs=[pl.BlockSpec((bm, d), lambda i: (i, 0)), pl.BlockSpec((1, d), lambda i: (0, 0))],
        out_specs=pl.BlockSpec((bm, d), lambda i: (i, 0)),
        out_shape=jax.ShapeDtypeStruct((t, d), out_dtype),
        compiler_params=_cp(("parallel",)),
        name="rmsnorm",
    )(x, g.reshape(1, d))


def _mm_rope_kernel(x_ref, w_ref, c_ref, s1_ref, s2_ref, o_ref, *, half, rope_blocks, interleave):
    y = jnp.dot(x_ref[...], w_ref[...], preferred_element_type=F32)
    bm = y.shape[0]
    nh = len(rope_blocks)
    c, s1, s2 = c_ref[...], s1_ref[...], s2_ref[...]
    for h, roped in enumerate(rope_blocks):
        yh = y[:, h * LANE:(h + 1) * LANE]
        if roped:
            yh = yh * c + pltpu.roll(yh, LANE - half, 1) * s1 + pltpu.roll(yh, half, 1) * s2
        if interleave:
            o_ref[pl.ds(h, bm, stride=nh), :] = yh
        else:
            o_ref[:, h * LANE:(h + 1) * LANE] = yh


def proj_rope(xn, w, tabs, *, bn, half, rope_blocks, stacked):
    t, k = xn.shape
    n = w.shape[1]
    tab_rows = tabs[0].shape[0]
    bm = _pick(math.gcd(t, tab_rows), 1024)
    tab_blocks = tab_rows // bm
    kern = functools.partial(_mm_rope_kernel, half=half, rope_blocks=rope_blocks, interleave=stacked)
    tab_spec = pl.BlockSpec((bm, LANE), lambda i, j: (i % tab_blocks, 0))
    if stacked:
        nh = bn // LANE
        out_shape = jax.ShapeDtypeStruct((n // bn, t * nh, LANE), F32)
        out_spec = pl.BlockSpec((None, bm * nh, LANE), lambda i, j: (j, i, 0))
    else:
        out_shape = jax.ShapeDtypeStruct((t, n), F32)
        out_spec = pl.BlockSpec((bm, bn), lambda i, j: (i, j))
    return pl.pallas_call(
        kern,
        grid=(t // bm, n // bn),
        in_specs=[pl.BlockSpec((bm, k), lambda i, j: (i, 0)), pl.BlockSpec((k, bn), lambda i, j: (0, j)),
                  tab_spec, tab_spec, tab_spec],
        out_specs=out_spec,
        out_shape=out_shape,
        compiler_params=_cp(("parallel", "arbitrary")),
        name="proj_rope",
    )(xn, w, *tabs)


def _mm_rope_kv_kernel(x_ref, w_ref, c_ref, s1_ref, s2_ref, *refs, half, n_kinds):
    outs = refs[n_kinds:]
    j = pl.program_id(1)
    y = jnp.dot(x_ref[...], w_ref[...], preferred_element_type=F32)
    bm = y.shape[0]
    c, s1, s2 = c_ref[...], s1_ref[...], s2_ref[...]
    slabs = []
    for h in range(KV_SLOTS):
        yh = y[:, h * LANE:(h + 1) * LANE]
        if h < NSA_KV:
            yh = yh * c + pltpu.roll(yh, LANE - half, 1) * s1 + pltpu.roll(yh, half, 1) * s2
        slabs.append(yh)
    for kind in range(n_kinds):
        @pl.when(j == kind)
        def _(kind=kind):
            for h, yh in enumerate(slabs):
                outs[kind][pl.ds(h, bm, stride=KV_SLOTS), :] = yh


def proj_rope_kv(xn, w, tabs, kv_all, layer, *, half):
    t, k = xn.shape
    n_kinds = len(kv_all)
    tab_rows = tabs[0].shape[0]
    bm = _pick(math.gcd(t, tab_rows), 1024)
    tab_blocks = tab_rows // bm
    tab_spec = pl.BlockSpec((bm, LANE), lambda i, j: (i % tab_blocks, 0))
    out_spec = pl.BlockSpec((None, bm * KV_SLOTS, LANE), lambda i, j: (layer, i, 0))
    return pl.pallas_call(
        functools.partial(_mm_rope_kv_kernel, half=half, n_kinds=n_kinds),
        grid=(t // bm, n_kinds),
        in_specs=[pl.BlockSpec((bm, k), lambda i, j: (i, 0)), pl.BlockSpec((k, KV_W), lambda i, j: (0, j)),
                  tab_spec, tab_spec, tab_spec] + [pl.BlockSpec(memory_space=pl.ANY)] * n_kinds,
        out_specs=[out_spec] * n_kinds,
        out_shape=[jax.ShapeDtypeStruct(a.shape, a.dtype) for a in kv_all],
        input_output_aliases={5 + i: i for i in range(n_kinds)},
        compiler_params=_cp(("parallel", "arbitrary")),
        name="proj_rope_kv",
    )(xn, w, *tabs, *kv_all)


def _mm_kernel(x_ref, w_ref, o_ref):
    o_ref[...] = jnp.dot(x_ref[...], w_ref[...], preferred_element_type=F32).astype(o_ref.dtype)


def matmul(x, w, *, bn, out_dtype=F32):
    t, k = x.shape
    n = w.shape[1]
    bm = _pick(t, 1024)
    return pl.pallas_call(
        _mm_kernel,
        grid=(t // bm, n // bn),
        in_specs=[pl.BlockSpec((bm, k), lambda i, j: (i, 0)), pl.BlockSpec((k, bn), lambda i, j: (0, j))],
        out_specs=pl.BlockSpec((bm, bn), lambda i, j: (i, j)),
        out_shape=jax.ShapeDtypeStruct((t, n), out_dtype),
        compiler_params=_cp(("parallel", "arbitrary")),
        name="proj_plain",
    )(x, w)


def _merge_kernel(a1_ref, a2_ref, a3_ref, ob_ref, oc_ref, w_ref, g0_ref, g1_ref, g2_ref, o_ref):
    xa = (a1_ref[...] + a2_ref[...] + a3_ref[...]).astype(BF16)
    acc = _sigmoid(g0_ref[...]) * jnp.dot(xa, w_ref[0], preferred_element_type=F32)
    acc += _sigmoid(g1_ref[...]) * jnp.dot(ob_ref[...].astype(BF16), w_ref[1], preferred_element_type=F32)
    acc += _sigmoid(g2_ref[...]) * jnp.dot(oc_ref[...], w_ref[2], preferred_element_type=F32)
    o_ref[...] = acc.astype(o_ref.dtype)


def _resident(shape, index_map):
    return pl.BlockSpec(shape, index_map, pipeline_mode=pl.Buffered(1))


def merge_branches(a1, a2, a3, ob, oc, wb, layer, cg):
    t = a1.shape[0]
    bm = _pick(t, 256)
    bn = D_MODEL
    nj = D_MODEL // bn
    xs = pl.BlockSpec((bm, BRANCH_W), lambda i, j: (i, 0))

    def gspec(br):
        return pl.BlockSpec((bm, bn), lambda i, j: (i, br * nj + j))

    return pl.pallas_call(
        _merge_kernel,
        grid=(t // bm, nj),
        in_specs=[xs, xs, xs, xs, xs,
                  _resident((None, N_BRANCH, BRANCH_W, bn), lambda i, j: (layer, 0, 0, j)),
                  gspec(0), gspec(1), gspec(2)],
        out_specs=pl.BlockSpec((bm, bn), lambda i, j: (i, j)),
        out_shape=jax.ShapeDtypeStruct((t, D_MODEL), BF16),
        compiler_params=_cp(("parallel", "arbitrary")),
        name="merge",
    )(a1, a2, a3, ob, oc, wb, cg, cg, cg)


def _outproj_kernel(m_ref, w_ref, x_ref, gpost_ref, gpre_ref, xo_ref, hn_ref):
    y = jnp.dot(m_ref[...], w_ref[...], preferred_element_type=F32)
    xn = x_ref[...] + _rms(y, gpost_ref[...])
    xo_ref[...] = xn
    hn_ref[...] = _rms(xn, gpre_ref[...]).astype(hn_ref.dtype)


def outproj_residual(merged, w_out, layer, x, g_post, g_ffn_pre):
    t, d = x.shape
    bm = _pick(t, 512)
    row = pl.BlockSpec((bm, d), lambda i: (i, 0))
    vec = pl.BlockSpec((1, d), lambda i: (0, 0))
    return pl.pallas_call(
        _outproj_kernel,
        grid=(t // bm,),
        in_specs=[row, _resident((None, d, d), lambda i: (layer, 0, 0)), row, vec, vec],
        out_specs=[row, row],
        out_shape=[jax.ShapeDtypeStruct((t, d), F32), jax.ShapeDtypeStruct((t, d), BF16)],
        compiler_params=_cp(("parallel",)),
        name="outproj",
    )(merged, w_out, x, g_post.reshape(1, d), g_ffn_pre.reshape(1, d))


def _ffn_gu_kernel(h_ref, wg_ref, wu_ref, o_ref):
    h = h_ref[...]
    g = jnp.dot(h, wg_ref[...], preferred_element_type=F32)
    u = jnp.dot(h, wu_ref[...], preferred_element_type=F32)
    o_ref[...] = (g * _sigmoid(g) * u).astype(o_ref.dtype)


def ffn_gate_up(hn, w_gu, layer):
    t, d = hn.shape
    bm = _pick(t, 1024)
    bn = 512
    nj = FFN_HIDDEN // bn
    return pl.pallas_call(
        _ffn_gu_kernel,
        grid=(t // bm, nj),
        in_specs=[pl.BlockSpec((bm, d), lambda i, j: (i, 0)),
                  pl.BlockSpec((None, d, bn), lambda i, j: (layer, 0, j)),
                  pl.BlockSpec((None, d, bn), lambda i, j: (layer, 0, nj + j))],
        out_specs=pl.BlockSpec((bm, bn), lambda i, j: (i, j)),
        out_shape=jax.ShapeDtypeStruct((t, FFN_HIDDEN), BF16),
        compiler_params=_cp(("parallel", "arbitrary")),
        name="ffn_gate_up",
    )(hn, w_gu, w_gu)


def _ffn_down_kernel(a_ref, w_ref, x_ref, gpost_ref, gnext_ref, y_ref, xn_ref):
    y = x_ref[...] + _rms(jnp.dot(a_ref[...], w_ref[...], preferred_element_type=F32), gpost_ref[...])
    y_ref[...] = y
    xn_ref[...] = _rms(y, gnext_ref[...]).astype(xn_ref.dtype)


def ffn_down_residual(act, w_down, layer, x, g_post, g_next):
    t, d = x.shape
    f = act.shape[1]
    bm = _pick(t, 256)
    row = pl.BlockSpec((bm, d), lambda i: (i, 0))
    vec = pl.BlockSpec((1, d), lambda i: (0, 0))
    return pl.pallas_call(
        _ffn_down_kernel,
        grid=(t // bm,),
        in_specs=[pl.BlockSpec((bm, f), lambda i: (i, 0)), _resident((None, f, d), lambda i: (layer, 0, 0)),
                  row, vec, vec],
        out_specs=[row, row],
        out_shape=[jax.ShapeDtypeStruct((t, d), F32), jax.ShapeDtypeStruct((t, d), BF16)],
        compiler_params=_cp(("parallel",)),
        name="ffn_down",
    )(act, w_down, x, g_post.reshape(1, d), g_next.reshape(1, d))


def _conv_kernel(u_ref, b_ref, c_ref, w_ref, o_ref, st_ref):
    v = c_ref[...] * u_ref[...]
    s = v.shape[0]
    row = lax.broadcasted_iota(jnp.int32, v.shape, 0)
    v1 = jnp.where(row >= 1, pltpu.roll(v, 1, 0), 0.0)
    v2 = jnp.where(row >= 2, pltpu.roll(v, 2, 0), 0.0)
    w = w_ref[...]
    y = w[0:1] * v2 + w[1:2] * v1 + w[2:3] * v
    o_ref[...] = (b_ref[...] * y).astype(o_ref.dtype)
    st_ref[...] = v[s - (CONV_WIDTH - 1):, :]


def conv_prompt(cg, conv_w, n, s):
    bc = 256
    nj = CONV_DIM // bc
    base = N_BRANCH * D_MODEL // bc
    return pl.pallas_call(
        _conv_kernel,
        grid=(n, nj),
        in_specs=[pl.BlockSpec((s, bc), lambda b, j: (b, base + j)),
                  pl.BlockSpec((s, bc), lambda b, j: (b, base + nj + j)),
                  pl.BlockSpec((s, bc), lambda b, j: (b, base + 2 * nj + j)),
                  pl.BlockSpec((CONV_WIDTH, bc), lambda b, j: (0, j))],
        out_specs=[pl.BlockSpec((s, bc), lambda b, j: (b, j)),
                   pl.BlockSpec((None, CONV_WIDTH - 1, bc), lambda b, j: (b, 0, j))],
        out_shape=[jax.ShapeDtypeStruct((n * s, CONV_DIM), BF16),
                   jax.ShapeDtypeStruct((n, CONV_WIDTH - 1, CONV_DIM), F32)],
        compiler_params=_cp(("parallel", "arbitrary")),
        name="conv_prompt",
    )(cg, cg, cg, conv_w)


def _conv_dec_kernel(u_ref, b_ref, c_ref, buf_ref, w_ref, o_ref, st_ref):
    v = c_ref[...] * u_ref[...]
    b0 = buf_ref[:, 0, :]
    b1 = buf_ref[:, 1, :]
    w = w_ref[...]
    y = w[0:1] * b0 + w[1:2] * b1 + w[2:3] * v
    o_ref[...] = (b_ref[...] * y).astype(o_ref.dtype)
    st_ref[:, 0, :] = b1
    st_ref[:, 1, :] = v


def conv_decode(cg, buf, conv_w):
    n = cg.shape[0]
    base = N_BRANCH * D_MODEL // CONV_DIM
    blk = lambda j: pl.BlockSpec((n, CONV_DIM), lambda i: (0, base + j))
    full3 = pl.BlockSpec((n, CONV_WIDTH - 1, CONV_DIM), lambda i: (0, 0, 0))
    return pl.pallas_call(
        _conv_dec_kernel,
        grid=(1,),
        in_specs=[blk(0), blk(1), blk(2), full3, pl.BlockSpec((CONV_WIDTH, CONV_DIM), lambda i: (0, 0))],
        out_specs=[pl.BlockSpec((n, CONV_DIM), lambda i: (0, 0)), full3],
        out_shape=[jax.ShapeDtypeStruct((n, CONV_DIM), BF16),
                   jax.ShapeDtypeStruct((n, CONV_WIDTH - 1, CONV_DIM), F32)],
        compiler_params=_cp(("arbitrary",)),
        name="conv_decode",
    )(cg, cg, cg, buf, conv_w)


def _compress(read_x, w1_ref, w2_ref, pe_ref, nc):
    out = []
    for kv in range(2):
        acc = jnp.zeros((NSA_KV * nc, 2 * HEAD_DIM), F32)
        bias = jnp.zeros((8, 2 * HEAD_DIM), F32)
        for tp in range(CMP_STRIDE // 2):
            xt = jnp.concatenate(
                [jnp.concatenate([read_x(2 * tp + u, kv * NSA_KV + g) for u in range(2)], axis=1)
                 for g in range(NSA_KV)], axis=0).astype(BF16)
            wt = w1_ref[kv, tp]
            acc += jnp.dot(xt, wt, preferred_element_type=F32)
            bias += jnp.dot(pe_ref[kv, tp], wt, preferred_element_type=F32)
        pe_bias = bias[0:1, :HEAD_DIM] + bias[1:2, HEAD_DIM:]
        per_group = []
        for g in range(NSA_KV):
            a = acc[g * nc:(g + 1) * nc, :HEAD_DIM]
            b = acc[g * nc:(g + 1) * nc, HEAD_DIM:]
            pre = a + pltpu.roll(b, nc - 1, 0) + pe_bias
            hid = pre * _sigmoid(pre)
            per_group.append(jnp.dot(hid.astype(BF16), w2_ref[kv], preferred_element_type=F32))
        out.append(per_group)
    return out[0], out[1]


def _gate_rows(gates_blk):
    return gates_blk.T


def _stack_heads(q_ref, g, bq):
    return jnp.concatenate(
        [q_ref[:, (g * NSA_REP + r) * HEAD_DIM:(g * NSA_REP + r + 1) * HEAD_DIM] for r in range(NSA_REP)],
        axis=0).astype(BF16)


KV_SLOTS = 2 * NSA_KV
CHUNK_ROWS = CMP_STRIDE * KV_SLOTS
CHUNK_PITCH = CHUNK_ROWS + 8


def _kv_rows(kv_ref, k0, n, slot):
    return kv_ref[pl.ds(k0 * KV_SLOTS + slot, n, stride=KV_SLOTS), :]


EXP2_SCALE = ATTN_SCALE * math.log2(math.e)
FLASH_KC = 256


def _flash_scratch(bq):
    nq = NSA_REP * bq
    return [pltpu.VMEM((NSA_KV, nq, HEAD_DIM), BF16), pltpu.VMEM((NSA_KV, 1, nq), F32),
            pltpu.VMEM((NSA_KV, 1, nq), F32), pltpu.VMEM((NSA_KV, HEAD_DIM, nq), F32),
            pltpu.VMEM((NSA_KV, FLASH_KC, nq), F32), pltpu.VMEM((NSA_KV, FLASH_KC, nq), F32)]


def _flash_t(st, q_ref, kv_ref, o_ref, lo, hi, kc, bias_fn, bq, gate_t, branch):
    qs_ref, m_ref, l_ref, acc_ref, s_even, s_odd = st
    for g in range(NSA_KV):
        qs_ref[g] = _stack_heads(q_ref, g, bq)
    m_ref[...] = jnp.full(m_ref.shape, NEG_INF, F32)
    l_ref[...] = jnp.zeros(l_ref.shape, F32)
    acc_ref[...] = jnp.zeros(acc_ref.shape, F32)

    def scores(c, s_ref):
        k0 = pl.multiple_of(c * kc, kc)
        for g in range(NSA_KV):
            kb = _kv_rows(kv_ref, k0, kc, g).astype(BF16)
            s = lax.dot_general(kb, qs_ref[g], (((1,), (1,)), ((), ())), preferred_element_type=F32)
            s_ref[g] = s + jnp.concatenate([bias_fn(k0, g)] * NSA_REP, axis=1)

    def step(c, s_cur, s_nxt):
        scores(jnp.minimum(c + 1, hi - 1), s_nxt)
        k0 = pl.multiple_of(c * kc, kc)
        for g in range(NSA_KV):
            vb = _kv_rows(kv_ref, k0, kc, NSA_KV + g).astype(BF16)
            s = s_cur[g]
            m = m_ref[g]
            m_new = jnp.maximum(m, jnp.max(s, axis=0, keepdims=True))
            alpha = jnp.exp2((m - m_new) * EXP2_SCALE)
            p = jnp.exp2((s - m_new) * EXP2_SCALE)
            m_ref[g] = m_new
            l_ref[g] = alpha * l_ref[g] + jnp.sum(p, axis=0, keepdims=True)
            pv = lax.dot_general(vb, p.astype(BF16), (((0,), (0,)), ((), ())), preferred_element_type=F32)
            acc_ref[g] = alpha * acc_ref[g] + pv

    scores(lo, s_even)
    n = hi - lo

    def pair(i, _):
        c = lo + 2 * i
        step(c, s_even, s_odd)
        step(c + 1, s_odd, s_even)
        return 0

    lax.fori_loop(0, n // 2, pair, 0)

    @pl.when(n % 2 == 1)
    def _():
        step(hi - 1, s_even, s_odd)
    for g in range(NSA_KV):
        ot = jnp.where(m_ref[g] > 0.5 * NEG_INF, acc_ref[g] / jnp.maximum(l_ref[g], 1e-30), 0.0)
        _store_heads(o_ref, [ot[:, r * bq:(r + 1) * bq] for r in range(NSA_REP)], g, gate_t, branch)


def _store_heads(o_ref, heads, g, gate_t, branch):
    for r, oh in enumerate(heads):
        h = g * NSA_REP + r
        if gate_t is not None:
            oh = oh * _sigmoid(gate_t[h * 3 + branch:h * 3 + branch + 1, :])
        o_ref[:, h * HEAD_DIM:(h + 1) * HEAD_DIM] = oh.T


def _cmp_prompt_kernel(x_ref, q_ref, gates_ref, w1_ref, w2_ref, pe_ref, ov_ref, o_ref, ps_ref, kc_ref, vc_ref,
                       *, nc, bq):
    qi = pl.program_id(1)

    @pl.when(qi == 0)
    def _():
        def read_x(t, kg):
            return x_ref[pl.ds(t * KV_SLOTS + kg, nc, stride=CMP_STRIDE * KV_SLOTS), :]
        k_c, v_c = _compress(read_x, w1_ref, w2_ref, pe_ref, nc)
        for g in range(NSA_KV):
            kc_ref[g] = k_c[g].astype(BF16)
            vc_ref[g] = v_c[g].astype(BF16)

    q0 = qi * bq
    nq = NSA_REP * bq
    gate_t = _gate_rows(gates_ref[...])
    pos = q0 + lax.broadcasted_iota(jnp.int32, (nc, bq), 1)
    blk_end = lax.broadcasted_iota(jnp.int32, (nc, bq), 0) * CMP_STRIDE + (CMP_LEN - 1)
    mk1 = blk_end <= pos
    mk = jnp.concatenate([mk1] * NSA_REP, axis=1)
    for g in range(NSA_KV):
        qs = _stack_heads(q_ref, g, bq)
        s = lax.dot_general(kc_ref[g], qs, (((1,), (1,)), ((), ())), preferred_element_type=F32) * ATTN_SCALE
        s = jnp.where(mk, s, NEG_INF)
        m = jnp.max(s, axis=0, keepdims=True)
        e = jnp.where(mk, jnp.exp(s - m), 0.0)
        p = e / jnp.maximum(jnp.sum(e, axis=0, keepdims=True), 1e-30)
        ot = lax.dot_general(vc_ref[g], p.astype(BF16), (((0,), (0,)), ((), ())), preferred_element_type=F32)
        _store_heads(o_ref, [ot[:, r * bq:(r + 1) * bq] for r in range(NSA_REP)], g, gate_t, 0)
        psum = p[:, 0:bq]
        for r in range(1, NSA_REP):
            psum = psum + p[:, r * bq:(r + 1) * bq]
        ps_ref[g] = jnp.dot(ov_ref[...], psum, precision=lax.Precision.HIGHEST, preferred_element_type=F32)


def cmp_prompt(cmp_x, layer, q, idxm, w1cat, w2, pe8, ov_t, n, s):
    nc = s // CMP_STRIDE
    bq = 256
    nq = s // bq
    n_slc = ov_t.shape[0]
    t = n * s
    full = lambda a: pl.BlockSpec(a.shape, lambda b, i: (0,) * a.ndim)
    return pl.pallas_call(
        functools.partial(_cmp_prompt_kernel, nc=nc, bq=bq),
        grid=(n, nq),
        in_specs=[pl.BlockSpec((None, s * KV_SLOTS, HEAD_DIM), lambda b, i: (layer, b, 0)),
                  pl.BlockSpec((bq, BRANCH_W), lambda b, i: (b * nq + i, 0)),
                  pl.BlockSpec((bq, LANE), lambda b, i: (b * nq + i, 5)),
                  full(w1cat), full(w2), full(pe8), full(ov_t)],
        out_specs=[pl.BlockSpec((bq, BRANCH_W), lambda b, i: (b * nq + i, 0)),
                   pl.BlockSpec((NSA_KV, n_slc, bq), lambda b, i: (0, 0, b * nq + i))],
        out_shape=[jax.ShapeDtypeStruct((t, BRANCH_W), F32), jax.ShapeDtypeStruct((NSA_KV, n_slc, t), F32)],
        scratch_shapes=[pltpu.VMEM((NSA_KV, nc, HEAD_DIM), BF16), pltpu.VMEM((NSA_KV, nc, HEAD_DIM), BF16)],
        compiler_params=_cp(("arbitrary", "arbitrary")),
        name="nsa_cmp_prompt",
    )(cmp_x, q, idxm, w1cat, w2, pe8, ov_t)


def _topk_rank_rows(sc, n_rows, k):
    j = lax.broadcasted_iota(jnp.int32, sc.shape, 0)
    rank = jnp.zeros(sc.shape, jnp.int32)
    for i in range(n_rows):
        si = sc[i:i + 1, :]
        beats = jnp.where(si > sc, 1, jnp.where((si == sc) & (j > i), 1, 0))
        rank = rank + beats
    return jnp.where(rank < k, 1.0, 0.0)


def _slc_prompt_kernel(q_ref, kv_ref, ps_ref, gates_ref, o_ref, bias_ref, *st, n_slc, bq, kc):
    qi = pl.program_id(1)
    q0 = qi * bq
    gate_t = _gate_rows(gates_ref[...])
    blk = lax.broadcasted_iota(jnp.int32, (n_slc, bq), 0)
    pos = q0 + lax.broadcasted_iota(jnp.int32, (n_slc, bq), 1)
    cur = pos // SEL_BLOCK
    forced = (blk == 0) | (blk == cur) | (blk == cur - 1)
    visible = blk * SEL_BLOCK <= pos
    qpos = q0 + lax.broadcasted_iota(jnp.int32, (SEL_BLOCK, bq), 1)
    krow = lax.broadcasted_iota(jnp.int32, (SEL_BLOCK, bq), 0)
    for g in range(NSA_KV):
        sc = jnp.where(forced, FORCE_SCORE, jnp.where(visible, ps_ref[g], NEG_INF))
        sel = _topk_rank_rows(sc, n_slc, min(N_SEL, n_slc))
        for j in range(n_slc):
            keep = (jnp.broadcast_to(sel[j:j + 1, :], (SEL_BLOCK, bq)) > 0.5) & (j * SEL_BLOCK + krow <= qpos)
            bias_ref[g, j * SEL_BLOCK:(j + 1) * SEL_BLOCK, :] = jnp.where(keep, 0.0, NEG_INF)

    _flash_t(st, q_ref, kv_ref, o_ref, 0, (q0 + bq + kc - 1) // kc, kc,
             lambda k0, g: bias_ref[g, pl.ds(k0, kc), :], bq, gate_t, 1)


def slc_prompt(q, kv4, layer, p_slc, idxm, n, s):
    bq, kc = 256, FLASH_KC
    nq = s // bq
    n_slc = p_slc.shape[1]
    t = n * s
    return pl.pallas_call(
        functools.partial(_slc_prompt_kernel, n_slc=n_slc, bq=bq, kc=kc),
        grid=(n, nq),
        in_specs=[pl.BlockSpec((bq, BRANCH_W), lambda b, i: (b * nq + i, 0)),
                  pl.BlockSpec((None, s * KV_SLOTS, HEAD_DIM), lambda b, i: (layer, b, 0)),
                  pl.BlockSpec((NSA_KV, n_slc, bq), lambda b, i: (0, 0, b * nq + i)),
                  pl.BlockSpec((bq, LANE), lambda b, i: (b * nq + i, 5))],
        out_specs=pl.BlockSpec((bq, BRANCH_W), lambda b, i: (b * nq + i, 0)),
        out_shape=jax.ShapeDtypeStruct((t, BRANCH_W), F32),
        scratch_shapes=[pltpu.VMEM((NSA_KV, s, bq), F32)] + _flash_scratch(bq),
        compiler_params=_cp(("parallel", "arbitrary")),
        name="nsa_slc_prompt",
    )(q, kv4, p_slc, idxm)


def _win_prompt_kernel(q_ref, kv_ref, gates_ref, o_ref, *st, bq, kc):
    qi = pl.program_id(1)
    q0 = qi * bq
    gate_t = _gate_rows(gates_ref[...])
    qpos = q0 + lax.broadcasted_iota(jnp.int32, (kc, bq), 1)
    krow = lax.broadcasted_iota(jnp.int32, (kc, bq), 0)

    def bias_fn(k0, g):
        rel = qpos - (k0 + krow)
        return jnp.where((rel >= 0) & (rel <= WINDOW), 0.0, NEG_INF)

    lo = jnp.maximum(q0 - WINDOW, 0) // kc
    hi = (q0 + bq + kc - 1) // kc
    _flash_t(st, q_ref, kv_ref, o_ref, lo, hi, kc, bias_fn, bq, gate_t, 2)


def win_prompt(q, kv4, layer, idxm, n, s):
    bq, kc = 256, FLASH_KC
    nq = s // bq
    t = n * s
    return pl.pallas_call(
        functools.partial(_win_prompt_kernel, bq=bq, kc=kc),
        grid=(n, nq),
        in_specs=[pl.BlockSpec((bq, BRANCH_W), lambda b, i: (b * nq + i, 0)),
                  pl.BlockSpec((None, s * KV_SLOTS, HEAD_DIM), lambda b, i: (layer, b, 0)),
                  pl.BlockSpec((bq, LANE), lambda b, i: (b * nq + i, 5))],
        out_specs=pl.BlockSpec((bq, BRANCH_W), lambda b, i: (b * nq + i, 0)),
        out_shape=jax.ShapeDtypeStruct((t, BRANCH_W), F32),
        scratch_shapes=_flash_scratch(bq),
        compiler_params=_cp(("parallel", "arbitrary")),
        name="nsa_win_prompt",
    )(q, kv4, idxm)


def _order_key(x):
    b = pltpu.bitcast(x + 0.0, jnp.int32)
    return b ^ ((b >> 31) & jnp.int32(0x7FFFFFFF))


def _radix_kth(count_ge, k, shape):
    zero = jnp.zeros(shape, jnp.int32)
    base = jnp.where(count_ge(zero) >= k, zero, jnp.full(shape, INT_MIN, jnp.int32))

    def body(i, base):
        cand = base | jnp.left_shift(jnp.int32(1), 30 - i)
        return jnp.where(count_ge(cand) >= k, cand, base)

    return lax.fori_loop(0, 31, body, base)


def _tie_cut(count_eq_below, need, shape, n_bits):
    def body(i, m):
        cand = m | jnp.left_shift(jnp.int32(1), n_bits - 1 - i)
        return jnp.where(count_eq_below(cand) <= need, cand, m)

    return lax.fori_loop(0, n_bits, body, jnp.zeros(shape, jnp.int32))


def _dsa_prompt_kernel(q_ref, qi_ref, kidx_ref, kv_ref, gates_ref, o_ref, key_ref, bias_ref, *st,
                       bq, kc, kf, s_len):
    qi = pl.program_id(1)
    q0 = qi * bq
    n_chunks = (q0 + bq + kc - 1) // kc
    gate_t = _gate_rows(gates_ref[...])
    n_tiles = IDX_HEADS // 2
    w_scale = IDX_HEADS ** -0.5 * IDX_SCALE
    w_rows = [jnp.concatenate([gate_t[24 + 2 * t + odd:25 + 2 * t + odd, :] for t in range(n_tiles)], axis=1) * w_scale
              for odd in range(2)]
    q_tiles = jnp.concatenate([qi_ref[:, t * LANE:(t + 1) * LANE] for t in range(n_tiles)], axis=0).astype(BF16)
    qpos = q0 + lax.broadcasted_iota(jnp.int32, (kc, bq), 1)
    krow = lax.broadcasted_iota(jnp.int32, (kc, bq), 0)

    def score_body(c, _):
        k0 = pl.multiple_of(c * kc, kc)
        kb = kidx_ref[pl.ds(k0, kc), :]
        sc = jnp.zeros((kc, bq), F32)
        for odd in range(2):
            kh = (pltpu.roll(kb, IDX_DIM, 1) if odd else kb).astype(BF16)
            lg = lax.dot_general(kh, q_tiles, (((1,), (1,)), ((), ())), preferred_element_type=F32)
            wl = jnp.maximum(lg, 0.0) * w_rows[odd]
            for t in range(n_tiles):
                sc = sc + wl[:, t * bq:(t + 1) * bq]
        sc = jnp.where(k0 + krow <= qpos, sc, NEG_INF)
        key_ref[pl.ds(k0, kc), :] = _order_key(sc)
        return 0

    lax.fori_loop(0, n_chunks, score_body, 0)

    def count(pred):
        def body(c, acc):
            k0 = pl.multiple_of(c * kc, kc)
            hit = jnp.where(pred(key_ref[pl.ds(k0, kc), :], k0 + krow), 1, 0)
            return acc + jnp.sum(hit.reshape(kc // 8, 8, bq), axis=0)
        acc = lax.fori_loop(0, n_chunks, body, jnp.zeros((8, bq), jnp.int32))
        return jnp.sum(acc, axis=0, keepdims=True)

    k_top = min(DSA_TOPK, s_len // 4)
    row1 = (1, bq)

    def select(_):
        thr = _radix_kth(lambda cand: count(lambda key, idx: key >= cand), k_top, row1)
        need = k_top - count(lambda key, idx: key > thr)
        n_eq = count(lambda key, idx: key == thr)
        n_bits = max(1, int(s_len).bit_length())
        cut = lax.cond(
            jnp.any(n_eq != need),
            lambda _: _tie_cut(lambda m: count(lambda key, idx: (key == thr) & (idx < m)), need, row1, n_bits),
            lambda _: jnp.full(row1, s_len, jnp.int32), 0)
        return thr, cut

    thr, cut = lax.cond(q0 + bq > k_top, select,
                        lambda _: (jnp.full(row1, INT_MIN, jnp.int32), jnp.full(row1, s_len, jnp.int32)), 0)

    def mask_body(c, _):
        k0 = pl.multiple_of(c * kc, kc)
        key = key_ref[pl.ds(k0, kc), :]
        idx = k0 + krow
        sel = ((key > thr) | ((key == thr) & (idx < cut))) & (idx <= qpos)
        bias_ref[pl.ds(k0, kc), :] = jnp.where(sel, 0.0, NEG_INF)
        return 0

    lax.fori_loop(0, n_chunks, mask_body, 0)

    _flash_t(st, q_ref, kv_ref, o_ref, 0, (q0 + bq + kf - 1) // kf, kf,
             lambda k0, g: bias_ref[pl.ds(k0, kf), :], bq, None, 0)


def dsa_prompt(q, kv4, layer, idxm, n, s):
    bq = kc = 256
    nq = s // bq
    t = n * s
    return pl.pallas_call(
        functools.partial(_dsa_prompt_kernel, bq=bq, kc=kc, kf=FLASH_KC, s_len=s),
        grid=(n, nq),
        in_specs=[pl.BlockSpec((bq, BRANCH_W), lambda b, i: (b * nq + i, 1)),
                  pl.BlockSpec((bq, IDX_HEADS * IDX_DIM), lambda b, i: (b * nq + i, 0)),
                  pl.BlockSpec((s, LANE), lambda b, i: (b, 4)),
                  pl.BlockSpec((None, s * KV_SLOTS, HEAD_DIM), lambda b, i: (layer, b, 0)),
                  pl.BlockSpec((bq, LANE), lambda b, i: (b * nq + i, 5))],
        out_specs=pl.BlockSpec((bq, BRANCH_W), lambda b, i: (b * nq + i, 0)),
        out_shape=jax.ShapeDtypeStruct((t, BRANCH_W), F32),
        scratch_shapes=[pltpu.VMEM((s, bq), jnp.int32), pltpu.VMEM((s, bq), F32)] + _flash_scratch(bq),
        compiler_params=_cp(("parallel", "arbitrary")),
        name="dsa_prompt",
    )(q, idxm, idxm, kv4, idxm)


def _page_copies(pt_ref, cache_ref, layer, buf_ref, sem_ref, seq, slot, n_pages, rows, mode, need_ref=None):
    def each(fn):
        def copy_page(p):
            page = pt_ref[seq, p]
            if mode == "chunks":
                for cc in range(rows):
                    dst = buf_ref.at[slot, pl.ds((p * rows + cc) * CHUNK_PITCH, CHUNK_ROWS)]
                    fn(pltpu.make_async_copy(cache_ref.at[layer, page, cc], dst, sem_ref.at[slot]), cc % 2)
                return
            win = pl.ds(p * rows, rows)
            dst = buf_ref.at[slot, :, win] if mode == "lanes" else buf_ref.at[slot, win]
            fn(pltpu.make_async_copy(cache_ref.at[layer, page], dst, sem_ref.at[slot]), 0)

        def body(p, _):
            if need_ref is None:
                copy_page(p)
            else:
                pl.when(need_ref[seq, p] > 0)(lambda: copy_page(p))
            return 0
        lax.fori_loop(0, n_pages, body, 0)
    return each


def _gather_step(pt_ref, cache_ref, layer, buf_ref, sem_ref, n_pages, rows, mode="rows", need_ref=None):
    b = pl.program_id(0)
    nb = pl.num_programs(0)
    slot = b % 2
    copies = functools.partial(_page_copies, pt_ref, cache_ref, layer, buf_ref, sem_ref,
                               n_pages=n_pages, rows=rows, mode=mode, need_ref=need_ref)

    @pl.when(b == 0)
    def _():
        if need_ref is not None:
            buf_ref[...] = jnp.zeros(buf_ref.shape, buf_ref.dtype)
        copies(seq=0, slot=0)(lambda cp, prio: cp.start(priority=prio))

    @pl.when(b + 1 < nb)
    def _():
        copies(seq=b + 1, slot=1 - slot)(lambda cp, prio: cp.start(priority=prio))

    copies(seq=b, slot=slot)(lambda cp, prio: cp.wait())
    return slot


def _head_column(row, offset, stride):
    lane = lax.broadcasted_iota(jnp.int32, (8, LANE), 1)
    h = lax.broadcasted_iota(jnp.int32, (8, LANE), 0)
    return jnp.sum(jnp.where(lane == offset + stride * h, jnp.broadcast_to(row, (8, LANE)), 0.0), axis=1, keepdims=True)


def _q8(q_ref, width):
    return jnp.concatenate([q_ref[0:1, h * width:(h + 1) * width] for h in range(8)], axis=0)


def _row_spec(width, blk, n_extra):
    del n_extra
    return pl.BlockSpec((None, 1, width), lambda b, *_: (b, 0, blk))


def _cmp_dec_kernel(pt_ref, cache_ref, q_ref, gates_ref, w1_ref, w2_ref, pe_ref, ov_ref, o_ref, ps_ref,
                    buf_ref, sem_ref, *, layer, n_pages, p0):
    nc = n_pages * PAGE_SIZE // CMP_STRIDE
    slot = _gather_step(pt_ref, cache_ref, layer, buf_ref, sem_ref, n_pages, PAGE_SIZE // CMP_STRIDE,
                        mode="chunks")

    def read_x(t, kg):
        return buf_ref[slot, pl.ds(t * KV_SLOTS + kg, nc, stride=CHUNK_PITCH), :]

    k_c, v_c = _compress(read_x, w1_ref, w2_ref, pe_ref, nc)
    q8 = _q8(q_ref, HEAD_DIM).astype(BF16)
    head = lax.broadcasted_iota(jnp.int32, (8, 1), 0)
    blk_end = lax.broadcasted_iota(jnp.int32, (8, nc), 1) * CMP_STRIDE + (CMP_LEN - 1)
    mk = blk_end <= p0
    s = jnp.zeros((8, nc), F32)
    for g in range(NSA_KV):
        sg = lax.dot_general(q8, k_c[g].astype(BF16), (((1,), (1,)), ((), ())), preferred_element_type=F32)
        s = jnp.where(head // NSA_REP == g, sg, s)
    s = jnp.where(mk, s * ATTN_SCALE, NEG_INF)
    m = jnp.max(s, axis=1, keepdims=True)
    e = jnp.where(mk, jnp.exp(s - m), 0.0)
    p = e / jnp.maximum(jnp.sum(e, axis=1, keepdims=True), 1e-30)
    o = jnp.zeros((8, HEAD_DIM), F32)
    for g in range(NSA_KV):
        og = jnp.dot(p.astype(BF16), v_c[g].astype(BF16), preferred_element_type=F32)
        o = jnp.where(head // NSA_REP == g, og, o)
        psum = jnp.sum(jnp.where(head // NSA_REP == g, p, 0.0), axis=0, keepdims=True)
        ps8 = jnp.dot(jnp.broadcast_to(psum, (8, nc)), ov_ref[...], precision=lax.Precision.HIGHEST,
                      preferred_element_type=F32)
        ps_ref[g:g + 1, :] = ps8[0:1, :]
    gate = _sigmoid(_head_column(gates_ref[...], 0, 3))
    o_ref[...] = o * gate


def cmp_decode(page_table, cache_x, layer, q, idxm, w1cat, w2, pe8, ov, p0):
    n, n_pages = page_table.shape
    n_slc_pad = ov.shape[1]
    full = lambda a: pl.BlockSpec(a.shape, lambda b, pt: (0,) * a.ndim)
    gs = pltpu.PrefetchScalarGridSpec(
        num_scalar_prefetch=1,
        grid=(n,),
        in_specs=[pl.BlockSpec(memory_space=pl.ANY), _row_spec(BRANCH_W, 0, 1), _row_spec(LANE, 5, 1),
                  full(w1cat), full(w2), full(pe8), full(ov)],
        out_specs=[pl.BlockSpec((None, NSA_HEADS, HEAD_DIM), lambda b, pt: (b, 0, 0)),
                   pl.BlockSpec((None, NSA_KV, n_slc_pad), lambda b, pt: (b, 0, 0))],
        scratch_shapes=[pltpu.VMEM((2, n_pages * PAGE_SIZE // CMP_STRIDE * CHUNK_PITCH, HEAD_DIM), F32),
                        pltpu.SemaphoreType.DMA((2,))],
    )
    return pl.pallas_call(
        functools.partial(_cmp_dec_kernel, layer=layer, n_pages=n_pages, p0=p0),
        grid_spec=gs,
        out_shape=[jax.ShapeDtypeStruct((n, NSA_HEADS, HEAD_DIM), F32),
                   jax.ShapeDtypeStruct((n, NSA_KV, n_slc_pad), F32)],
        compiler_params=_cp(("arbitrary",)),
        name="nsa_cmp_decode",
    )(page_table, cache_x, q, idxm, w1cat, w2, pe8, ov)


def _slc_mask_kernel(ps_ref, e_ref, pg_ref, o_ref, need_ref, *, n_slc, p0):
    sc = ps_ref[...]
    j = lax.broadcasted_iota(jnp.int32, sc.shape, 1)
    cur = p0 // SEL_BLOCK
    forced = (j == 0) | (j == cur) | (j == cur - 1)
    visible = j * SEL_BLOCK <= p0
    sc = jnp.where(j >= n_slc, BELOW_ALL, jnp.where(forced, FORCE_SCORE, jnp.where(visible, sc, NEG_INF)))
    rank = jnp.zeros(sc.shape, jnp.int32)
    for i in range(n_slc):
        si = sc[:, i:i + 1]
        rank = rank + jnp.where(si > sc, 1, jnp.where((si == sc) & (j > i), 1, 0))
    sel = jnp.where((rank < min(N_SEL, n_slc)) & (j < n_slc), 1.0, 0.0).astype(BF16)
    o_ref[...] = jnp.dot(sel, e_ref[...], preferred_element_type=F32)
    need_ref[...] = jnp.dot(sel, pg_ref[...], preferred_element_type=F32)


def slc_mask_decode(p_slc, expand, block_page, n_slc, n_pages, p0):
    n, g, w = p_slc.shape
    l_pad = expand.shape[1]
    assert n_pages <= LANE, "one lane per cache page in the page-need output"
    out, need = pl.pallas_call(
        functools.partial(_slc_mask_kernel, n_slc=n_slc, p0=p0),
        grid=(1,),
        in_specs=[pl.BlockSpec((n * g, w), lambda i: (0, 0)), pl.BlockSpec(expand.shape, lambda i: (0, 0)),
                  pl.BlockSpec(block_page.shape, lambda i: (0, 0))],
        out_specs=[pl.BlockSpec((n * g, l_pad), lambda i: (0, 0)), pl.BlockSpec((n * g, LANE), lambda i: (0, 0))],
        out_shape=[jax.ShapeDtypeStruct((n * g, l_pad), F32), jax.ShapeDtypeStruct((n * g, LANE), F32)],
        compiler_params=_cp(("arbitrary",)),
        name="nsa_slc_mask_decode",
    )(p_slc.reshape(n * g, w), expand, block_page)
    need = (need.reshape(n, g, LANE).sum(axis=1) > 0.5).astype(jnp.int32)[:, :n_pages]
    return out.reshape(n, g, l_pad), need


def _idx_score_kernel(pt_ref, cache_ref, qi_ref, knew_ref, gates_ref, o_ref, buf_ref, sem_ref,
                      *, layer, n_pages, p0):
    past = n_pages * PAGE_SIZE
    l_pad = past + LANE
    slot = _gather_step(pt_ref, cache_ref, layer, buf_ref, sem_ref, n_pages, PAGE_SIZE, mode="lanes")
    d = lax.broadcasted_iota(jnp.int32, (IDX_DIM, LANE), 0)
    lane = lax.broadcasted_iota(jnp.int32, (IDX_DIM, LANE), 1)
    k_row = jnp.broadcast_to(knew_ref[...], (IDX_DIM, LANE))
    k_col = jnp.sum(jnp.where(lane == d, k_row, 0.0), axis=1, keepdims=True)
    buf_ref[slot, :, past:l_pad] = jnp.where(lane == 0, k_col, 0.0)
    q8 = _q8(qi_ref, IDX_DIM).astype(BF16)
    lg = jnp.dot(q8, buf_ref[slot].astype(BF16), preferred_element_type=F32) * IDX_SCALE
    w_col = _head_column(gates_ref[...], 24, 1) * (IDX_HEADS ** -0.5)
    sc = jnp.sum(jnp.maximum(lg, 0.0) * w_col, axis=0, keepdims=True)
    key = lax.broadcasted_iota(jnp.int32, (1, l_pad), 1)
    o_ref[...] = jnp.where(key <= p0, sc, BELOW_ALL)


def idx_score_decode(page_table, cache_idx, layer, idxm, p0):
    n, n_pages = page_table.shape
    l_pad = n_pages * PAGE_SIZE + LANE
    gs = pltpu.PrefetchScalarGridSpec(
        num_scalar_prefetch=1,
        grid=(n,),
        in_specs=[pl.BlockSpec(memory_space=pl.ANY), _row_spec(IDX_HEADS * IDX_DIM, 0, 1),
                  _row_spec(LANE, 4, 1), _row_spec(LANE, 5, 1)],
        out_specs=pl.BlockSpec((None, 1, l_pad), lambda b, pt: (b, 0, 0)),
        scratch_shapes=[pltpu.VMEM((2, IDX_DIM, l_pad), F32), pltpu.SemaphoreType.DMA((2,))],
    )
    return pl.pallas_call(
        functools.partial(_idx_score_kernel, layer=layer, n_pages=n_pages, p0=p0),
        grid_spec=gs,
        out_shape=jax.ShapeDtypeStruct((n, 1, l_pad), F32),
        compiler_params=_cp(("arbitrary",)),
        name="dsa_idx_score_decode",
    )(page_table, cache_idx, idxm, idxm, idxm)


def _dsa_mask_kernel(sc_ref, o_ref, *, k_top, p0):
    key = _order_key(sc_ref[...])
    n, l_pad = key.shape
    idx = lax.broadcasted_iota(jnp.int32, key.shape, 1)
    col = (n, 1)

    def count(pred):
        return jnp.sum(jnp.where(pred, 1, 0), axis=1, keepdims=True)

    thr = _radix_kth(lambda cand: count(key >= cand), k_top, col)
    need = k_top - count(key > thr)
    cut = _tie_cut(lambda m: count((key == thr) & (idx < m)), need, col, max(1, int(l_pad).bit_length()))
    sel = ((key > thr) | ((key == thr) & (idx < cut))) & (idx <= p0)
    mask = jnp.where(sel, 1.0, 0.0)
    for g in range(DSA_KV):
        o_ref[:, g, :] = mask


def dsa_mask_decode(score, k_top, p0):
    n, _, l_pad = score.shape
    return pl.pallas_call(
        functools.partial(_dsa_mask_kernel, k_top=k_top, p0=p0),
        grid=(1,),
        in_specs=[pl.BlockSpec((n, l_pad), lambda i: (0, 0))],
        out_specs=pl.BlockSpec((n, DSA_KV, l_pad), lambda i: (0, 0, 0)),
        out_shape=jax.ShapeDtypeStruct((n, DSA_KV, l_pad), F32),
        compiler_params=_cp(("arbitrary",)),
        name="dsa_mask_decode",
    )(score.reshape(n, l_pad))


def _attend_rows(q8, kv_rows, mask_ref, n_keys):
    head = lax.broadcasted_iota(jnp.int32, (8, 1), 0)
    o = jnp.zeros((8, HEAD_DIM), F32)
    for g in range(NSA_KV):
        kb = _kv_rows(kv_rows, 0, n_keys, g).astype(BF16)
        vb = _kv_rows(kv_rows, 0, n_keys, NSA_KV + g).astype(BF16)
        s = lax.dot_general(q8, kb, (((1,), (1,)), ((), ())), preferred_element_type=F32) * ATTN_SCALE
        mk = mask_ref[g:g + 1, :] > 0.5
        s = jnp.where(mk, s, NEG_INF)
        m = jnp.max(s, axis=1, keepdims=True)
        e = jnp.where(mk, jnp.exp(s - m), 0.0)
        p = e / jnp.maximum(jnp.sum(e, axis=1, keepdims=True), 1e-30)
        og = jnp.dot(p.astype(BF16), vb, preferred_element_type=F32)
        o = jnp.where(head // NSA_REP == g, og, o)
    return o


def _attn_paged_kernel(pt_ref, need_ref, cache_ref, q_ref, kvnew_ref, mask_ref, gates_ref, o_ref, buf_ref, sem_ref,
                       *, layer, n_pages, branch):
    past = n_pages * PAGE_SIZE
    l_pad = past + LANE
    slot = _gather_step(pt_ref, cache_ref, layer, buf_ref, sem_ref, n_pages, PAGE_SIZE * KV_SLOTS,
                        need_ref=need_ref)
    buf_ref[slot, past * KV_SLOTS:l_pad * KV_SLOTS, :] = jnp.zeros((LANE * KV_SLOTS, HEAD_DIM), F32)
    buf_ref[slot, past * KV_SLOTS:(past + 1) * KV_SLOTS, :] = kvnew_ref[...]
    o = _attend_rows(_q8(q_ref, HEAD_DIM).astype(BF16), buf_ref.at[slot], mask_ref, l_pad)
    if branch is not None:
        o = o * _sigmoid(_head_column(gates_ref[...], branch, 3))
    o_ref[...] = o


def attn_paged_decode(page_table, need, cache, layer, q, q_blk, kvnew, mask, idxm, branch):
    n, n_pages = page_table.shape
    l_pad = n_pages * PAGE_SIZE + LANE
    gs = pltpu.PrefetchScalarGridSpec(
        num_scalar_prefetch=2,
        grid=(n,),
        in_specs=[pl.BlockSpec(memory_space=pl.ANY), _row_spec(BRANCH_W, q_blk, 2),
                  pl.BlockSpec((None, KV_SLOTS, HEAD_DIM), lambda b, *_: (b, 0, 0)),
                  pl.BlockSpec((None, NSA_KV, l_pad), lambda b, *_: (b, 0, 0)), _row_spec(LANE, 5, 2)],
        out_specs=pl.BlockSpec((None, NSA_HEADS, HEAD_DIM), lambda b, *_: (b, 0, 0)),
        scratch_shapes=[pltpu.VMEM((2, l_pad * KV_SLOTS, HEAD_DIM), F32), pltpu.SemaphoreType.DMA((2,))],
    )
    return pl.pallas_call(
        functools.partial(_attn_paged_kernel, layer=layer, n_pages=n_pages, branch=branch),
        grid_spec=gs,
        out_shape=jax.ShapeDtypeStruct((n, NSA_HEADS, HEAD_DIM), F32),
        compiler_params=_cp(("arbitrary",)),
        name="attn_paged_decode",
    )(page_table, need, cache, q, kvnew, mask, idxm)


def _attn_win_kernel(st_ref, q_ref, kvnew_ref, gates_ref, o_ref, buf_ref, mask_ref, *, wb):
    l_pad = wb + LANE
    buf_ref[0:wb * KV_SLOTS, :] = st_ref[...]
    buf_ref[wb * KV_SLOTS:l_pad * KV_SLOTS, :] = jnp.zeros((LANE * KV_SLOTS, HEAD_DIM), F32)
    buf_ref[wb * KV_SLOTS:(wb + 1) * KV_SLOTS, :] = kvnew_ref[...]
    key = lax.broadcasted_iota(jnp.int32, (NSA_KV, l_pad), 1)
    mask_ref[...] = jnp.where(key <= wb, 1.0, 0.0)
    o = _attend_rows(_q8(q_ref, HEAD_DIM).astype(BF16), buf_ref, mask_ref, l_pad)
    o_ref[...] = o * _sigmoid(_head_column(gates_ref[...], 2, 3))


def attn_win_decode(state, layer, q, kvnew, idxm):
    n, wb = state.shape[1], state.shape[2] // KV_SLOTS
    l_pad = wb + LANE
    return pl.pallas_call(
        functools.partial(_attn_win_kernel, wb=wb),
        grid=(n,),
        in_specs=[pl.BlockSpec((None, None, wb * KV_SLOTS, HEAD_DIM), lambda b: (layer, b, 0, 0)),
                  _row_spec(BRANCH_W, 0, 0), pl.BlockSpec((None, KV_SLOTS, HEAD_DIM), lambda b: (b, 0, 0)),
                  _row_spec(LANE, 5, 0)],
        out_specs=pl.BlockSpec((None, NSA_HEADS, HEAD_DIM), lambda b: (b, 0, 0)),
        out_shape=jax.ShapeDtypeStruct((n, NSA_HEADS, HEAD_DIM), F32),
        scratch_shapes=[pltpu.VMEM((l_pad * KV_SLOTS, HEAD_DIM), F32), pltpu.VMEM((NSA_KV, l_pad), F32)],
        compiler_params=_cp(("parallel",)),
        name="attn_win_decode",
    )(state, q, kvnew, idxm)


def _rope_tables(pos, head_dim):
    d_rot = head_dim // ROPE_FRACTION
    half = d_rot // 2
    inv_freq = jnp.exp(jnp.arange(half, dtype=F32) * (-2.0 * math.log(ROPE_THETA) / d_rot))
    ang = pos.astype(F32)[:, None] * inv_freq[None, :]
    cos, sin = jnp.cos(ang), jnp.sin(ang)
    lane = np.arange(LANE) % head_dim
    j = lane % half
    first = jnp.asarray(lane < half)[None, :]
    second = jnp.asarray((lane >= half) & (lane < d_rot))[None, :]
    c = jnp.where(first | second, cos[:, j], 1.0)
    s1 = jnp.where(first, -sin[:, j], 0.0)
    s2 = jnp.where(second, sin[:, j], 0.0)
    return (c, s1, s2), half


def _split_w_in(w_in):
    sizes = (NSA_HEADS * HEAD_DIM, KV_W, KV_W, KV_W, 3 * NSA_HEADS, DSA_HEADS * HEAD_DIM, KV_W,
             IDX_HEADS * IDX_DIM, IDX_DIM, IDX_HEADS, CONV_DIM, CONV_DIM, CONV_DIM, N_BRANCH * D_MODEL)
    offs = np.concatenate([[0], np.cumsum(sizes)])
    col = lambda i: w_in[:, int(offs[i]):int(offs[i + 1])]
    (q_a, cmp_kv, slc_kv, win_kv, gate_a, q_b, dsa_kv, q_i, k_i, w_i, cu, cb, cc, gm) = [col(i) for i in range(14)]
    d = w_in.shape[0]
    zeros = lambda n: jnp.zeros((d, n), w_in.dtype)
    w_q = jnp.concatenate([q_a, q_b], axis=1).astype(BF16)
    w_kv = jnp.concatenate([cmp_kv, slc_kv, win_kv, dsa_kv], axis=1).astype(BF16)
    w_idx = jnp.concatenate([q_i, k_i, zeros(LANE - IDX_DIM), gate_a, w_i, zeros(LANE - 3 * NSA_HEADS - IDX_HEADS)],
                            axis=1).astype(BF16)
    w_cg = jnp.concatenate([gm, cu, cb, cc], axis=1).astype(BF16)
    return w_q, w_kv, w_idx, w_cg


def _cmp_weights(w1, w2, pe):
    half = CMP_STRIDE * HEAD_DIM
    pairs = CMP_STRIDE // 2
    wa = w1[:, :half].reshape(2, pairs, 2 * HEAD_DIM, HEAD_DIM)
    wb = w1[:, half:].reshape(2, pairs, 2 * HEAD_DIM, HEAD_DIM)
    w1cat = jnp.concatenate([wa, wb], axis=-1).astype(BF16)
    pe8 = jnp.zeros((2, pairs, 8, 2 * HEAD_DIM), F32)
    pe8 = pe8.at[:, :, 0, :].set(pe[:, :CMP_STRIDE].reshape(2, pairs, 2 * HEAD_DIM))
    pe8 = pe8.at[:, :, 1, :].set(pe[:, CMP_STRIDE:].reshape(2, pairs, 2 * HEAD_DIM))
    return w1cat, w2.astype(BF16), pe8.astype(BF16)


def _overlap(n_cmp_rows, n_slc, seq_len):
    n_cmp = seq_len // CMP_STRIDE - 1
    c = np.arange(n_cmp_rows)
    c_start = c * CMP_STRIDE
    s_start = np.arange(n_slc) * SEL_BLOCK
    ov = ((c_start[:, None] < s_start[None, :] + SEL_BLOCK) & (c_start[:, None] + CMP_LEN > s_start[None, :])
          & (c[:, None] < n_cmp))
    return ov.astype(np.float32)


def _project(xn, wts, tabs128, half128, tabs64, half64):
    w_q, _, w_idx, w_cg = wts
    q = proj_rope(xn, w_q, tabs128, bn=1024, half=half128, rope_blocks=(True,) * 8, stacked=False)
    idxm = proj_rope(xn, w_idx, tabs64, bn=w_idx.shape[1], half=half64,
                     rope_blocks=(True,) * 5 + (False,), stacked=False)
    cg = matmul(xn, w_cg, bn=1024)
    return q, idxm, cg


def _finish_layer(x, branches, oc, cg, lw, layer, g_next):
    a1, a2, a3, ob = branches
    merged = merge_branches(a1, a2, a3, ob, oc, lw["w_branch"], layer, cg)
    x_mid, hn = outproj_residual(merged, lw["w_out"], layer, x, lw["g_mix_post"], lw["g_ffn_pre"])
    act = ffn_gate_up(hn, lw["w_gu"], layer)
    return ffn_down_residual(act, lw["w_down"], layer, x_mid, lw["g_ffn_post"], g_next)


def _prompt_layer(x, xn, lw, layer, kv_all, n, s, consts, g_next):
    tabs128, half128 = consts["rope_p"][:2]
    q, idxm, cg = _project(xn, lw["w_in"], *consts["rope_p"])
    kv_all = proj_rope_kv(xn, lw["w_in"][1], tabs128, kv_all, layer, half=half128)
    a1, p_slc = cmp_prompt(kv_all[0], layer, q, idxm, *lw["cmp"], consts["ov_p_t"], n, s)
    a2 = slc_prompt(q, kv_all[1], layer, p_slc, idxm, n, s)
    a3 = win_prompt(q, kv_all[2], layer, idxm, n, s)
    ob = dsa_prompt(q, kv_all[3], layer, idxm, n, s)
    oc, conv_state = conv_prompt(cg, lw["conv_w"], n, s)
    y, xn_next = _finish_layer(x, (a1, a2, a3, ob), oc, cg, lw, layer, g_next)
    state = (idxm[:, IDX_HEADS * IDX_DIM:IDX_HEADS * IDX_DIM + IDX_DIM].reshape(n, s, IDX_DIM), conv_state)
    return y, xn_next, kv_all, state


def _sample_layer(x, xn, lw, layer, caches, page_table, consts, g_next):
    n = x.shape[0]
    p0 = consts["p0"]
    tabs128, half128 = consts["rope_s"][:2]
    q, idxm, cg = _project(xn, lw["w_in"], *consts["rope_s"])
    kv4 = proj_rope(xn, lw["w_in"][1], tabs128, bn=KV_W, half=half128,
                    rope_blocks=(True, True, False, False), stacked=True)
    cache_cmp_x, cache_slc, state_win, cache_dsa, cache_idx, state_conv = caches
    q3 = q.reshape(n, 1, -1)
    idx3 = idxm.reshape(n, 1, -1)
    new_row = lambda i: kv4[i].reshape(n, KV_SLOTS, HEAD_DIM)
    o_cmp, p_slc = cmp_decode(page_table, cache_cmp_x, layer, q3, idx3, *lw["cmp"], consts["ov_s"], p0)
    slc_mask, slc_need = slc_mask_decode(p_slc, consts["expand"], consts["block_page"], consts["n_slc_s"],
                                         page_table.shape[1], p0)
    o_slc = attn_paged_decode(page_table, slc_need, cache_slc, layer, q3, 0, new_row(1), slc_mask, idx3, 1)
    o_win = attn_win_decode(state_win, layer, q3, new_row(2), idx3)
    score = idx_score_decode(page_table, cache_idx, layer, idx3, p0)
    dsa_mask = dsa_mask_decode(score, min(DSA_TOPK, (p0 + 1) // 4), p0)
    o_dsa = attn_paged_decode(page_table, jnp.ones_like(page_table), cache_dsa, layer, q3, 1, new_row(3),
                              dsa_mask, idx3, None)
    oc, conv_state = conv_decode(cg, state_conv[layer], lw["conv_w"])
    flat = lambda a: a.reshape(n, BRANCH_W)
    y, xn_next = _finish_layer(x, (flat(o_cmp), flat(o_slc), flat(o_win), flat(o_dsa)), oc, cg, lw, layer, g_next)
    kv5 = lambda a: a.reshape(n, 1, 2, NSA_KV, HEAD_DIM)
    win_all = jnp.concatenate([consts["state_win"][layer], kv5(kv4[2])], axis=1)
    keep = min(WINDOW, win_all.shape[1])
    state = (kv5(kv4[0]), kv5(kv4[1]), win_all[:, win_all.shape[1] - keep:], kv5(kv4[3]),
             idxm[:, IDX_HEADS * IDX_DIM:IDX_HEADS * IDX_DIM + IDX_DIM].reshape(n, 1, IDX_DIM), conv_state)
    return y, xn_next, state


def kernel(x_prompt, x_sample, cache_nsa_cmp_kv, cache_nsa_slc_kv, state_nsa_win_kv, cache_dsa_kv, cache_dsa_idx_k, state_conv, page_table, norm_mix_pre, norm_mix_post, norm_ffn_pre, norm_ffn_post, w_in, cmp_w1, cmp_w2, cmp_pe, conv_w, w_branch, w_out, ffn_w_gate_up, ffn_w_down):
    n_p, s, d = x_prompt.shape
    n_s = x_sample.shape[0]
    depth = w_in.shape[0]
    n_pages = page_table.shape[1]
    n_pool = cache_nsa_cmp_kv.shape[1]
    p0 = n_pages * PAGE_SIZE
    l_s = p0 + 1
    l_pad = p0 + LANE
    n_slc_s = -(-l_s // SEL_BLOCK)
    n_slc_pad = -(-n_slc_s // LANE) * LANE
    nc_s = p0 // CMP_STRIDE

    tabs128_p, half128 = _rope_tables(jnp.arange(s, dtype=jnp.int32), HEAD_DIM)
    tabs64_p, half64 = _rope_tables(jnp.arange(s, dtype=jnp.int32), IDX_DIM)
    tabs128_s, _ = _rope_tables(jnp.full((n_s,), p0, jnp.int32), HEAD_DIM)
    tabs64_s, _ = _rope_tables(jnp.full((n_s,), p0, jnp.int32), IDX_DIM)
    key_block = np.arange(l_pad) // SEL_BLOCK
    expand = ((key_block[None, :] == np.arange(n_slc_pad)[:, None]) & (np.arange(l_pad)[None, :] <= p0))
    consts = {
        "p0": p0,
        "n_slc_s": n_slc_s,
        "state_win": state_nsa_win_kv,
        "rope_p": (tabs128_p, half128, tabs64_p, half64),
        "rope_s": (tabs128_s, half128, tabs64_s, half64),
        "ov_p_t": jnp.asarray(_overlap(s // CMP_STRIDE, -(-s // SEL_BLOCK), s).T),
        "ov_s": jnp.asarray(_overlap(nc_s, n_slc_pad, l_s) * (np.arange(n_slc_pad) < n_slc_s)[None, :]),
        "expand": jnp.asarray(expand.astype(np.float32)).astype(BF16),
        "block_page": jnp.asarray((np.arange(n_slc_pad)[:, None] // (PAGE_SIZE // SEL_BLOCK)
                                   == np.arange(LANE)[None, :]).astype(np.float32)).astype(BF16),
    }
    paged = lambda c: c.reshape(depth, n_pool, PAGE_SIZE * KV_SLOTS, HEAD_DIM)
    cmp_chunks = cache_nsa_cmp_kv.reshape(depth, n_pool, PAGE_SIZE // CMP_STRIDE, CHUNK_ROWS, HEAD_DIM)
    caches = (cmp_chunks, paged(cache_nsa_slc_kv),
              state_nsa_win_kv.reshape(depth, n_s, -1, HEAD_DIM), paged(cache_dsa_kv),
              jnp.swapaxes(cache_dsa_idx_k, 2, 3), state_conv)

    x_p = x_prompt.reshape(n_p * s, d)
    x_s = x_sample.reshape(n_s, d)
    xn_p = rmsnorm(x_p, norm_mix_pre[0])
    xn_s = rmsnorm(x_s, norm_mix_pre[0])
    stacked = {"w_branch": w_branch.astype(BF16), "w_out": w_out.astype(BF16),
               "w_gu": ffn_w_gate_up.astype(BF16), "w_down": ffn_w_down.astype(BF16)}
    kv_all = [jnp.zeros((depth, n_p * s * KV_SLOTS, HEAD_DIM), F32) for _ in range(4)]
    new_p, new_s = [], []
    for l in range(depth):
        lw = {
            "w_in": _split_w_in(w_in[l]),
            "cmp": _cmp_weights(cmp_w1[l], cmp_w2[l], cmp_pe[l]),
            "conv_w": conv_w[l],
            "g_mix_post": norm_mix_post[l], "g_ffn_pre": norm_ffn_pre[l], "g_ffn_post": norm_ffn_post[l],
            **stacked,
        }
        g_next = norm_mix_pre[l + 1] if l + 1 < depth else norm_mix_pre[l]
        x_p, xn_p, kv_all, st_p = _prompt_layer(x_p, xn_p, lw, l, kv_all, n_p, s, consts, g_next)
        x_s, xn_s, st_s = _sample_layer(x_s, xn_s, lw, l, caches, page_table, consts, g_next)
        new_p.append(st_p)
        new_s.append(st_s)
    kv6 = [a.reshape(depth, n_p, s, 2, NSA_KV, HEAD_DIM) for a in kv_all]
    keep = min(WINDOW, s)
    p_out = [kv6[0], kv6[1], kv6[2][:, :, s - keep:], kv6[3]] + [jnp.stack([st[i] for st in new_p]) for i in range(2)]
    s_out = [jnp.stack([st[i] for st in new_s]) for i in range(6)]
    return (x_p.reshape(n_p, s, d), x_s.reshape(n_s, 1, d), *p_out, *s_out)
```
